```python
import math
import jax, jax.numpy as jnp
from jax import lax
import numpy as np

D_MODEL = 1024
BATCH = 32
SEQ = 2048
DEPTH = 4

CTX_LEN = 256
GRID_W = 64
N_MIXERS = 3
LAYER_TYPES = tuple(i % N_MIXERS for i in range(DEPTH))
N_ATTN = LAYER_TYPES.count(0)
N_S5 = LAYER_TYPES.count(1)
N_HG = LAYER_TYPES.count(2)

DA_HEADS = 8
DA_HEAD_DIM = 64
DA_V_DIM = 2 * DA_HEAD_DIM
Q_BLOCK = 128
ROPE_THETA = 10000.0
S5_GROUP = 16
S5_GROUPS = D_MODEL // S5_GROUP
S5_STATE = 64
HG_HEADS = 8
HG_KEY = D_MODEL // HG_HEADS
HG_VAL = D_MODEL // HG_HEADS
HG_CHUNK = 32
PEER_HEADS = 8
PEER_NKEYS = 128
PEER_EXPERTS = PEER_NKEYS * PEER_NKEYS
PEER_QDIM = 256
PEER_TOPK = 16
PEER_BLOCK = 128
LN_EPS = 1e-5
RMS_EPS = 1e-6
DN_ALPHA = (2 * DEPTH) ** 0.25
DN_BETA = (8 * DEPTH) ** -0.25

kernel_name = "hybrid_diffattn_s5_hgrn2_peer_dit"

F32 = jnp.float32


def layer_norm(x, g, b):
    xf = x.astype(F32)
    mu = jnp.mean(xf, -1, keepdims=True)
    var = jnp.mean(jnp.square(xf - mu), -1, keepdims=True)
    return ((xf - mu) * lax.rsqrt(var + LN_EPS)).astype(x.dtype) * g + b


def rms_norm(x):
    xf = x.astype(F32)
    return (xf * lax.rsqrt(jnp.mean(xf * xf, -1, keepdims=True) + RMS_EPS)).astype(x.dtype)


def axial_rope(length, dim):
    rows = length // GRID_W
    row = jnp.repeat(jnp.arange(rows, dtype=F32), GRID_W)
    col = jnp.tile(jnp.arange(GRID_W, dtype=F32), rows)
    n_freq = dim // 4
    inv = ROPE_THETA ** (-jnp.arange(n_freq, dtype=F32) / n_freq)
    ang = jnp.concatenate([row[:, None] * inv, col[:, None] * inv], axis=-1)
    return jnp.cos(ang), jnp.sin(ang)


def apply_rope(x, cos, sin):
    c = cos[None, :, None, None].astype(x.dtype)
    s = sin[None, :, None, None].astype(x.dtype)
    x1, x2 = x[..., 0::2], x[..., 1::2]
    return jnp.stack([x1 * c - x2 * s, x1 * s + x2 * c], axis=-1).reshape(x.shape)


def diff_softmax_mix(q, k, v, lam):
    s = jnp.einsum('bqhcd,bkhcd->bhcqk', q, k).astype(F32)
    p = jax.nn.softmax(s, axis=-1)
    a = p[:, :, 0] - lam * p[:, :, 1]
    return jnp.einsum('bhqk,bkhe->bqhe', a.astype(v.dtype), v)


def diff_attention(u_ctx, u_lat, w_in, w_out, lam_q, lam_k, subln_g, lam_init, cos, sin, need_ctx):
    H, d = DA_HEADS, DA_HEAD_DIM

    def project(u):
        b_, n, _ = u.shape
        q, k, v = jnp.split(u @ w_in, 3, axis=-1)
        return (q.reshape(b_, n, H, 2, d) * d ** -0.5,
                k.reshape(b_, n, H, 2, d),
                v.reshape(b_, n, H, 2 * d))

    qc, kc, vc = project(u_ctx)
    ql, kl, vl = project(u_lat)
    ql, kl = apply_rope(ql, cos, sin), apply_rope(kl, cos, sin)
    lq, lk = lam_q.astype(F32), lam_k.astype(F32)
    lam = jnp.exp(jnp.sum(lq[0] * lk[0])) - jnp.exp(jnp.sum(lq[1] * lk[1])) + lam_init
    k_all = jnp.concatenate([kl, kc], axis=1)
    v_all = jnp.concatenate([vl, vc], axis=1)
    b_, L = ql.shape[:2]
    nb = L // Q_BLOCK
    q_blocks = jnp.moveaxis(ql.reshape(b_, nb, Q_BLOCK, H, 2, d), 1, 0)
    o_lat = lax.map(lambda qb: diff_softmax_mix(qb, k_all, v_all, lam), q_blocks)
    o_lat = jnp.moveaxis(o_lat, 0, 1).reshape(b_, L, H, 2 * d)

    def finish(o):
        b2, n = o.shape[:2]
        o = rms_norm(o) * subln_g * (1.0 - lam_init)
        return o.reshape(b2, n, D_MODEL) @ w_out

    o_ctx = finish(diff_softmax_mix(qc, kc, vc, lam)) if need_ctx else None
    return finish(o_lat), o_ctx


def s5_discretize(lam_re, lam_im, log_dt, b_re, b_im):
    lam_re, lam_im = lam_re.astype(F32), lam_im.astype(F32)
    b_re, b_im = b_re.astype(F32), b_im.astype(F32)
    dt = jnp.exp(log_dt.astype(F32))[:, None]
    mag = jnp.exp(lam_re * dt)
    abar_re, abar_im = mag * jnp.cos(lam_im * dt), mag * jnp.sin(lam_im * dt)
    nr, ni = abar_re - 1.0, abar_im
    den = lam_re * lam_re + lam_im * lam_im
    k_re = (nr * lam_re + ni * lam_im) / den
    k_im = (ni * lam_re - nr * lam_im) / den
    bb_re = k_re[..., None] * b_re - k_im[..., None] * b_im
    bb_im = k_re[..., None] * b_im + k_im[..., None] * b_re
    return abar_re, abar_im, bb_re, bb_im


def complex_affine_combine(e1, e2):
    a1r, a1i, b1r, b1i = e1
    a2r, a2i, b2r, b2i = e2
    return (a2r * a1r - a2i * a1i, a2r * a1i + a2i * a1r,
            a2r * b1r - a2i * b1i + b2r, a2r * b1i + a2i * b1r + b2i)


def s5_scan(u, h0, abar_re, abar_im, bb_re, bb_im, reverse):
    br = jnp.einsum('gpm,bngm->bngp', bb_re, u)
    bi = jnp.einsum('gpm,bngm->bngp', bb_im, u)
    if h0 is not None:
        edge = -1 if reverse else 0
        h0r, h0i = h0
        br = br.at[:, edge].add(abar_re * h0r - abar_im * h0i)
        bi = bi.at[:, edge].add(abar_re * h0i + abar_im * h0r)
    n = u.shape[1]
    ar = jnp.broadcast_to(abar_re, (1, n) + abar_re.shape)
    ai = jnp.broadcast_to(abar_im, (1, n) + abar_im.shape)
    _, _, xr, xi = lax.associative_scan(complex_affine_combine, (ar, ai, br, bi), reverse=reverse, axis=1)
    return xr, xi


def s5_readout(xr, xi, c_re, c_im):
    return (jnp.einsum('gmp,bngp->bngm', c_re.astype(F32), xr)
            - jnp.einsum('gmp,bngp->bngm', c_im.astype(F32), xi))


def s5_mixer(u_ctx, u_lat, lam_re, lam_im, log_dt, b_re, b_im, c_re, c_im, d_skip, w_glu, need_ctx):
    groups = lambda u: u.astype(F32).reshape(u.shape[0], u.shape[1], S5_GROUPS, S5_GROUP)
    uc, ul = groups(u_ctx), groups(u_lat)
    y_ctx = jnp.zeros_like(uc)
    y_lat = jnp.zeros_like(ul)
    for dr, rev in ((0, False), (1, True)):
        ab_re, ab_im, bb_re, bb_im = s5_discretize(lam_re[dr], lam_im[dr], log_dt[dr], b_re[dr], b_im[dr])
        xr_c, xi_c = s5_scan(uc, None, ab_re, ab_im, bb_re, bb_im, rev)
        edge = 0 if rev else -1
        xr_l, xi_l = s5_scan(ul, (xr_c[:, edge], xi_c[:, edge]), ab_re, ab_im, bb_re, bb_im, rev)
        y_lat = y_lat + s5_readout(xr_l, xi_l, c_re[dr], c_im[dr])
        y_ctx = y_ctx + s5_readout(xr_c, xi_c, c_re[dr], c_im[dr])

    def finish(y, u):
        y = y.reshape(u.shape).astype(u.dtype) + d_skip * u
        z = jax.nn.gelu(y)
        val, gate = jnp.split(z @ w_glu, 2, axis=-1)
        return val * jax.nn.sigmoid(gate)

    o_ctx = finish(y_ctx, u_ctx) if need_ctx else None
    return finish(y_lat, u_lat), o_ctx


def hgrn2_chunk_scan(q, k, v, log_f, s0):
    out_dtype = v.dtype
    b_, n, H, K = q.shape
    C = HG_CHUNK
    nc = n // C
    to_chunks = lambda t: jnp.moveaxis(t.astype(F32).reshape(b_, nc, C, H, t.shape[-1]), 1, 0)
    causal = jnp.tril(jnp.ones((C, C), dtype=bool))[None, :, :, None, None]

    def step(S, inp):
        qc, kc, vc, gc = inp
        cum = jnp.cumsum(gc, axis=1)
        diff = cum[:, :, None] - cum[:, None]
        decay = jnp.exp(jnp.where(causal, diff, -jnp.inf))
        att = jnp.sum(qc[:, :, None] * kc[:, None] * decay, axis=-1)
        o = jnp.einsum('btsh,bshv->bthv', att, vc)
        o = o + jnp.einsum('bthk,bhkv->bthv', qc * jnp.exp(cum), S)
        last = cum[:, -1]
        S = jnp.exp(last)[..., None] * S + jnp.einsum('bshk,bshv->bhkv', kc * jnp.exp(last[:, None] - cum), vc)
        return S, o

    S, o = lax.scan(step, s0, (to_chunks(q), to_chunks(k), to_chunks(v), to_chunks(log_f)))
    o = jnp.moveaxis(o, 0, 1).reshape(b_, n, H, v.shape[-1])
    return o.astype(out_dtype), S


def hgrn2_mixer(u_ctx, u_lat, w_in, w_out, norm_g, lb, need_ctx):
    H, K, V = HG_HEADS, HG_KEY, HG_VAL
    log_lb, log_1m_lb = jnp.log(lb), jnp.log1p(-lb)

    def project(u):
        b_, n, _ = u.shape
        q, f_fw, f_bw, inp, gate = jnp.split(u @ w_in, 5, axis=-1)
        q = jax.nn.silu(q).reshape(b_, n, H, K)
        inp = inp.reshape(b_, n, H, V)
        dirs = []
        for f in (f_fw, f_bw):
            f = f.astype(F32)
            log_f = jnp.logaddexp(log_lb, log_1m_lb + jax.nn.log_sigmoid(f))
            one_m_f = (1.0 - lb) * jax.nn.sigmoid(-f)
            dirs.append((one_m_f.reshape(b_, n, H, K), log_f.reshape(b_, n, H, K)))
        return q, inp, dirs, gate

    qc, ic, dc, gc = project(u_ctx)
    ql, il, dl, gl = project(u_lat)
    s0 = jnp.zeros((u_lat.shape[0], H, K, V), F32)
    o_ctx = jnp.zeros(ic.shape, ic.dtype)
    o_lat = jnp.zeros(il.shape, il.dtype)
    for (kc_, lfc), (kl_, lfl), rev in zip(dc, dl, (False, True)):
        fl = (lambda t: jnp.flip(t, axis=1)) if rev else (lambda t: t)
        oc_d, s_ctx = hgrn2_chunk_scan(fl(qc), fl(kc_), fl(ic), fl(lfc), s0)
        ol_d, _ = hgrn2_chunk_scan(fl(ql), fl(kl_), fl(il), fl(lfl), s_ctx)
        o_ctx = o_ctx + fl(oc_d)
        o_lat = o_lat + fl(ol_d)

    def finish(o, gate):
        b2, n = o.shape[:2]
        o = rms_norm(o) * norm_g.reshape(H, V)
        return (o.reshape(b2, n, D_MODEL) * jax.nn.silu(gate)) @ w_out

    out_ctx = finish(o_ctx, gc) if need_ctx else None
    return finish(o_lat, gl), out_ctx


def peer_ffn(u, w_q, sub_keys, u_tab, v_tab):
    b_, n, D = u.shape
    H, KT, NK = PEER_HEADS, PEER_TOPK, PEER_NKEYS
    tok = u.reshape((b_ * n) // PEER_BLOCK, PEER_BLOCK, D)

    def block(xb):
        q = (xb @ w_q).reshape(PEER_BLOCK, H, 2, PEER_QDIM // 2)
        s = jnp.einsum('thcd,ckd->thck', q, sub_keys).astype(F32)
        s1, i1 = lax.top_k(s[:, :, 0], KT)
        s2, i2 = lax.top_k(s[:, :, 1], KT)
        cand_s = (s1[..., :, None] + s2[..., None, :]).reshape(PEER_BLOCK, H, KT * KT)
        cand_i = (i1[..., :, None] * NK + i2[..., None, :]).reshape(PEER_BLOCK, H, KT * KT)
        top_s, pos = lax.top_k(cand_s, KT)
        idx = jnp.take_along_axis(cand_i, pos, axis=-1)
        g = jax.nn.softmax(top_s, axis=-1)
        act = jax.nn.gelu(jnp.einsum('thkd,td->thk', u_tab[idx], xb))
        return jnp.einsum('thk,thkd->td', (g * act).astype(xb.dtype), v_tab[idx])

    return lax.map(block, tok).reshape(b_, n, D)


def setup_inputs(seed: int = 0) -> dict:
    key = jax.random.key(seed)
    ks = iter(jax.random.split(key, 40))
    nrm = lambda shape, scale: scale * jax.random.normal(next(ks), shape, F32)
    D = D_MODEL
    G, P, M = S5_GROUPS, S5_STATE, S5_GROUP
    x = nrm((BATCH, SEQ, D), 1.0)
    c = nrm((BATCH, D), 1.0)
    ctx = nrm((BATCH, CTX_LEN, D), 1.0)
    c_ctx = nrm((D,), 1.0)
    ada_w = nrm((DEPTH, D, 6 * D), 0.5 * D ** -0.5)
    ada_b = nrm((DEPTH, 6 * D), 0.02)
    ln_g = 1.0 + nrm((DEPTH, 2, D), 0.02)
    ln_b = nrm((DEPTH, 2, D), 0.02)
    da_w_in = nrm((N_ATTN, D, 3 * D), D ** -0.5)
    da_w_out = nrm((N_ATTN, D, D), DN_BETA * D ** -0.5)
    da_lam_q = nrm((N_ATTN, 2, DA_HEAD_DIM), 0.1)
    da_lam_k = nrm((N_ATTN, 2, DA_HEAD_DIM), 0.1)
    da_subln = 1.0 + nrm((N_ATTN, DA_V_DIM), 0.02)
    s5_lam_re = -0.5 + nrm((N_S5, 2, G, P), 0.01)
    s5_lam_im = math.pi * jnp.arange(P, dtype=F32) + nrm((N_S5, 2, G, P), 0.01)
    s5_log_dt = jax.random.uniform(next(ks), (N_S5, 2, G), F32, math.log(1e-3), math.log(1e-1))
    s5_b_re = nrm((N_S5, 2, G, P, M), (2 * M) ** -0.5)
    s5_b_im = nrm((N_S5, 2, G, P, M), (2 * M) ** -0.5)
    s5_c_re = nrm((N_S5, 2, G, M, P), (2 * P) ** -0.5)
    s5_c_im = nrm((N_S5, 2, G, M, P), (2 * P) ** -0.5)
    s5_d = nrm((N_S5, D), 1.0)
    s5_w_glu = jnp.concatenate([nrm((N_S5, D, D), DN_BETA * D ** -0.5), nrm((N_S5, D, D), D ** -0.5)], axis=-1)
    hg_w_in = nrm((N_HG, D, 5 * D), D ** -0.5)
    hg_w_out = nrm((N_HG, D, D), DN_BETA * D ** -0.5)
    hg_norm = 1.0 + nrm((N_HG, D), 0.02)
    hg_lb = nrm((DEPTH, D), 0.1)
    peer_wq = nrm((DEPTH, D, PEER_HEADS * PEER_QDIM), D ** -0.5)
    peer_keys = nrm((DEPTH, 2, PEER_NKEYS, PEER_QDIM // 2), (PEER_QDIM // 2) ** -0.5)
    peer_u = nrm((DEPTH, PEER_EXPERTS, D), D ** -0.5)
    peer_v = nrm((DEPTH, PEER_EXPERTS, D), DN_BETA * PEER_HEADS ** -0.5)
    return {"x": x, "c": c, "ctx": ctx, "c_ctx": c_ctx, "ada_w": ada_w, "ada_b": ada_b,
            "ln_g": ln_g, "ln_b": ln_b, "da_w_in": da_w_in, "da_w_out": da_w_out,
            "da_lam_q": da_lam_q, "da_lam_k": da_lam_k, "da_subln": da_subln,
            "s5_lam_re": s5_lam_re, "s5_lam_im": s5_lam_im, "s5_log_dt": s5_log_dt,
            "s5_b_re": s5_b_re, "s5_b_im": s5_b_im, "s5_c_re": s5_c_re, "s5_c_im": s5_c_im,
            "s5_d": s5_d, "s5_w_glu": s5_w_glu, "hg_w_in": hg_w_in, "hg_w_out": hg_w_out,
            "hg_norm": hg_norm, "hg_lb": hg_lb, "peer_wq": peer_wq, "peer_keys": peer_keys,
            "peer_u": peer_u, "peer_v": peer_v}


def reference(x, c, ctx, c_ctx, ada_w, ada_b, ln_g, ln_b, da_w_in, da_w_out, da_lam_q, da_lam_k, da_subln,
              s5_lam_re, s5_lam_im, s5_log_dt, s5_b_re, s5_b_im, s5_c_re, s5_c_im, s5_d, s5_w_glu,
              hg_w_in, hg_w_out, hg_norm, hg_lb, peer_wq, peer_keys, peer_u, peer_v):
    L = x.shape[1]
    cos, sin = axial_rope(L, DA_HEAD_DIM)
    s_c = jax.nn.silu(c)
    s_ctx = jax.nn.silu(c_ctx)
    lb_soft = jax.nn.softmax(hg_lb.astype(F32), axis=0)
    lb_all = jnp.cumsum(lb_soft, axis=0) - lb_soft[0]
    h, hc = x, ctx
    for i in range(DEPTH):
        kind, slot = LAYER_TYPES[i], i // N_MIXERS
        need_ctx = i < DEPTH - 1
        mod = (s_c @ ada_w[i] + ada_b[i])[:, None, :]
        mod_c = s_ctx @ ada_w[i] + ada_b[i]
        sh1, sc1, g1, sh2, sc2, g2 = jnp.split(mod, 6, axis=-1)
        csh1, csc1, cg1, csh2, csc2, cg2 = jnp.split(mod_c, 6, axis=-1)
        u = h * (1.0 + sc1) + sh1
        uc = hc * (1.0 + csc1) + csh1
        if kind == 0:
            lam_init = 0.8 - 0.6 * math.exp(-0.3 * i)
            o, oc = diff_attention(uc, u, da_w_in[slot], da_w_out[slot], da_lam_q[slot], da_lam_k[slot],
                                   da_subln[slot], lam_init, cos, sin, need_ctx)
        elif kind == 1:
            o, oc = s5_mixer(uc, u, s5_lam_re[slot], s5_lam_im[slot], s5_log_dt[slot], s5_b_re[slot],
                             s5_b_im[slot], s5_c_re[slot], s5_c_im[slot], s5_d[slot], s5_w_glu[slot], need_ctx)
        else:
            o, oc = hgrn2_mixer(uc, u, hg_w_in[slot], hg_w_out[slot], hg_norm[slot], lb_all[i], need_ctx)
        h = layer_norm(DN_ALPHA * h + g1 * o, ln_g[i, 0], ln_b[i, 0])
        f = peer_ffn(h * (1.0 + sc2) + sh2, peer_wq[i], peer_keys[i], peer_u[i], peer_v[i])
        h = layer_norm(DN_ALPHA * h + g2 * f, ln_g[i, 1], ln_b[i, 1])
        if need_ctx:
            hc = layer_norm(DN_ALPHA * hc + cg1 * oc, ln_g[i, 0], ln_b[i, 0])
            fc = peer_ffn(hc * (1.0 + csc2) + csh2, peer_wq[i], peer_keys[i], peer_u[i], peer_v[i])
            hc = layer_norm(DN_ALPHA * hc + cg2 * fc, ln_g[i, 1], ln_b[i, 1])
    return h
```

```python
import functools
import math
import jax, jax.numpy as jnp
from jax import lax
import numpy as np
from jax.experimental import pallas as pl
from jax.experimental.pallas import tpu as pltpu

D_MODEL = 1024
BATCH = 32
SEQ = 2048
DEPTH = 4

CTX_LEN = 256
GRID_W = 64
N_MIXERS = 3
LAYER_TYPES = tuple(i % N_MIXERS for i in range(DEPTH))
N_ATTN = LAYER_TYPES.count(0)
N_S5 = LAYER_TYPES.count(1)
N_HG = LAYER_TYPES.count(2)

DA_HEADS = 8
DA_HEAD_DIM = 64
DA_V_DIM = 2 * DA_HEAD_DIM
Q_BLOCK = 128
ROPE_THETA = 10000.0
S5_GROUP = 16
S5_GROUPS = D_MODEL // S5_GROUP
S5_STATE = 64
HG_HEADS = 8
HG_KEY = D_MODEL // HG_HEADS
HG_VAL = D_MODEL // HG_HEADS
HG_CHUNK = 32
PEER_HEADS = 8
PEER_NKEYS = 128
PEER_EXPERTS = PEER_NKEYS * PEER_NKEYS
PEER_QDIM = 256
PEER_TOPK = 16
PEER_BLOCK = 128
LN_EPS = 1e-5
RMS_EPS = 1e-6
DN_ALPHA = (2 * DEPTH) ** 0.25
DN_BETA = (8 * DEPTH) ** -0.25

F32 = jnp.float32


def _res_ln_kernel(h_ref, o_ref, gate_ref, g_ref, b_ref, out_ref):
    y = DN_ALPHA * h_ref[0] + gate_ref[0] * o_ref[0]
    mu = jnp.mean(y, -1, keepdims=True)
    yc = y - mu
    var = jnp.mean(yc * yc, -1, keepdims=True)
    out_ref[0] = yc * lax.rsqrt(var + LN_EPS) * g_ref[...] + b_ref[...]


def residual_layer_norm(h, o, gate, g, b, block_n=512):
    b_, n, d = h.shape
    bn = min(block_n, n)
    per_batch_gate = gate.shape[0] == b_
    gate_map = (lambda i, j: (i, 0, 0)) if per_batch_gate else (lambda i, j: (0, 0, 0))
    return pl.pallas_call(
        _res_ln_kernel,
        grid=(b_, n // bn),
        in_specs=[pl.BlockSpec((1, bn, d), lambda i, j: (i, j, 0)),
                  pl.BlockSpec((1, bn, d), lambda i, j: (i, j, 0)),
                  pl.BlockSpec((1, 1, d), gate_map),
                  pl.BlockSpec((1, d), lambda i, j: (0, 0)),
                  pl.BlockSpec((1, d), lambda i, j: (0, 0))],
        out_specs=pl.BlockSpec((1, bn, d), lambda i, j: (i, j, 0)),
        out_shape=jax.ShapeDtypeStruct(h.shape, h.dtype),
        name="residual_layer_norm",
    )(h, o, gate, g.reshape(1, d), b.reshape(1, d))


def rms_norm(x):
    xf = x.astype(F32)
    return (xf * lax.rsqrt(jnp.mean(xf * xf, -1, keepdims=True) + RMS_EPS)).astype(x.dtype)


def axial_rope(length, dim):
    rows = length // GRID_W
    row = jnp.repeat(jnp.arange(rows, dtype=F32), GRID_W)
    col = jnp.tile(jnp.arange(GRID_W, dtype=F32), rows)
    n_freq = dim // 4
    inv = ROPE_THETA ** (-jnp.arange(n_freq, dtype=F32) / n_freq)
    ang = jnp.concatenate([row[:, None] * inv, col[:, None] * inv], axis=-1)
    return jnp.cos(ang), jnp.sin(ang)


def apply_rope(x, cos, sin):
    c = cos[None, :, None, None].astype(x.dtype)
    s = sin[None, :, None, None].astype(x.dtype)
    x1, x2 = x[..., 0::2], x[..., 1::2]
    return jnp.stack([x1 * c - x2 * s, x1 * s + x2 * c], axis=-1).reshape(x.shape)


def diff_softmax_mix(q, k, v, lam):
    s = jnp.einsum('bqhcd,bkhcd->bhcqk', q, k).astype(F32)
    p = jax.nn.softmax(s, axis=-1)
    a = p[:, :, 0] - lam * p[:, :, 1]
    return jnp.einsum('bhqk,bkhe->bqhe', a.astype(v.dtype), v)


def diff_attention(u_ctx, u_lat, w_in, w_out, lam_q, lam_k, subln_g, lam_init, cos, sin, need_ctx):
    H, d = DA_HEADS, DA_HEAD_DIM

    def project(u):
        b_, n, _ = u.shape
        q, k, v = jnp.split(u @ w_in, 3, axis=-1)
        return (q.reshape(b_, n, H, 2, d) * d ** -0.5,
                k.reshape(b_, n, H, 2, d),
                v.reshape(b_, n, H, 2 * d))

    qc, kc, vc = project(u_ctx)
    ql, kl, vl = project(u_lat)
    ql, kl = apply_rope(ql, cos, sin), apply_rope(kl, cos, sin)
    lq, lk = lam_q.astype(F32), lam_k.astype(F32)
    lam = jnp.exp(jnp.sum(lq[0] * lk[0])) - jnp.exp(jnp.sum(lq[1] * lk[1])) + lam_init
    k_all = jnp.concatenate([kl, kc], axis=1)
    v_all = jnp.concatenate([vl, vc], axis=1)
    b_, L = ql.shape[:2]
    nb = L // Q_BLOCK
    q_blocks = jnp.moveaxis(ql.reshape(b_, nb, Q_BLOCK, H, 2, d), 1, 0)
    o_lat = lax.map(lambda qb: diff_softmax_mix(qb, k_all, v_all, lam), q_blocks)
    o_lat = jnp.moveaxis(o_lat, 0, 1).reshape(b_, L, H, 2 * d)

    def finish(o):
        b2, n = o.shape[:2]
        o = rms_norm(o) * subln_g * (1.0 - lam_init)
        return o.reshape(b2, n, D_MODEL) @ w_out

    o_ctx = finish(diff_softmax_mix(qc, kc, vc, lam)) if need_ctx else None
    return finish(o_lat), o_ctx


def s5_discretize(lam_re, lam_im, log_dt, b_re, b_im):
    lam_re, lam_im = lam_re.astype(F32), lam_im.astype(F32)
    b_re, b_im = b_re.astype(F32), b_im.astype(F32)
    dt = jnp.exp(log_dt.astype(F32))[:, None]
    mag = jnp.exp(lam_re * dt)
    abar_re, abar_im = mag * jnp.cos(lam_im * dt), mag * jnp.sin(lam_im * dt)
    nr, ni = abar_re - 1.0, abar_im
    den = lam_re * lam_re + lam_im * lam_im
    k_re = (nr * lam_re + ni * lam_im) / den
    k_im = (ni * lam_re - nr * lam_im) / den
    bb_re = k_re[..., None] * b_re - k_im[..., None] * b_im
    bb_im = k_re[..., None] * b_im + k_im[..., None] * b_re
    return abar_re, abar_im, bb_re, bb_im


def complex_affine_combine(e1, e2):
    a1r, a1i, b1r, b1i = e1
    a2r, a2i, b2r, b2i = e2
    return (a2r * a1r - a2i * a1i, a2r * a1i + a2i * a1r,
            a2r * b1r - a2i * b1i + b2r, a2r * b1i + a2i * b1r + b2i)


def s5_scan(u, h0, abar_re, abar_im, bb_re, bb_im, reverse):
    br = jnp.einsum('gpm,bngm->bngp', bb_re, u)
    bi = jnp.einsum('gpm,bngm->bngp', bb_im, u)
    if h0 is not None:
        edge = -1 if reverse else 0
        h0r, h0i = h0
        br = br.at[:, edge].add(abar_re * h0r - abar_im * h0i)
        bi = bi.at[:, edge].add(abar_re * h0i + abar_im * h0r)
    n = u.shape[1]
    ar = jnp.broadcast_to(abar_re, (1, n) + abar_re.shape)
    ai = jnp.broadcast_to(abar_im, (1, n) + abar_im.shape)
    _, _, xr, xi = lax.associative_scan(complex_affine_combine, (ar, ai, br, bi), reverse=reverse, axis=1)
    return xr, xi


def s5_readout(xr, xi, c_re, c_im):
    return (jnp.einsum('gmp,bngp->bngm', c_re.astype(F32), xr)
            - jnp.einsum('gmp,bngp->bngm', c_im.astype(F32), xi))


def s5_mixer(u_ctx, u_lat, lam_re, lam_im, log_dt, b_re, b_im, c_re, c_im, d_skip, w_glu, need_ctx):
    groups = lambda u: u.astype(F32).reshape(u.shape[0], u.shape[1], S5_GROUPS, S5_GROUP)
    uc, ul = groups(u_ctx), groups(u_lat)
    y_ctx = jnp.zeros_like(uc)
    y_lat = jnp.zeros_like(ul)
    for dr, rev in ((0, False), (1, True)):
        ab_re, ab_im, bb_re, bb_im = s5_discretize(lam_re[dr], lam_im[dr], log_dt[dr], b_re[dr], b_im[dr])
        xr_c, xi_c = s5_scan(uc, None, ab_re, ab_im, bb_re, bb_im, rev)
        edge = 0 if rev else -1
        xr_l, xi_l = s5_scan(ul, (xr_c[:, edge], xi_c[:, edge]), ab_re, ab_im, bb_re, bb_im, rev)
        y_lat = y_lat + s5_readout(xr_l, xi_l, c_re[dr], c_im[dr])
        y_ctx = y_ctx + s5_readout(xr_c, xi_c, c_re[dr], c_im[dr])

    def finish(y, u):
        y = y.reshape(u.shape).astype(u.dtype) + d_skip * u
        z = jax.nn.gelu(y)
        val, gate = jnp.split(z @ w_glu, 2, axis=-1)
        return val * jax.nn.sigmoid(gate)

    o_ctx = finish(y_ctx, u_ctx) if need_ctx else None
    return finish(y_lat, u_lat), o_ctx


def hgrn2_chunk_scan(q, k, v, log_f, s0):
    out_dtype = v.dtype
    b_, n, H, K = q.shape
    C = HG_CHUNK
    nc = n // C
    to_chunks = lambda t: jnp.moveaxis(t.astype(F32).reshape(b_, nc, C, H, t.shape[-1]), 1, 0)
    causal = jnp.tril(jnp.ones((C, C), dtype=bool))[None, :, :, None, None]

    def step(S, inp):
        qc, kc, vc, gc = inp
        cum = jnp.cumsum(gc, axis=1)
        diff = cum[:, :, None] - cum[:, None]
        decay = jnp.exp(jnp.where(causal, diff, -jnp.inf))
        att = jnp.sum(qc[:, :, None] * kc[:, None] * decay, axis=-1)
        o = jnp.einsum('btsh,bshv->bthv', att, vc)
        o = o + jnp.einsum('bthk,bhkv->bthv', qc * jnp.exp(cum), S)
        last = cum[:, -1]
        S = jnp.exp(last)[..., None] * S + jnp.einsum('bshk,bshv->bhkv', kc * jnp.exp(last[:, None] - cum), vc)
        return S, o

    S, o = lax.scan(step, s0, (to_chunks(q), to_chunks(k), to_chunks(v), to_chunks(log_f)))
    o = jnp.moveaxis(o, 0, 1).reshape(b_, n, H, v.shape[-1])
    return o.astype(out_dtype), S


def hgrn2_mixer(u_ctx, u_lat, w_in, w_out, norm_g, lb, need_ctx):
    H, K, V = HG_HEADS, HG_KEY, HG_VAL
    log_lb, log_1m_lb = jnp.log(lb), jnp.log1p(-lb)

    def project(u):
        b_, n, _ = u.shape
        q, f_fw, f_bw, inp, gate = jnp.split(u @ w_in, 5, axis=-1)
        q = jax.nn.silu(q).reshape(b_, n, H, K)
        inp = inp.reshape(b_, n, H, V)
        dirs = []
        for f in (f_fw, f_bw):
            f = f.astype(F32)
            log_f = jnp.logaddexp(log_lb, log_1m_lb + jax.nn.log_sigmoid(f))
            one_m_f = (1.0 - lb) * jax.nn.sigmoid(-f)
            dirs.append((one_m_f.reshape(b_, n, H, K), log_f.reshape(b_, n, H, K)))
        return q, inp, dirs, gate

    qc, ic, dc, gc = project(u_ctx)
    ql, il, dl, gl = project(u_lat)
    s0 = jnp.zeros((u_lat.shape[0], H, K, V), F32)
    o_ctx = jnp.zeros(ic.shape, ic.dtype)
    o_lat = jnp.zeros(il.shape, il.dtype)
    for (kc_, lfc), (kl_, lfl), rev in zip(dc, dl, (False, True)):
        fl = (lambda t: jnp.flip(t, axis=1)) if rev else (lambda t: t)
        oc_d, s_ctx = hgrn2_chunk_scan(fl(qc), fl(kc_), fl(ic), fl(lfc), s0)
        ol_d, _ = hgrn2_chunk_scan(fl(ql), fl(kl_), fl(il), fl(lfl), s_ctx)
        o_ctx = o_ctx + fl(oc_d)
        o_lat = o_lat + fl(ol_d)

    def finish(o, gate):
        b2, n = o.shape[:2]
        o = rms_norm(o) * norm_g.reshape(H, V)
        return (o.reshape(b2, n, D_MODEL) * jax.nn.silu(gate)) @ w_out

    out_ctx = finish(o_ctx, gc) if need_ctx else None
    return finish(o_lat, gl), out_ctx


def peer_ffn(u, w_q, sub_keys, u_tab, v_tab):
    b_, n, D = u.shape
    H, KT, NK = PEER_HEADS, PEER_TOPK, PEER_NKEYS
    tok = u.reshape((b_ * n) // PEER_BLOCK, PEER_BLOCK, D)

    def block(xb):
        q = (xb @ w_q).reshape(PEER_BLOCK, H, 2, PEER_QDIM // 2)
        s = jnp.einsum('thcd,ckd->thck', q, sub_keys).astype(F32)
        s1, i1 = lax.top_k(s[:, :, 0], KT)
        s2, i2 = lax.top_k(s[:, :, 1], KT)
        cand_s = (s1[..., :, None] + s2[..., None, :]).reshape(PEER_BLOCK, H, KT * KT)
        cand_i = (i1[..., :, None] * NK + i2[..., None, :]).reshape(PEER_BLOCK, H, KT * KT)
        top_s, pos = lax.top_k(cand_s, KT)
        idx = jnp.take_along_axis(cand_i, pos, axis=-1)
        g = jax.nn.softmax(top_s, axis=-1)
        act = jax.nn.gelu(jnp.einsum('thkd,td->thk', u_tab[idx], xb))
        return jnp.einsum('thk,thkd->td', (g * act).astype(xb.dtype), v_tab[idx])

    return lax.map(block, tok).reshape(b_, n, D)


def kernel(x, c, ctx, c_ctx, ada_w, ada_b, ln_g, ln_b, da_w_in, da_w_out, da_lam_q, da_lam_k, da_subln,
           s5_lam_re, s5_lam_im, s5_log_dt, s5_b_re, s5_b_im, s5_c_re, s5_c_im, s5_d, s5_w_glu,
           hg_w_in, hg_w_out, hg_norm, hg_lb, peer_wq, peer_keys, peer_u, peer_v):
    L = x.shape[1]
    cos, sin = axial_rope(L, DA_HEAD_DIM)
    s_c = jax.nn.silu(c)
    s_ctx = jax.nn.silu(c_ctx)
    lb_soft = jax.nn.softmax(hg_lb.astype(F32), axis=0)
    lb_all = jnp.cumsum(lb_soft, axis=0) - lb_soft[0]
    h, hc = x, ctx
    for i in range(DEPTH):
        kind, slot = LAYER_TYPES[i], i // N_MIXERS
        need_ctx = i < DEPTH - 1
        mod = (s_c @ ada_w[i] + ada_b[i])[:, None, :]
        mod_c = s_ctx @ ada_w[i] + ada_b[i]
        sh1, sc1, g1, sh2, sc2, g2 = jnp.split(mod, 6, axis=-1)
        csh1, csc1, cg1, csh2, csc2, cg2 = jnp.split(mod_c, 6, axis=-1)
        u = h * (1.0 + sc1) + sh1
        uc = hc * (1.0 + csc1) + csh1
        if kind == 0:
            lam_init = 0.8 - 0.6 * math.exp(-0.3 * i)
            o, oc = diff_attention(uc, u, da_w_in[slot], da_w_out[slot], da_lam_q[slot], da_lam_k[slot],
                                   da_subln[slot], lam_init, cos, sin, need_ctx)
        elif kind == 1:
            o, oc = s5_mixer(uc, u, s5_lam_re[slot], s5_lam_im[slot], s5_log_dt[slot], s5_b_re[slot],
                             s5_b_im[slot], s5_c_re[slot], s5_c_im[slot], s5_d[slot], s5_w_glu[slot], need_ctx)
        else:
            o, oc = hgrn2_mixer(uc, u, hg_w_in[slot], hg_w_out[slot], hg_norm[slot], lb_all[i], need_ctx)
        h = residual_layer_norm(h, o, g1, ln_g[i, 0], ln_b[i, 0])
        f = peer_ffn(h * (1.0 + sc2) + sh2, peer_wq[i], peer_keys[i], peer_u[i], peer_v[i])
        h = residual_layer_norm(h, f, g2, ln_g[i, 1], ln_b[i, 1])
        if need_ctx:
            hc = residual_layer_norm(hc, oc, cg1.reshape(1, 1, -1), ln_g[i, 0], ln_b[i, 0])
            fc = peer_ffn(hc * (1.0 + csc2) + csh2, peer_wq[i], peer_keys[i], peer_u[i], peer_v[i])
            hc = residual_layer_norm(hc, fc, cg2.reshape(1, 1, -1), ln_g[i, 1], ln_b[i, 1])
    return h
```

```python
import functools
import math
import jax, jax.numpy as jnp
from jax import lax
import numpy as np
from jax.experimental import pallas as pl
from jax.experimental.pallas import tpu as pltpu

D_MODEL = 1024
BATCH = 32
SEQ = 2048
DEPTH = 4

CTX_LEN = 256
GRID_W = 64
N_MIXERS = 3
LAYER_TYPES = tuple(i % N_MIXERS for i in range(DEPTH))
N_ATTN = LAYER_TYPES.count(0)
N_S5 = LAYER_TYPES.count(1)
N_HG = LAYER_TYPES.count(2)

DA_HEADS = 8
DA_HEAD_DIM = 64
DA_V_DIM = 2 * DA_HEAD_DIM
Q_BLOCK = 128
ROPE_THETA = 10000.0
S5_GROUP = 16
S5_GROUPS = D_MODEL // S5_GROUP
S5_STATE = 64
HG_HEADS = 8
HG_KEY = D_MODEL // HG_HEADS
HG_VAL = D_MODEL // HG_HEADS
HG_CHUNK = 32
PEER_HEADS = 8
PEER_NKEYS = 128
PEER_EXPERTS = PEER_NKEYS * PEER_NKEYS
PEER_QDIM = 256
PEER_TOPK = 16
PEER_BLOCK = 128
LN_EPS = 1e-5
RMS_EPS = 1e-6
DN_ALPHA = (2 * DEPTH) ** 0.25
DN_BETA = (8 * DEPTH) ** -0.25

F32 = jnp.float32


def _res_ln_kernel(h_ref, o_ref, gate_ref, g_ref, b_ref, out_ref):
    y = DN_ALPHA * h_ref[0] + gate_ref[0] * o_ref[0]
    mu = jnp.mean(y, -1, keepdims=True)
    yc = y - mu
    var = jnp.mean(yc * yc, -1, keepdims=True)
    out_ref[0] = yc * lax.rsqrt(var + LN_EPS) * g_ref[...] + b_ref[...]


def residual_layer_norm(h, o, gate, g, b, block_n=512):
    b_, n, d = h.shape
    bn = min(block_n, n)
    per_batch_gate = gate.shape[0] == b_
    gate_map = (lambda i, j: (i, 0, 0)) if per_batch_gate else (lambda i, j: (0, 0, 0))
    return pl.pallas_call(
        _res_ln_kernel,
        grid=(b_, n // bn),
        in_specs=[pl.BlockSpec((1, bn, d), lambda i, j: (i, j, 0)),
                  pl.BlockSpec((1, bn, d), lambda i, j: (i, j, 0)),
                  pl.BlockSpec((1, 1, d), gate_map),
                  pl.BlockSpec((1, d), lambda i, j: (0, 0)),
                  pl.BlockSpec((1, d), lambda i, j: (0, 0))],
        out_specs=pl.BlockSpec((1, bn, d), lambda i, j: (i, j, 0)),
        out_shape=jax.ShapeDtypeStruct(h.shape, h.dtype),
        name="residual_layer_norm",
    )(h, o, gate, g.reshape(1, d), b.reshape(1, d))


def rms_norm(x):
    xf = x.astype(F32)
    return (xf * lax.rsqrt(jnp.mean(xf * xf, -1, keepdims=True) + RMS_EPS)).astype(x.dtype)


def axial_rope(length, dim):
    rows = length // GRID_W
    row = jnp.repeat(jnp.arange(rows, dtype=F32), GRID_W)
    col = jnp.tile(jnp.arange(GRID_W, dtype=F32), rows)
    n_freq = dim // 4
    inv = ROPE_THETA ** (-jnp.arange(n_freq, dtype=F32) / n_freq)
    ang = jnp.concatenate([row[:, None] * inv, col[:, None] * inv], axis=-1)
    return jnp.cos(ang), jnp.sin(ang)


def apply_rope(x, cos, sin):
    c = cos[None, :, None, None].astype(x.dtype)
    s = sin[None, :, None, None].astype(x.dtype)
    x1, x2 = x[..., 0::2], x[..., 1::2]
    return jnp.stack([x1 * c - x2 * s, x1 * s + x2 * c], axis=-1).reshape(x.shape)


def diff_softmax_mix(q, k, v, lam):
    s = jnp.einsum('bqhcd,bkhcd->bhcqk', q, k).astype(F32)
    p = jax.nn.softmax(s, axis=-1)
    a = p[:, :, 0] - lam * p[:, :, 1]
    return jnp.einsum('bhqk,bkhe->bqhe', a.astype(v.dtype), v)


def diff_attention(u_ctx, u_lat, w_in, w_out, lam_q, lam_k, subln_g, lam_init, cos, sin, need_ctx):
    H, d = DA_HEADS, DA_HEAD_DIM

    def project(u):
        b_, n, _ = u.shape
        q, k, v = jnp.split(u @ w_in, 3, axis=-1)
        return (q.reshape(b_, n, H, 2, d) * d ** -0.5,
                k.reshape(b_, n, H, 2, d),
                v.reshape(b_, n, H, 2 * d))

    qc, kc, vc = project(u_ctx)
    ql, kl, vl = project(u_lat)
    ql, kl = apply_rope(ql, cos, sin), apply_rope(kl, cos, sin)
    lq, lk = lam_q.astype(F32), lam_k.astype(F32)
    lam = jnp.exp(jnp.sum(lq[0] * lk[0])) - jnp.exp(jnp.sum(lq[1] * lk[1])) + lam_init
    k_all = jnp.concatenate([kl, kc], axis=1)
    v_all = jnp.concatenate([vl, vc], axis=1)
    b_, L = ql.shape[:2]
    nb = L // Q_BLOCK
    q_blocks = jnp.moveaxis(ql.reshape(b_, nb, Q_BLOCK, H, 2, d), 1, 0)
    o_lat = lax.map(lambda qb: diff_softmax_mix(qb, k_all, v_all, lam), q_blocks)
    o_lat = jnp.moveaxis(o_lat, 0, 1).reshape(b_, L, H, 2 * d)

    def finish(o):
        b2, n = o.shape[:2]
        o = rms_norm(o) * subln_g * (1.0 - lam_init)
        return o.reshape(b2, n, D_MODEL) @ w_out

    o_ctx = finish(diff_softmax_mix(qc, kc, vc, lam)) if need_ctx else None
    return finish(o_lat), o_ctx


def s5_discretize(lam_re, lam_im, log_dt, b_re, b_im):
    lam_re, lam_im = lam_re.astype(F32), lam_im.astype(F32)
    b_re, b_im = b_re.astype(F32), b_im.astype(F32)
    dt = jnp.exp(log_dt.astype(F32))[:, None]
    mag = jnp.exp(lam_re * dt)
    abar_re, abar_im = mag * jnp.cos(lam_im * dt), mag * jnp.sin(lam_im * dt)
    nr, ni = abar_re - 1.0, abar_im
    den = lam_re * lam_re + lam_im * lam_im
    k_re = (nr * lam_re + ni * lam_im) / den
    k_im = (ni * lam_re - nr * lam_im) / den
    bb_re = k_re[..., None] * b_re - k_im[..., None] * b_im
    bb_im = k_re[..., None] * b_im + k_im[..., None] * b_re
    return abar_re, abar_im, bb_re, bb_im


def complex_affine_combine(e1, e2):
    a1r, a1i, b1r, b1i = e1
    a2r, a2i, b2r, b2i = e2
    return (a2r * a1r - a2i * a1i, a2r * a1i + a2i * a1r,
            a2r * b1r - a2i * b1i + b2r, a2r * b1i + a2i * b1r + b2i)


def s5_scan(u, h0, abar_re, abar_im, bb_re, bb_im, reverse):
    br = jnp.einsum('gpm,bngm->bngp', bb_re, u)
    bi = jnp.einsum('gpm,bngm->bngp', bb_im, u)
    if h0 is not None:
        edge = -1 if reverse else 0
        h0r, h0i = h0
        br = br.at[:, edge].add(abar_re * h0r - abar_im * h0i)
        bi = bi.at[:, edge].add(abar_re * h0i + abar_im * h0r)
    n = u.shape[1]
    ar = jnp.broadcast_to(abar_re, (1, n) + abar_re.shape)
    ai = jnp.broadcast_to(abar_im, (1, n) + abar_im.shape)
    _, _, xr, xi = lax.associative_scan(complex_affine_combine, (ar, ai, br, bi), reverse=reverse, axis=1)
    return xr, xi


def s5_readout(xr, xi, c_re, c_im):
    return (jnp.einsum('gmp,bngp->bngm', c_re.astype(F32), xr)
            - jnp.einsum('gmp,bngp->bngm', c_im.astype(F32), xi))


def s5_mixer(u_ctx, u_lat, lam_re, lam_im, log_dt, b_re, b_im, c_re, c_im, d_skip, w_glu, need_ctx):
    groups = lambda u: u.astype(F32).reshape(u.shape[0], u.shape[1], S5_GROUPS, S5_GROUP)
    uc, ul = groups(u_ctx), groups(u_lat)
    y_ctx = jnp.zeros_like(uc)
    y_lat = jnp.zeros_like(ul)
    for dr, rev in ((0, False), (1, True)):
        ab_re, ab_im, bb_re, bb_im = s5_discretize(lam_re[dr], lam_im[dr], log_dt[dr], b_re[dr], b_im[dr])
        xr_c, xi_c = s5_scan(uc, None, ab_re, ab_im, bb_re, bb_im, rev)
        edge = 0 if rev else -1
        xr_l, xi_l = s5_scan(ul, (xr_c[:, edge], xi_c[:, edge]), ab_re, ab_im, bb_re, bb_im, rev)
        y_lat = y_lat + s5_readout(xr_l, xi_l, c_re[dr], c_im[dr])
        y_ctx = y_ctx + s5_readout(xr_c, xi_c, c_re[dr], c_im[dr])

    def finish(y, u):
        y = y.reshape(u.shape).astype(u.dtype) + d_skip * u
        z = jax.nn.gelu(y)
        val, gate = jnp.split(z @ w_glu, 2, axis=-1)
        return val * jax.nn.sigmoid(gate)

    o_ctx = finish(y_ctx, u_ctx) if need_ctx else None
    return finish(y_lat, u_lat), o_ctx


def hgrn2_chunk_scan(q, k, v, log_f, s0):
    out_dtype = v.dtype
    b_, n, H, K = q.shape
    C = HG_CHUNK
    nc = n // C
    to_chunks = lambda t: jnp.moveaxis(t.astype(F32).reshape(b_, nc, C, H, t.shape[-1]), 1, 0)
    causal = jnp.tril(jnp.ones((C, C), dtype=bool))[None, :, :, None, None]

    def step(S, inp):
        qc, kc, vc, gc = inp
        cum = jnp.cumsum(gc, axis=1)
        diff = cum[:, :, None] - cum[:, None]
        decay = jnp.exp(jnp.where(causal, diff, -jnp.inf))
        att = jnp.sum(qc[:, :, None] * kc[:, None] * decay, axis=-1)
        o = jnp.einsum('btsh,bshv->bthv', att, vc)
        o = o + jnp.einsum('bthk,bhkv->bthv', qc * jnp.exp(cum), S)
        last = cum[:, -1]
        S = jnp.exp(last)[..., None] * S + jnp.einsum('bshk,bshv->bhkv', kc * jnp.exp(last[:, None] - cum), vc)
        return S, o

    S, o = lax.scan(step, s0, (to_chunks(q), to_chunks(k), to_chunks(v), to_chunks(log_f)))
    o = jnp.moveaxis(o, 0, 1).reshape(b_, n, H, v.shape[-1])
    return o.astype(out_dtype), S


def hgrn2_mixer(u_ctx, u_lat, w_in, w_out, norm_g, lb, need_ctx):
    H, K, V = HG_HEADS, HG_KEY, HG_VAL
    log_lb, log_1m_lb = jnp.log(lb), jnp.log1p(-lb)

    def project(u):
        b_, n, _ = u.shape
        q, f_fw, f_bw, inp, gate = jnp.split(u @ w_in, 5, axis=-1)
        q = jax.nn.silu(q).reshape(b_, n, H, K)
        inp = inp.reshape(b_, n, H, V)
        dirs = []
        for f in (f_fw, f_bw):
            f = f.astype(F32)
            log_f = jnp.logaddexp(log_lb, log_1m_lb + jax.nn.log_sigmoid(f))
            one_m_f = (1.0 - lb) * jax.nn.sigmoid(-f)
            dirs.append((one_m_f.reshape(b_, n, H, K), log_f.reshape(b_, n, H, K)))
        return q, inp, dirs, gate

    qc, ic, dc, gc = project(u_ctx)
    ql, il, dl, gl = project(u_lat)
    s0 = jnp.zeros((u_lat.shape[0], H, K, V), F32)
    o_ctx = jnp.zeros(ic.shape, ic.dtype)
    o_lat = jnp.zeros(il.shape, il.dtype)
    for (kc_, lfc), (kl_, lfl), rev in zip(dc, dl, (False, True)):
        fl = (lambda t: jnp.flip(t, axis=1)) if rev else (lambda t: t)
        oc_d, s_ctx = hgrn2_chunk_scan(fl(qc), fl(kc_), fl(ic), fl(lfc), s0)
        ol_d, _ = hgrn2_chunk_scan(fl(ql), fl(kl_), fl(il), fl(lfl), s_ctx)
        o_ctx = o_ctx + fl(oc_d)
        o_lat = o_lat + fl(ol_d)

    def finish(o, gate):
        b2, n = o.shape[:2]
        o = rms_norm(o) * norm_g.reshape(H, V)
        return (o.reshape(b2, n, D_MODEL) * jax.nn.silu(gate)) @ w_out

    out_ctx = finish(o_ctx, gc) if need_ctx else None
    return finish(o_lat, gl), out_ctx


LANES = 128
PEER_SLOTS = PEER_HEADS * PEER_TOPK
PEER_SEL_TOKENS = 256
PEER_MIX_TOKENS = 128
PEER_ROW_BUFFERS = 4
BF16 = jnp.bfloat16


def _gelu_tanh(x):
    return 0.5 * x * (1.0 + jnp.tanh(math.sqrt(2.0 / math.pi) * (x + 0.044715 * (x * x * x))))


def _topk_axis0(cur, k, payload=None):
    rows = cur.shape[0]
    iota = lax.broadcasted_iota(jnp.int32, cur.shape, 0)
    vals, picks = [], []
    for _ in range(k):
        m = jnp.max(cur, axis=0, keepdims=True)
        pos = jnp.min(jnp.where(cur == m, iota, rows), axis=0, keepdims=True)
        hit = iota == pos
        vals.append(m)
        if payload is None:
            picks.append(pos)
        else:
            picks.append(jnp.sum(jnp.where(hit, payload, 0), axis=0, keepdims=True))
        cur = jnp.where(hit, -jnp.inf, cur)
    return jnp.concatenate(vals, axis=0), jnp.concatenate(picks, axis=0)


def _peer_select_kernel(h_ref, sc_ref, sh_ref, wq_ref, keys_ref, idx_ref, g_ref):
    half = PEER_QDIM // 2
    x = h_ref[0] * (1.0 + sc_ref[0]) + sh_ref[0]
    q = jnp.dot(x.astype(BF16), wq_ref[...], preferred_element_type=F32)
    tokens = x.shape[0]
    for c0 in range(0, tokens, LANES):
        idx_rows, g_rows = [], []
        for hd in range(PEER_HEADS):
            tops = []
            for c in range(2):
                lo = (hd * 2 + c) * half
                qhc = q[c0:c0 + LANES, lo:lo + half].astype(BF16)
                s_t = lax.dot_general(keys_ref[c], qhc, (((1,), (1,)), ((), ())),
                                      preferred_element_type=F32)
                tops.append(_topk_axis0(s_t, PEER_TOPK))
            (s1, i1), (s2, i2) = tops
            cand_s = jnp.concatenate([s1[a:a + 1] + s2 for a in range(PEER_TOPK)], axis=0)
            cand_i = jnp.concatenate([i1[a:a + 1] * PEER_NKEYS + i2 for a in range(PEER_TOPK)], axis=0)
            top_s, top_i = _topk_axis0(cand_s, PEER_TOPK, payload=cand_i)
            e = jnp.exp(top_s - top_s[0:1])
            g_rows.append(e / jnp.sum(e, axis=0, keepdims=True))
            idx_rows.append(top_i)
        idx_ref[0, c0:c0 + LANES, :] = jnp.concatenate(idx_rows, axis=0).T
        g_ref[0, c0:c0 + LANES, :] = jnp.concatenate(g_rows, axis=0).T


def peer_select(h, sc, sh, w_q, sub_keys):
    b_, n, d = h.shape
    tb = min(PEER_SEL_TOKENS, n)
    mod_map = (lambda i, j: (i, 0, 0)) if sc.shape[0] == b_ else (lambda i, j: (0, 0, 0))
    return pl.pallas_call(
        _peer_select_kernel,
        grid=(b_, n // tb),
        in_specs=[pl.BlockSpec((1, tb, d), lambda i, j: (i, j, 0)),
                  pl.BlockSpec((1, 1, d), mod_map),
                  pl.BlockSpec((1, 1, d), mod_map),
                  pl.BlockSpec(w_q.shape, lambda i, j: (0, 0)),
                  pl.BlockSpec(sub_keys.shape, lambda i, j: (0, 0, 0))],
        out_specs=[pl.BlockSpec((1, tb, PEER_SLOTS), lambda i, j: (i, j, 0)),
                   pl.BlockSpec((1, tb, PEER_SLOTS), lambda i, j: (i, j, 0))],
        out_shape=[jax.ShapeDtypeStruct((b_, n, PEER_SLOTS), jnp.int32),
                   jax.ShapeDtypeStruct((b_, n, PEER_SLOTS), F32)],
        compiler_params=pltpu.CompilerParams(vmem_limit_bytes=48 * 1024 * 1024),
        name="peer_select",
    )(h, sc, sh, w_q.astype(BF16), sub_keys.astype(BF16))


def _peer_mix_kernel(idx_hbm, h_ref, sc_ref, sh_ref, g_ref, tab_hbm, out_ref,
                     idx_smem, x_scr, rows, row_sem, idx_sem):
    tokens, d = x_scr.shape
    blk = pl.program_id(0) * pl.num_programs(1) + pl.program_id(1)
    per_blk = tokens * PEER_SLOTS
    idx_copy = pltpu.make_async_copy(idx_hbm.at[pl.ds(pl.multiple_of(blk * per_blk, per_blk), per_blk)],
                                     idx_smem, idx_sem)
    idx_copy.start()
    x_scr[...] = h_ref[0] * (1.0 + sc_ref[0]) + sh_ref[0]
    idx_copy.wait()

    def row_copy(e, slot, j):
        return pltpu.make_async_copy(tab_hbm.at[pl.ds(e, 1), :], rows.at[slot, pl.ds(j, 1), :], row_sem.at[slot])

    def fetch(t, slot):
        for j in range(PEER_SLOTS):
            row_copy(idx_smem[t * PEER_SLOTS + j], slot, j).start()

    def wait_rows(slot):
        pltpu.make_async_copy(tab_hbm.at[pl.ds(0, PEER_SLOTS), :], rows.at[slot], row_sem.at[slot]).wait()

    ahead = PEER_ROW_BUFFERS - 1
    for t0 in range(ahead):
        fetch(t0, t0)

    eye = (lax.broadcasted_iota(jnp.int32, (PEER_SLOTS, PEER_SLOTS), 0)
           == lax.broadcasted_iota(jnp.int32, (PEER_SLOTS, PEER_SLOTS), 1))

    def body(t, carry):
        slot = lax.rem(t, PEER_ROW_BUFFERS)

        @pl.when(t + ahead < tokens)
        def _():
            fetch(t + ahead, lax.rem(t + ahead, PEER_ROW_BUFFERS))

        wait_rows(slot)
        x_row = x_scr[pl.ds(t, 1), :]
        act = jnp.sum(rows[slot, :, :d] * x_row, axis=1, keepdims=True)
        g_col = jnp.sum(jnp.where(eye, g_ref[0, pl.ds(t, 1), :], 0.0), axis=1, keepdims=True)
        w = g_col * _gelu_tanh(act)
        out_ref[0, pl.ds(t, 1), :] = jnp.sum(w * rows[slot, :, d:], axis=0, keepdims=True)
        return carry

    lax.fori_loop(0, tokens, body, 0)


def peer_mix(h, sc, sh, idx, g, uv_tab):
    b_, n, d = h.shape
    tb = min(PEER_MIX_TOKENS, n)
    mod_map = (lambda i, j: (i, 0, 0)) if sc.shape[0] == b_ else (lambda i, j: (0, 0, 0))
    return pl.pallas_call(
        _peer_mix_kernel,
        grid=(b_, n // tb),
        in_specs=[pl.BlockSpec(memory_space=pl.ANY),
                  pl.BlockSpec((1, tb, d), lambda i, j: (i, j, 0)),
                  pl.BlockSpec((1, 1, d), mod_map),
                  pl.BlockSpec((1, 1, d), mod_map),
                  pl.BlockSpec((1, tb, PEER_SLOTS), lambda i, j: (i, j, 0)),
                  pl.BlockSpec(memory_space=pl.ANY)],
        out_specs=pl.BlockSpec((1, tb, d), lambda i, j: (i, j, 0)),
        out_shape=jax.ShapeDtypeStruct(h.shape, F32),
        scratch_shapes=[pltpu.SMEM((tb * PEER_SLOTS,), jnp.int32),
                        pltpu.VMEM((tb, d), F32),
                        pltpu.VMEM((PEER_ROW_BUFFERS, PEER_SLOTS, 2 * d), F32),
                        pltpu.SemaphoreType.DMA((PEER_ROW_BUFFERS,)),
                        pltpu.SemaphoreType.DMA(())],
        compiler_params=pltpu.CompilerParams(vmem_limit_bytes=32 * 1024 * 1024),
        name="peer_mix",
    )(idx.reshape(-1), h, sc, sh, g, uv_tab)


def peer_ffn(h, sc, sh, w_q, sub_keys, uv_tab):
    idx, g = peer_select(h, sc, sh, w_q, sub_keys)
    return peer_mix(h, sc, sh, idx, g, uv_tab)


def kernel(x, c, ctx, c_ctx, ada_w, ada_b, ln_g, ln_b, da_w_in, da_w_out, da_lam_q, da_lam_k, da_subln,
           s5_lam_re, s5_lam_im, s5_log_dt, s5_b_re, s5_b_im, s5_c_re, s5_c_im, s5_d, s5_w_glu,
           hg_w_in, hg_w_out, hg_norm, hg_lb, peer_wq, peer_keys, peer_u, peer_v):
    L = x.shape[1]
    cos, sin = axial_rope(L, DA_HEAD_DIM)
    s_c = jax.nn.silu(c)
    s_ctx = jax.nn.silu(c_ctx)
    lb_soft = jax.nn.softmax(hg_lb.astype(F32), axis=0)
    lb_all = jnp.cumsum(lb_soft, axis=0) - lb_soft[0]
    h, hc = x, ctx
    for i in range(DEPTH):
        kind, slot = LAYER_TYPES[i], i // N_MIXERS
        need_ctx = i < DEPTH - 1
        mod = (s_c @ ada_w[i] + ada_b[i])[:, None, :]
        mod_c = s_ctx @ ada_w[i] + ada_b[i]
        sh1, sc1, g1, sh2, sc2, g2 = jnp.split(mod, 6, axis=-1)
        csh1, csc1, cg1, csh2, csc2, cg2 = jnp.split(mod_c, 6, axis=-1)
        u = h * (1.0 + sc1) + sh1
        uc = hc * (1.0 + csc1) + csh1
        if kind == 0:
            lam_init = 0.8 - 0.6 * math.exp(-0.3 * i)
            o, oc = diff_attention(uc, u, da_w_in[slot], da_w_out[slot], da_lam_q[slot], da_lam_k[slot],
                                   da_subln[slot], lam_init, cos, sin, need_ctx)
        elif kind == 1:
            o, oc = s5_mixer(uc, u, s5_lam_re[slot], s5_lam_im[slot], s5_log_dt[slot], s5_b_re[slot],
                             s5_b_im[slot], s5_c_re[slot], s5_c_im[slot], s5_d[slot], s5_w_glu[slot], need_ctx)
        else:
            o, oc = hgrn2_mixer(uc, u, hg_w_in[slot], hg_w_out[slot], hg_norm[slot], lb_all[i], need_ctx)
        h = residual_layer_norm(h, o, g1, ln_g[i, 0], ln_b[i, 0])
        uv_tab = jnp.concatenate([peer_u[i], peer_v[i]], axis=1)
        f = peer_ffn(h, sc2, sh2, peer_wq[i], peer_keys[i], uv_tab)
        h = residual_layer_norm(h, f, g2, ln_g[i, 1], ln_b[i, 1])
        if need_ctx:
            hc = residual_layer_norm(hc, oc, cg1.reshape(1, 1, -1), ln_g[i, 0], ln_b[i, 0])
            fc = peer_ffn(hc, csc2.reshape(1, 1, -1), csh2.reshape(1, 1, -1), peer_wq[i], peer_keys[i], uv_tab)
            hc = residual_layer_norm(hc, fc, cg2.reshape(1, 1, -1), ln_g[i, 1], ln_b[i, 1])
    return h
```

```python
import functools
import math
import jax, jax.numpy as jnp
from jax import lax
import numpy as np
from jax.experimental import pallas as pl
from jax.experimental.pallas import tpu as pltpu

D_MODEL = 1024
BATCH = 32
SEQ = 2048
DEPTH = 4

CTX_LEN = 256
GRID_W = 64
N_MIXERS = 3
LAYER_TYPES = tuple(i % N_MIXERS for i in range(DEPTH))
N_ATTN = LAYER_TYPES.count(0)
N_S5 = LAYER_TYPES.count(1)
N_HG = LAYER_TYPES.count(2)

DA_HEADS = 8
DA_HEAD_DIM = 64
DA_V_DIM = 2 * DA_HEAD_DIM
Q_BLOCK = 128
ROPE_THETA = 10000.0
S5_GROUP = 16
S5_GROUPS = D_MODEL // S5_GROUP
S5_STATE = 64
HG_HEADS = 8
HG_KEY = D_MODEL // HG_HEADS
HG_VAL = D_MODEL // HG_HEADS
HG_CHUNK = 32
PEER_HEADS = 8
PEER_NKEYS = 128
PEER_EXPERTS = PEER_NKEYS * PEER_NKEYS
PEER_QDIM = 256
PEER_TOPK = 16
PEER_BLOCK = 128
LN_EPS = 1e-5
RMS_EPS = 1e-6
DN_ALPHA = (2 * DEPTH) ** 0.25
DN_BETA = (8 * DEPTH) ** -0.25

F32 = jnp.float32


def _res_ln_kernel(h_ref, o_ref, gate_ref, g_ref, b_ref, out_ref):
    y = DN_ALPHA * h_ref[0] + gate_ref[0] * o_ref[0]
    mu = jnp.mean(y, -1, keepdims=True)
    yc = y - mu
    var = jnp.mean(yc * yc, -1, keepdims=True)
    out_ref[0] = yc * lax.rsqrt(var + LN_EPS) * g_ref[...] + b_ref[...]


def residual_layer_norm(h, o, gate, g, b, block_n=512):
    b_, n, d = h.shape
    bn = min(block_n, n)
    per_batch_gate = gate.shape[0] == b_
    gate_map = (lambda i, j: (i, 0, 0)) if per_batch_gate else (lambda i, j: (0, 0, 0))
    return pl.pallas_call(
        _res_ln_kernel,
        grid=(b_, n // bn),
        in_specs=[pl.BlockSpec((1, bn, d), lambda i, j: (i, j, 0)),
                  pl.BlockSpec((1, bn, d), lambda i, j: (i, j, 0)),
                  pl.BlockSpec((1, 1, d), gate_map),
                  pl.BlockSpec((1, d), lambda i, j: (0, 0)),
                  pl.BlockSpec((1, d), lambda i, j: (0, 0))],
        out_specs=pl.BlockSpec((1, bn, d), lambda i, j: (i, j, 0)),
        out_shape=jax.ShapeDtypeStruct(h.shape, h.dtype),
        name="residual_layer_norm",
    )(h, o, gate, g.reshape(1, d), b.reshape(1, d))


def rms_norm(x):
    xf = x.astype(F32)
    return (xf * lax.rsqrt(jnp.mean(xf * xf, -1, keepdims=True) + RMS_EPS)).astype(x.dtype)


def axial_rope(length, dim):
    rows = length // GRID_W
    row = jnp.repeat(jnp.arange(rows, dtype=F32), GRID_W)
    col = jnp.tile(jnp.arange(GRID_W, dtype=F32), rows)
    n_freq = dim // 4
    inv = ROPE_THETA ** (-jnp.arange(n_freq, dtype=F32) / n_freq)
    ang = jnp.concatenate([row[:, None] * inv, col[:, None] * inv], axis=-1)
    return jnp.cos(ang), jnp.sin(ang)


def apply_rope(x, cos, sin):
    c = cos[None, :, None, None].astype(x.dtype)
    s = sin[None, :, None, None].astype(x.dtype)
    x1, x2 = x[..., 0::2], x[..., 1::2]
    return jnp.stack([x1 * c - x2 * s, x1 * s + x2 * c], axis=-1).reshape(x.shape)


def diff_softmax_mix(q, k, v, lam):
    s = jnp.einsum('bqhcd,bkhcd->bhcqk', q, k).astype(F32)
    p = jax.nn.softmax(s, axis=-1)
    a = p[:, :, 0] - lam * p[:, :, 1]
    return jnp.einsum('bhqk,bkhe->bqhe', a.astype(v.dtype), v)


def diff_attention(u_ctx, u_lat, w_in, w_out, lam_q, lam_k, subln_g, lam_init, cos, sin, need_ctx):
    H, d = DA_HEADS, DA_HEAD_DIM

    def project(u):
        b_, n, _ = u.shape
        q, k, v = jnp.split(u @ w_in, 3, axis=-1)
        return (q.reshape(b_, n, H, 2, d) * d ** -0.5,
                k.reshape(b_, n, H, 2, d),
                v.reshape(b_, n, H, 2 * d))

    qc, kc, vc = project(u_ctx)
    ql, kl, vl = project(u_lat)
    ql, kl = apply_rope(ql, cos, sin), apply_rope(kl, cos, sin)
    lq, lk = lam_q.astype(F32), lam_k.astype(F32)
    lam = jnp.exp(jnp.sum(lq[0] * lk[0])) - jnp.exp(jnp.sum(lq[1] * lk[1])) + lam_init
    k_all = jnp.concatenate([kl, kc], axis=1)
    v_all = jnp.concatenate([vl, vc], axis=1)
    b_, L = ql.shape[:2]
    nb = L // Q_BLOCK
    q_blocks = jnp.moveaxis(ql.reshape(b_, nb, Q_BLOCK, H, 2, d), 1, 0)
    o_lat = lax.map(lambda qb: diff_softmax_mix(qb, k_all, v_all, lam), q_blocks)
    o_lat = jnp.moveaxis(o_lat, 0, 1).reshape(b_, L, H, 2 * d)

    def finish(o):
        b2, n = o.shape[:2]
        o = rms_norm(o) * subln_g * (1.0 - lam_init)
        return o.reshape(b2, n, D_MODEL) @ w_out

    o_ctx = finish(diff_softmax_mix(qc, kc, vc, lam)) if need_ctx else None
    return finish(o_lat), o_ctx


def s5_discretize(lam_re, lam_im, log_dt, b_re, b_im):
    lam_re, lam_im = lam_re.astype(F32), lam_im.astype(F32)
    b_re, b_im = b_re.astype(F32), b_im.astype(F32)
    dt = jnp.exp(log_dt.astype(F32))[:, None]
    mag = jnp.exp(lam_re * dt)
    abar_re, abar_im = mag * jnp.cos(lam_im * dt), mag * jnp.sin(lam_im * dt)
    nr, ni = abar_re - 1.0, abar_im
    den = lam_re * lam_re + lam_im * lam_im
    k_re = (nr * lam_re + ni * lam_im) / den
    k_im = (ni * lam_re - nr * lam_im) / den
    bb_re = k_re[..., None] * b_re - k_im[..., None] * b_im
    bb_im = k_re[..., None] * b_im + k_im[..., None] * b_re
    return abar_re, abar_im, bb_re, bb_im


S5_BATCH_TILE = 8
S5_SCAN_TOKENS = 128
S5_PANEL_GROUPS = 16
S5_PANELS = S5_GROUPS // S5_PANEL_GROUPS
MOSAIC_VMEM_LIMIT = 48 * 1024 * 1024


def _s5_panels(bb, c):
    g, p, m = bb.shape
    pg = S5_PANEL_GROUPS
    eye = jnp.eye(pg, dtype=bb.dtype)
    w_in = jnp.einsum('qgpm,gh->qgmhp', bb.reshape(g // pg, pg, p, m), eye).reshape(g // pg, pg * m, pg * p)
    w_out = jnp.einsum('qgmp,gh->qgphm', c.reshape(g // pg, pg, m, p), eye).reshape(g // pg, pg * p, pg * m)
    return w_in.astype(BF16), w_out.astype(BF16)


def _s5_scan_kernel(u_ref, wb_re_ref, wb_im_ref, a_re_ref, a_im_ref, c_re_ref, c_im_ref, y_ref,
                    x_re, x_im, state, *, reverse):
    rows = u_ref.shape[1]
    tokens = rows // S5_BATCH_TILE
    cin = S5_PANEL_GROUPS * S5_GROUP

    @pl.when(pl.program_id(1) == 0)
    def _():
        state[...] = jnp.zeros_like(state)

    for p in range(S5_PANELS):
        ub = u_ref[0, :, p * cin:(p + 1) * cin].astype(BF16)
        x_re[...] = jnp.dot(ub, wb_re_ref[p], preferred_element_type=F32)
        x_im[...] = jnp.dot(ub, wb_im_ref[p], preferred_element_type=F32)
        a_re, a_im = a_re_ref[p], a_im_ref[p]

        def step(i, carry):
            xr, xi = carry
            t = (tokens - 1 - i) if reverse else i
            r0 = pl.multiple_of(t * S5_BATCH_TILE, S5_BATCH_TILE)
            nr = a_re * xr - a_im * xi + x_re[pl.ds(r0, S5_BATCH_TILE), :]
            ni = a_re * xi + a_im * xr + x_im[pl.ds(r0, S5_BATCH_TILE), :]
            x_re[pl.ds(r0, S5_BATCH_TILE), :] = nr
            x_im[pl.ds(r0, S5_BATCH_TILE), :] = ni
            return nr, ni

        xr, xi = lax.fori_loop(0, tokens, step, (state[p, 0], state[p, 1]), unroll=2)
        state[p, 0] = xr
        state[p, 1] = xi
        y_ref[0, :, p * cin:(p + 1) * cin] = (
            jnp.dot(x_re[...].astype(BF16), c_re_ref[p], preferred_element_type=F32)
            - jnp.dot(x_im[...].astype(BF16), c_im_ref[p], preferred_element_type=F32))


def s5_scan(u_g, lam_re, lam_im, log_dt, b_re, b_im, c_re, c_im, n_ctx, reverse):
    nbg, rows_total, d = u_g.shape
    abar_re, abar_im, bb_re, bb_im = s5_discretize(lam_re, lam_im, log_dt, b_re, b_im)
    wb_re, cp_re = _s5_panels(bb_re, c_re.astype(F32))
    wb_im, cp_im = _s5_panels(bb_im, c_im.astype(F32))
    states = S5_PANEL_GROUPS * S5_STATE
    tile = lambda a: jnp.broadcast_to(a.reshape(S5_PANELS, 1, states), (S5_PANELS, S5_BATCH_TILE, states))
    rows = S5_SCAN_TOKENS * S5_BATCH_TILE
    n_chunks = rows_total // rows
    ctx_chunks = n_ctx // S5_SCAN_TOKENS
    if reverse:
        chunk = lambda s: jnp.where(s < ctx_chunks, ctx_chunks - 1 - s, n_chunks - 1 - (s - ctx_chunks))
    else:
        chunk = lambda s: s
    full = lambda a: pl.BlockSpec(a.shape, lambda i, s: (0,) * a.ndim)
    a_re_t, a_im_t = tile(abar_re), tile(abar_im)
    return pl.pallas_call(
        functools.partial(_s5_scan_kernel, reverse=reverse),
        grid=(nbg, n_chunks),
        in_specs=[pl.BlockSpec((1, rows, d), lambda i, s: (i, chunk(s), 0)),
                  full(wb_re), full(wb_im), full(a_re_t), full(a_im_t), full(cp_re), full(cp_im)],
        out_specs=pl.BlockSpec((1, rows, d), lambda i, s: (i, chunk(s), 0)),
        out_shape=jax.ShapeDtypeStruct(u_g.shape, F32),
        scratch_shapes=[pltpu.VMEM((rows, states), F32), pltpu.VMEM((rows, states), F32),
                        pltpu.VMEM((S5_PANELS, 2, S5_BATCH_TILE, states), F32)],
        compiler_params=pltpu.CompilerParams(dimension_semantics=("arbitrary", "arbitrary"),
                                             vmem_limit_bytes=MOSAIC_VMEM_LIMIT),
        name="s5_scan_bwd" if reverse else "s5_scan_fwd",
    )(u_g, wb_re, wb_im, a_re_t, a_im_t, cp_re, cp_im)


def _s5_glu_kernel(yf_ref, yb_ref, u_ref, d_ref, w_ref, o_ref):
    d = o_ref.shape[-1]
    y = yf_ref[0] + yb_ref[0] + d_ref[...] * u_ref[0]
    r = jnp.dot(_gelu_tanh(y).astype(BF16), w_ref[...], preferred_element_type=F32)
    o_ref[0] = r[:, :d] * jax.nn.sigmoid(r[:, d:])


def s5_glu(y_fw, y_bw, u_g, d_skip, w_glu, block_rows=512):
    nbg, rows_total, d = u_g.shape
    blk = pl.BlockSpec((1, block_rows, d), lambda i, j: (i, j, 0))
    return pl.pallas_call(
        _s5_glu_kernel,
        grid=(nbg, rows_total // block_rows),
        in_specs=[blk, blk, blk,
                  pl.BlockSpec((1, d), lambda i, j: (0, 0)),
                  pl.BlockSpec(w_glu.shape, lambda i, j: (0, 0))],
        out_specs=blk,
        out_shape=jax.ShapeDtypeStruct(u_g.shape, F32),
        compiler_params=pltpu.CompilerParams(vmem_limit_bytes=MOSAIC_VMEM_LIMIT),
        name="s5_glu",
    )(y_fw, y_bw, u_g, d_skip.reshape(1, d), w_glu.astype(BF16))


def s5_mixer(u_ctx, u_lat, lam_re, lam_im, log_dt, b_re, b_im, c_re, c_im, d_skip, w_glu, need_ctx):
    b_, n_ctx, d = u_ctx.shape
    n_all = n_ctx + u_lat.shape[1]
    bt = S5_BATCH_TILE
    u_all = jnp.concatenate([u_ctx, u_lat], axis=1)
    u_g = u_all.reshape(b_ // bt, bt, n_all, d).transpose(0, 2, 1, 3).reshape(b_ // bt, n_all * bt, d)
    ys = [s5_scan(u_g, lam_re[dr], lam_im[dr], log_dt[dr], b_re[dr], b_im[dr], c_re[dr], c_im[dr],
                  n_ctx, reverse=bool(dr)) for dr in range(2)]
    o_g = s5_glu(ys[0], ys[1], u_g, d_skip, w_glu)
    o = o_g.reshape(b_ // bt, n_all, bt, d).transpose(0, 2, 1, 3).reshape(b_, n_all, d)
    return o[:, n_ctx:], (o[:, :n_ctx] if need_ctx else None)


def hgrn2_chunk_scan(q, k, v, log_f, s0):
    out_dtype = v.dtype
    b_, n, H, K = q.shape
    C = HG_CHUNK
    nc = n // C
    to_chunks = lambda t: jnp.moveaxis(t.astype(F32).reshape(b_, nc, C, H, t.shape[-1]), 1, 0)
    causal = jnp.tril(jnp.ones((C, C), dtype=bool))[None, :, :, None, None]

    def step(S, inp):
        qc, kc, vc, gc = inp
        cum = jnp.cumsum(gc, axis=1)
        diff = cum[:, :, None] - cum[:, None]
        decay = jnp.exp(jnp.where(causal, diff, -jnp.inf))
        att = jnp.sum(qc[:, :, None] * kc[:, None] * decay, axis=-1)
        o = jnp.einsum('btsh,bshv->bthv', att, vc)
        o = o + jnp.einsum('bthk,bhkv->bthv', qc * jnp.exp(cum), S)
        last = cum[:, -1]
        S = jnp.exp(last)[..., None] * S + jnp.einsum('bshk,bshv->bhkv', kc * jnp.exp(last[:, None] - cum), vc)
        return S, o

    S, o = lax.scan(step, s0, (to_chunks(q), to_chunks(k), to_chunks(v), to_chunks(log_f)))
    o = jnp.moveaxis(o, 0, 1).reshape(b_, n, H, v.shape[-1])
    return o.astype(out_dtype), S


def hgrn2_mixer(u_ctx, u_lat, w_in, w_out, norm_g, lb, need_ctx):
    H, K, V = HG_HEADS, HG_KEY, HG_VAL
    log_lb, log_1m_lb = jnp.log(lb), jnp.log1p(-lb)

    def project(u):
        b_, n, _ = u.shape
        q, f_fw, f_bw, inp, gate = jnp.split(u @ w_in, 5, axis=-1)
        q = jax.nn.silu(q).reshape(b_, n, H, K)
        inp = inp.reshape(b_, n, H, V)
        dirs = []
        for f in (f_fw, f_bw):
            f = f.astype(F32)
            log_f = jnp.logaddexp(log_lb, log_1m_lb + jax.nn.log_sigmoid(f))
            one_m_f = (1.0 - lb) * jax.nn.sigmoid(-f)
            dirs.append((one_m_f.reshape(b_, n, H, K), log_f.reshape(b_, n, H, K)))
        return q, inp, dirs, gate

    qc, ic, dc, gc = project(u_ctx)
    ql, il, dl, gl = project(u_lat)
    s0 = jnp.zeros((u_lat.shape[0], H, K, V), F32)
    o_ctx = jnp.zeros(ic.shape, ic.dtype)
    o_lat = jnp.zeros(il.shape, il.dtype)
    for (kc_, lfc), (kl_, lfl), rev in zip(dc, dl, (False, True)):
        fl = (lambda t: jnp.flip(t, axis=1)) if rev else (lambda t: t)
        oc_d, s_ctx = hgrn2_chunk_scan(fl(qc), fl(kc_), fl(ic), fl(lfc), s0)
        ol_d, _ = hgrn2_chunk_scan(fl(ql), fl(kl_), fl(il), fl(lfl), s_ctx)
        o_ctx = o_ctx + fl(oc_d)
        o_lat = o_lat + fl(ol_d)

    def finish(o, gate):
        b2, n = o.shape[:2]
        o = rms_norm(o) * norm_g.reshape(H, V)
        return (o.reshape(b2, n, D_MODEL) * jax.nn.silu(gate)) @ w_out

    out_ctx = finish(o_ctx, gc) if need_ctx else None
    return finish(o_lat, gl), out_ctx


LANES = 128
PEER_SLOTS = PEER_HEADS * PEER_TOPK
PEER_SEL_TOKENS = 256
PEER_MIX_TOKENS = 128
PEER_ROW_BUFFERS = 4
BF16 = jnp.bfloat16


def _gelu_tanh(x):
    return 0.5 * x * (1.0 + jnp.tanh(math.sqrt(2.0 / math.pi) * (x + 0.044715 * (x * x * x))))


def _topk_axis0(cur, k, payload=None):
    rows = cur.shape[0]
    iota = lax.broadcasted_iota(jnp.int32, cur.shape, 0)
    vals, picks = [], []
    for _ in range(k):
        m = jnp.max(cur, axis=0, keepdims=True)
        pos = jnp.min(jnp.where(cur == m, iota, rows), axis=0, keepdims=True)
        hit = iota == pos
        vals.append(m)
        if payload is None:
            picks.append(pos)
        else:
            picks.append(jnp.sum(jnp.where(hit, payload, 0), axis=0, keepdims=True))
        cur = jnp.where(hit, -jnp.inf, cur)
    return jnp.concatenate(vals, axis=0), jnp.concatenate(picks, axis=0)


def _peer_select_kernel(h_ref, sc_ref, sh_ref, wq_ref, keys_ref, idx_ref, g_ref):
    half = PEER_QDIM // 2
    x = h_ref[0] * (1.0 + sc_ref[0]) + sh_ref[0]
    q = jnp.dot(x.astype(BF16), wq_ref[...], preferred_element_type=F32)
    tokens = x.shape[0]
    for c0 in range(0, tokens, LANES):
        idx_rows, g_rows = [], []
        for hd in range(PEER_HEADS):
            tops = []
            for c in range(2):
                lo = (hd * 2 + c) * half
                qhc = q[c0:c0 + LANES, lo:lo + half].astype(BF16)
                s_t = lax.dot_general(keys_ref[c], qhc, (((1,), (1,)), ((), ())),
                                      preferred_element_type=F32)
                tops.append(_topk_axis0(s_t, PEER_TOPK))
            (s1, i1), (s2, i2) = tops
            cand_s = jnp.concatenate([s1[a:a + 1] + s2 for a in range(PEER_TOPK)], axis=0)
            cand_i = jnp.concatenate([i1[a:a + 1] * PEER_NKEYS + i2 for a in range(PEER_TOPK)], axis=0)
            top_s, top_i = _topk_axis0(cand_s, PEER_TOPK, payload=cand_i)
            e = jnp.exp(top_s - top_s[0:1])
            g_rows.append(e / jnp.sum(e, axis=0, keepdims=True))
            idx_rows.append(top_i)
        idx_ref[0, c0:c0 + LANES, :] = jnp.concatenate(idx_rows, axis=0).T
        g_ref[0, c0:c0 + LANES, :] = jnp.concatenate(g_rows, axis=0).T


def peer_select(h, sc, sh, w_q, sub_keys):
    b_, n, d = h.shape
    tb = min(PEER_SEL_TOKENS, n)
    mod_map = (lambda i, j: (i, 0, 0)) if sc.shape[0] == b_ else (lambda i, j: (0, 0, 0))
    return pl.pallas_call(
        _peer_select_kernel,
        grid=(b_, n // tb),
        in_specs=[pl.BlockSpec((1, tb, d), lambda i, j: (i, j, 0)),
                  pl.BlockSpec((1, 1, d), mod_map),
                  pl.BlockSpec((1, 1, d), mod_map),
                  pl.BlockSpec(w_q.shape, lambda i, j: (0, 0)),
                  pl.BlockSpec(sub_keys.shape, lambda i, j: (0, 0, 0))],
        out_specs=[pl.BlockSpec((1, tb, PEER_SLOTS), lambda i, j: (i, j, 0)),
                   pl.BlockSpec((1, tb, PEER_SLOTS), lambda i, j: (i, j, 0))],
        out_shape=[jax.ShapeDtypeStruct((b_, n, PEER_SLOTS), jnp.int32),
                   jax.ShapeDtypeStruct((b_, n, PEER_SLOTS), F32)],
        compiler_params=pltpu.CompilerParams(vmem_limit_bytes=48 * 1024 * 1024),
        name="peer_select",
    )(h, sc, sh, w_q.astype(BF16), sub_keys.astype(BF16))


def _peer_mix_kernel(idx_hbm, h_ref, sc_ref, sh_ref, g_ref, tab_hbm, out_ref,
                     idx_smem, x_scr, rows, row_sem, idx_sem):
    tokens, d = x_scr.shape
    blk = pl.program_id(0) * pl.num_programs(1) + pl.program_id(1)
    per_blk = tokens * PEER_SLOTS
    idx_copy = pltpu.make_async_copy(idx_hbm.at[pl.ds(pl.multiple_of(blk * per_blk, per_blk), per_blk)],
                                     idx_smem, idx_sem)
    idx_copy.start()
    x_scr[...] = h_ref[0] * (1.0 + sc_ref[0]) + sh_ref[0]
    idx_copy.wait()

    def row_copy(e, slot, j):
        return pltpu.make_async_copy(tab_hbm.at[pl.ds(e, 1), :], rows.at[slot, pl.ds(j, 1), :], row_sem.at[slot])

    def fetch(t, slot):
        for j in range(PEER_SLOTS):
            row_copy(idx_smem[t * PEER_SLOTS + j], slot, j).start()

    def wait_rows(slot):
        pltpu.make_async_copy(tab_hbm.at[pl.ds(0, PEER_SLOTS), :], rows.at[slot], row_sem.at[slot]).wait()

    ahead = PEER_ROW_BUFFERS - 1
    for t0 in range(ahead):
        fetch(t0, t0)

    eye = (lax.broadcasted_iota(jnp.int32, (PEER_SLOTS, PEER_SLOTS), 0)
           == lax.broadcasted_iota(jnp.int32, (PEER_SLOTS, PEER_SLOTS), 1))

    def body(t, carry):
        slot = lax.rem(t, PEER_ROW_BUFFERS)

        @pl.when(t + ahead < tokens)
        def _():
            fetch(t + ahead, lax.rem(t + ahead, PEER_ROW_BUFFERS))

        wait_rows(slot)
        x_row = x_scr[pl.ds(t, 1), :]
        act = jnp.sum(rows[slot, :, :d] * x_row, axis=1, keepdims=True)
        g_col = jnp.sum(jnp.where(eye, g_ref[0, pl.ds(t, 1), :], 0.0), axis=1, keepdims=True)
        w = g_col * _gelu_tanh(act)
        out_ref[0, pl.ds(t, 1), :] = jnp.sum(w * rows[slot, :, d:], axis=0, keepdims=True)
        return carry

    lax.fori_loop(0, tokens, body, 0)


def peer_mix(h, sc, sh, idx, g, uv_tab):
    b_, n, d = h.shape
    tb = min(PEER_MIX_TOKENS, n)
    mod_map = (lambda i, j: (i, 0, 0)) if sc.shape[0] == b_ else (lambda i, j: (0, 0, 0))
    return pl.pallas_call(
        _peer_mix_kernel,
        grid=(b_, n // tb),
        in_specs=[pl.BlockSpec(memory_space=pl.ANY),
                  pl.BlockSpec((1, tb, d), lambda i, j: (i, j, 0)),
                  pl.BlockSpec((1, 1, d), mod_map),
                  pl.BlockSpec((1, 1, d), mod_map),
                  pl.BlockSpec((1, tb, PEER_SLOTS), lambda i, j: (i, j, 0)),
                  pl.BlockSpec(memory_space=pl.ANY)],
        out_specs=pl.BlockSpec((1, tb, d), lambda i, j: (i, j, 0)),
        out_shape=jax.ShapeDtypeStruct(h.shape, F32),
        scratch_shapes=[pltpu.SMEM((tb * PEER_SLOTS,), jnp.int32),
                        pltpu.VMEM((tb, d), F32),
                        pltpu.VMEM((PEER_ROW_BUFFERS, PEER_SLOTS, 2 * d), F32),
                        pltpu.SemaphoreType.DMA((PEER_ROW_BUFFERS,)),
                        pltpu.SemaphoreType.DMA(())],
        compiler_params=pltpu.CompilerParams(vmem_limit_bytes=32 * 1024 * 1024),
        name="peer_mix",
    )(idx.reshape(-1), h, sc, sh, g, uv_tab)


def peer_ffn(h, sc, sh, w_q, sub_keys, uv_tab):
    idx, g = peer_select(h, sc, sh, w_q, sub_keys)
    return peer_mix(h, sc, sh, idx, g, uv_tab)


def kernel(x, c, ctx, c_ctx, ada_w, ada_b, ln_g, ln_b, da_w_in, da_w_out, da_lam_q, da_lam_k, da_subln,
           s5_lam_re, s5_lam_im, s5_log_dt, s5_b_re, s5_b_im, s5_c_re, s5_c_im, s5_d, s5_w_glu,
           hg_w_in, hg_w_out, hg_norm, hg_lb, peer_wq, peer_keys, peer_u, peer_v):
    L = x.shape[1]
    cos, sin = axial_rope(L, DA_HEAD_DIM)
    s_c = jax.nn.silu(c)
    s_ctx = jax.nn.silu(c_ctx)
    lb_soft = jax.nn.softmax(hg_lb.astype(F32), axis=0)
    lb_all = jnp.cumsum(lb_soft, axis=0) - lb_soft[0]
    h, hc = x, ctx
    for i in range(DEPTH):
        kind, slot = LAYER_TYPES[i], i // N_MIXERS
        need_ctx = i < DEPTH - 1
        mod = (s_c @ ada_w[i] + ada_b[i])[:, None, :]
        mod_c = s_ctx @ ada_w[i] + ada_b[i]
        sh1, sc1, g1, sh2, sc2, g2 = jnp.split(mod, 6, axis=-1)
        csh1, csc1, cg1, csh2, csc2, cg2 = jnp.split(mod_c, 6, axis=-1)
        u = h * (1.0 + sc1) + sh1
        uc = hc * (1.0 + csc1) + csh1
        if kind == 0:
            lam_init = 0.8 - 0.6 * math.exp(-0.3 * i)
            o, oc = diff_attention(uc, u, da_w_in[slot], da_w_out[slot], da_lam_q[slot], da_lam_k[slot],
                                   da_subln[slot], lam_init, cos, sin, need_ctx)
        elif kind == 1:
            o, oc = s5_mixer(uc, u, s5_lam_re[slot], s5_lam_im[slot], s5_log_dt[slot], s5_b_re[slot],
                             s5_b_im[slot], s5_c_re[slot], s5_c_im[slot], s5_d[slot], s5_w_glu[slot], need_ctx)
        else:
            o, oc = hgrn2_mixer(uc, u, hg_w_in[slot], hg_w_out[slot], hg_norm[slot], lb_all[i], need_ctx)
        h = residual_layer_norm(h, o, g1, ln_g[i, 0], ln_b[i, 0])
        uv_tab = jnp.concatenate([peer_u[i], peer_v[i]], axis=1)
        f = peer_ffn(h, sc2, sh2, peer_wq[i], peer_keys[i], uv_tab)
        h = residual_layer_norm(h, f, g2, ln_g[i, 1], ln_b[i, 1])
        if need_ctx:
            hc = residual_layer_norm(hc, oc, cg1.reshape(1, 1, -1), ln_g[i, 0], ln_b[i, 0])
            fc = peer_ffn(hc, csc2.reshape(1, 1, -1), csh2.reshape(1, 1, -1), peer_wq[i], peer_keys[i], uv_tab)
            hc = residual_layer_norm(hc, fc, cg2.reshape(1, 1, -1), ln_g[i, 1], ln_b[i, 1])
    return h
```

```python
import functools
import math
import jax, jax.numpy as jnp
from jax import lax
import numpy as np
from jax.experimental import pallas as pl
from jax.experimental.pallas import tpu as pltpu

D_MODEL = 1024
BATCH = 32
SEQ = 2048
DEPTH = 4

CTX_LEN = 256
GRID_W = 64
N_MIXERS = 3
LAYER_TYPES = tuple(i % N_MIXERS for i in range(DEPTH))
N_ATTN = LAYER_TYPES.count(0)
N_S5 = LAYER_TYPES.count(1)
N_HG = LAYER_TYPES.count(2)

DA_HEADS = 8
DA_HEAD_DIM = 64
DA_V_DIM = 2 * DA_HEAD_DIM
Q_BLOCK = 128
ROPE_THETA = 10000.0
S5_GROUP = 16
S5_GROUPS = D_MODEL // S5_GROUP
S5_STATE = 64
HG_HEADS = 8
HG_KEY = D_MODEL // HG_HEADS
HG_VAL = D_MODEL // HG_HEADS
HG_CHUNK = 32
PEER_HEADS = 8
PEER_NKEYS = 128
PEER_EXPERTS = PEER_NKEYS * PEER_NKEYS
PEER_QDIM = 256
PEER_TOPK = 16
PEER_BLOCK = 128
LN_EPS = 1e-5
RMS_EPS = 1e-6
DN_ALPHA = (2 * DEPTH) ** 0.25
DN_BETA = (8 * DEPTH) ** -0.25

F32 = jnp.float32
BF16 = jnp.bfloat16
LANES = 128
MOSAIC_VMEM_LIMIT = 48 * 1024 * 1024


def _gelu_tanh(x):
    return 0.5 * x * (1.0 + jnp.tanh(math.sqrt(2.0 / math.pi) * (x + 0.044715 * (x * x * x))))


def _res_ln_kernel(h_ref, o_ref, gate_ref, g_ref, b_ref, out_ref):
    y = DN_ALPHA * h_ref[0] + gate_ref[0] * o_ref[0]
    mu = jnp.mean(y, -1, keepdims=True)
    yc = y - mu
    var = jnp.mean(yc * yc, -1, keepdims=True)
    out_ref[0] = yc * lax.rsqrt(var + LN_EPS) * g_ref[...] + b_ref[...]


def residual_layer_norm(h, o, gate, g, b, block_n=512):
    b_, n, d = h.shape
    bn = min(block_n, n)
    per_batch_gate = gate.shape[0] == b_
    gate_map = (lambda i, j: (i, 0, 0)) if per_batch_gate else (lambda i, j: (0, 0, 0))
    return pl.pallas_call(
        _res_ln_kernel,
        grid=(b_, n // bn),
        in_specs=[pl.BlockSpec((1, bn, d), lambda i, j: (i, j, 0)),
                  pl.BlockSpec((1, bn, d), lambda i, j: (i, j, 0)),
                  pl.BlockSpec((1, 1, d), gate_map),
                  pl.BlockSpec((1, d), lambda i, j: (0, 0)),
                  pl.BlockSpec((1, d), lambda i, j: (0, 0))],
        out_specs=pl.BlockSpec((1, bn, d), lambda i, j: (i, j, 0)),
        out_shape=jax.ShapeDtypeStruct(h.shape, h.dtype),
        name="residual_layer_norm",
    )(h, o, gate, g.reshape(1, d), b.reshape(1, d))


def axial_rope(length, dim):
    rows = length // GRID_W
    row = jnp.repeat(jnp.arange(rows, dtype=F32), GRID_W)
    col = jnp.tile(jnp.arange(GRID_W, dtype=F32), rows)
    n_freq = dim // 4
    inv = ROPE_THETA ** (-jnp.arange(n_freq, dtype=F32) / n_freq)
    ang = jnp.concatenate([row[:, None] * inv, col[:, None] * inv], axis=-1)
    return jnp.cos(ang), jnp.sin(ang)


def _linear_kernel(x_ref, w_ref, o_ref):
    o_ref[0] = jnp.dot(x_ref[0].astype(BF16), w_ref[...], preferred_element_type=F32).astype(o_ref.dtype)


def linear(x, w, out_dtype=F32, block_rows=256, block_cols=1024):
    b_, n, kdim = x.shape
    ncols = w.shape[1]
    br, bc = min(block_rows, n), min(block_cols, ncols)
    assert n % br == 0 and ncols % bc == 0
    return pl.pallas_call(
        _linear_kernel,
        grid=(b_, n // br, ncols // bc),
        in_specs=[pl.BlockSpec((1, br, kdim), lambda i, j, c: (i, j, 0)),
                  pl.BlockSpec((kdim, bc), lambda i, j, c: (0, c))],
        out_specs=pl.BlockSpec((1, br, bc), lambda i, j, c: (i, j, c)),
        out_shape=jax.ShapeDtypeStruct((b_, n, ncols), out_dtype),
        compiler_params=pltpu.CompilerParams(vmem_limit_bytes=MOSAIC_VMEM_LIMIT),
        name="linear",
    )(x, w.astype(BF16))


DA_Q_TOKENS = 256
DA_PROJ_TOKENS = 256


def _da_qkv_kernel(x_ref, w_ref, wsw_ref, cos_ref, sin_ref, q_ref, k_ref, v_ref):
    d = q_ref.shape[-1]
    x = x_ref[0].astype(BF16)
    reps = 2 * d // cos_ref.shape[-1]
    c = jnp.concatenate([cos_ref[...]] * reps, axis=1)
    s = jnp.concatenate([sin_ref[...]] * reps, axis=1)
    qk = (jnp.dot(x, w_ref[:, :2 * d], preferred_element_type=F32) * c
          + jnp.dot(x, wsw_ref[...], preferred_element_type=F32) * s)
    q_ref[0] = (qk[:, :d] * DA_HEAD_DIM ** -0.5).astype(BF16)
    k_ref[0] = qk[:, d:].astype(BF16)
    v_ref[0] = jnp.dot(x, w_ref[:, 2 * d:], preferred_element_type=F32).astype(BF16)


def da_qkv(u_all, w_in, cos_t, sin_t):
    b_, n, d = u_all.shape
    tb = DA_PROJ_TOKENS
    swap = jnp.arange(2 * d) ^ 1
    w = w_in.astype(BF16)
    w_sw = w[:, :2 * d][:, swap]
    row = pl.BlockSpec((1, tb, d), lambda i, j: (i, j, 0))
    tab = pl.BlockSpec((tb, cos_t.shape[1]), lambda i, j: (j, 0))
    return pl.pallas_call(
        _da_qkv_kernel,
        grid=(b_, n // tb),
        in_specs=[row, pl.BlockSpec(w.shape, lambda i, j: (0, 0)), pl.BlockSpec(w_sw.shape, lambda i, j: (0, 0)),
                  tab, tab],
        out_specs=[row, row, row],
        out_shape=[jax.ShapeDtypeStruct((b_, n, d), BF16)] * 3,
        compiler_params=pltpu.CompilerParams(vmem_limit_bytes=MOSAIC_VMEM_LIMIT),
        name="da_qkv",
    )(u_all, w, w_sw, cos_t, sin_t)


def _da_attn_kernel(lam_ref, q_ref, k_ref, v_ref, g_ref, o_ref, *, post_scale):
    q, k, v = q_ref[0], k_ref[0], v_ref[0]
    lane = lax.broadcasted_iota(jnp.int32, q.shape, 1)
    zero = jnp.zeros_like(q)
    contract_last = (((1,), (1,)), ((), ()))

    def softmax_parts(qm):
        s = lax.dot_general(qm, k, contract_last, preferred_element_type=F32)
        e = jnp.exp(s - jnp.max(s, axis=-1, keepdims=True))
        return e, jnp.sum(e, axis=-1, keepdims=True)

    e0, z0 = softmax_parts(jnp.where(lane < DA_HEAD_DIM, q, zero))
    e1, z1 = softmax_parts(jnp.where(lane >= DA_HEAD_DIM, q, zero))
    a = e0 / z0 - lam_ref[0] * (e1 / z1)
    o = jnp.dot(a.astype(BF16), v, preferred_element_type=F32)
    o = o * lax.rsqrt(jnp.mean(o * o, axis=-1, keepdims=True) + RMS_EPS)
    o_ref[0] = (o * g_ref[...] * post_scale).astype(o_ref.dtype)


def da_attend(q, k, v, lam, subln_g, post_scale, q_start, n_q, n_k):
    b_, _, d = q.shape
    hd = d // DA_HEADS
    tq = DA_Q_TOKENS
    q0 = q_start // tq
    return pl.pallas_call(
        functools.partial(_da_attn_kernel, post_scale=post_scale),
        grid=(b_, DA_HEADS, n_q // tq),
        in_specs=[pl.BlockSpec(memory_space=pltpu.SMEM),
                  pl.BlockSpec((1, tq, hd), lambda i, h, j: (i, q0 + j, h)),
                  pl.BlockSpec((1, n_k, hd), lambda i, h, j: (i, 0, h)),
                  pl.BlockSpec((1, n_k, hd), lambda i, h, j: (i, 0, h)),
                  pl.BlockSpec((1, hd), lambda i, h, j: (0, 0))],
        out_specs=pl.BlockSpec((1, tq, hd), lambda i, h, j: (i, j, h)),
        out_shape=jax.ShapeDtypeStruct((b_, n_q, d), BF16),
        compiler_params=pltpu.CompilerParams(vmem_limit_bytes=MOSAIC_VMEM_LIMIT),
        name="da_attend",
    )(lam.reshape(1), q, k, v, subln_g.reshape(1, hd))


def diff_attention(u_ctx, u_lat, w_in, w_out, lam_q, lam_k, subln_g, lam_init, cos, sin, need_ctx):
    b_, n_ctx, d = u_ctx.shape
    n_lat = u_lat.shape[1]
    lanes_cos = jnp.tile(jnp.repeat(cos, 2, axis=1), (1, 2))
    lanes_sin = jnp.tile(jnp.stack([-sin, sin], axis=-1).reshape(n_lat, -1), (1, 2))
    cos_t = jnp.concatenate([jnp.ones((n_ctx, lanes_cos.shape[1]), F32), lanes_cos], axis=0)
    sin_t = jnp.concatenate([jnp.zeros((n_ctx, lanes_sin.shape[1]), F32), lanes_sin], axis=0)
    u_all = jnp.concatenate([u_ctx, u_lat], axis=1)
    q, k, v = da_qkv(u_all, w_in, cos_t, sin_t)
    lq, lk = lam_q.astype(F32), lam_k.astype(F32)
    lam = jnp.exp(jnp.sum(lq[0] * lk[0])) - jnp.exp(jnp.sum(lq[1] * lk[1])) + lam_init
    post = 1.0 - lam_init
    o_lat = linear(da_attend(q, k, v, lam, subln_g, post, n_ctx, n_lat, n_ctx + n_lat), w_out)
    o_ctx = linear(da_attend(q, k, v, lam, subln_g, post, 0, n_ctx, n_ctx), w_out) if need_ctx else None
    return o_lat, o_ctx


def s5_discretize(lam_re, lam_im, log_dt, b_re, b_im):
    lam_re, lam_im = lam_re.astype(F32), lam_im.astype(F32)
    b_re, b_im = b_re.astype(F32), b_im.astype(F32)
    dt = jnp.exp(log_dt.astype(F32))[:, None]
    mag = jnp.exp(lam_re * dt)
    abar_re, abar_im = mag * jnp.cos(lam_im * dt), mag * jnp.sin(lam_im * dt)
    nr, ni = abar_re - 1.0, abar_im
    den = lam_re * lam_re + lam_im * lam_im
    k_re = (nr * lam_re + ni * lam_im) / den
    k_im = (ni * lam_re - nr * lam_im) / den
    bb_re = k_re[..., None] * b_re - k_im[..., None] * b_im
    bb_im = k_re[..., None] * b_im + k_im[..., None] * b_re
    return abar_re, abar_im, bb_re, bb_im


S5_BATCH_TILE = 8
S5_SCAN_TOKENS = 128
S5_PANEL_GROUPS = 16
S5_PANELS = S5_GROUPS // S5_PANEL_GROUPS


def _s5_panels(bb, c):
    g, p, m = bb.shape
    pg = S5_PANEL_GROUPS
    eye = jnp.eye(pg, dtype=bb.dtype)
    w_in = jnp.einsum('qgpm,gh->qgmhp', bb.reshape(g // pg, pg, p, m), eye).reshape(g // pg, pg * m, pg * p)
    w_out = jnp.einsum('qgmp,gh->qgphm', c.reshape(g // pg, pg, m, p), eye).reshape(g // pg, pg * p, pg * m)
    return w_in.astype(BF16), w_out.astype(BF16)


def _s5_scan_kernel(u_ref, wb_re_ref, wb_im_ref, a_re_ref, a_im_ref, c_re_ref, c_im_ref, y_ref,
                    x_re, x_im, state, *, reverse):
    rows = u_ref.shape[1]
    tokens = rows // S5_BATCH_TILE
    cin = S5_PANEL_GROUPS * S5_GROUP

    @pl.when(pl.program_id(1) == 0)
    def _():
        state[...] = jnp.zeros_like(state)

    for p in range(S5_PANELS):
        ub = u_ref[0, :, p * cin:(p + 1) * cin].astype(BF16)
        x_re[...] = jnp.dot(ub, wb_re_ref[p], preferred_element_type=F32)
        x_im[...] = jnp.dot(ub, wb_im_ref[p], preferred_element_type=F32)
        a_re, a_im = a_re_ref[p], a_im_ref[p]

        def step(i, carry):
            xr, xi = carry
            t = (tokens - 1 - i) if reverse else i
            r0 = pl.multiple_of(t * S5_BATCH_TILE, S5_BATCH_TILE)
            nr = a_re * xr - a_im * xi + x_re[pl.ds(r0, S5_BATCH_TILE), :]
            ni = a_re * xi + a_im * xr + x_im[pl.ds(r0, S5_BATCH_TILE), :]
            x_re[pl.ds(r0, S5_BATCH_TILE), :] = nr
            x_im[pl.ds(r0, S5_BATCH_TILE), :] = ni
            return nr, ni

        xr, xi = lax.fori_loop(0, tokens, step, (state[p, 0], state[p, 1]), unroll=2)
        state[p, 0] = xr
        state[p, 1] = xi
        y_ref[0, :, p * cin:(p + 1) * cin] = (
            jnp.dot(x_re[...].astype(BF16), c_re_ref[p], preferred_element_type=F32)
            - jnp.dot(x_im[...].astype(BF16), c_im_ref[p], preferred_element_type=F32))


def s5_scan(u_g, lam_re, lam_im, log_dt, b_re, b_im, c_re, c_im, n_ctx, reverse):
    nbg, rows_total, d = u_g.shape
    abar_re, abar_im, bb_re, bb_im = s5_discretize(lam_re, lam_im, log_dt, b_re, b_im)
    wb_re, cp_re = _s5_panels(bb_re, c_re.astype(F32))
    wb_im, cp_im = _s5_panels(bb_im, c_im.astype(F32))
    states = S5_PANEL_GROUPS * S5_STATE
    tile = lambda a: jnp.broadcast_to(a.reshape(S5_PANELS, 1, states), (S5_PANELS, S5_BATCH_TILE, states))
    rows = S5_SCAN_TOKENS * S5_BATCH_TILE
    n_chunks = rows_total // rows
    ctx_chunks = n_ctx // S5_SCAN_TOKENS
    if reverse:
        chunk = lambda s: jnp.where(s < ctx_chunks, ctx_chunks - 1 - s, n_chunks - 1 - (s - ctx_chunks))
    else:
        chunk = lambda s: s
    full = lambda a: pl.BlockSpec(a.shape, lambda i, s: (0,) * a.ndim)
    a_re_t, a_im_t = tile(abar_re), tile(abar_im)
    return pl.pallas_call(
        functools.partial(_s5_scan_kernel, reverse=reverse),
        grid=(nbg, n_chunks),
        in_specs=[pl.BlockSpec((1, rows, d), lambda i, s: (i, chunk(s), 0)),
                  full(wb_re), full(wb_im), full(a_re_t), full(a_im_t), full(cp_re), full(cp_im)],
        out_specs=pl.BlockSpec((1, rows, d), lambda i, s: (i, chunk(s), 0)),
        out_shape=jax.ShapeDtypeStruct(u_g.shape, F32),
        scratch_shapes=[pltpu.VMEM((rows, states), F32), pltpu.VMEM((rows, states), F32),
                        pltpu.VMEM((S5_PANELS, 2, S5_BATCH_TILE, states), F32)],
        compiler_params=pltpu.CompilerParams(dimension_semantics=("arbitrary", "arbitrary"),
                                             vmem_limit_bytes=MOSAIC_VMEM_LIMIT),
        name="s5_scan_bwd" if reverse else "s5_scan_fwd",
    )(u_g, wb_re, wb_im, a_re_t, a_im_t, cp_re, cp_im)


def _s5_glu_kernel(yf_ref, yb_ref, u_ref, d_ref, w_ref, o_ref):
    d = o_ref.shape[-1]
    y = yf_ref[0] + yb_ref[0] + d_ref[...] * u_ref[0]
    r = jnp.dot(_gelu_tanh(y).astype(BF16), w_ref[...], preferred_element_type=F32)
    o_ref[0] = r[:, :d] * jax.nn.sigmoid(r[:, d:])


def s5_glu(y_fw, y_bw, u_g, d_skip, w_glu, block_rows=512):
    nbg, rows_total, d = u_g.shape
    blk = pl.BlockSpec((1, block_rows, d), lambda i, j: (i, j, 0))
    return pl.pallas_call(
        _s5_glu_kernel,
        grid=(nbg, rows_total // block_rows),
        in_specs=[blk, blk, blk,
                  pl.BlockSpec((1, d), lambda i, j: (0, 0)),
                  pl.BlockSpec(w_glu.shape, lambda i, j: (0, 0))],
        out_specs=blk,
        out_shape=jax.ShapeDtypeStruct(u_g.shape, F32),
        compiler_params=pltpu.CompilerParams(vmem_limit_bytes=MOSAIC_VMEM_LIMIT),
        name="s5_glu",
    )(y_fw, y_bw, u_g, d_skip.reshape(1, d), w_glu.astype(BF16))


def s5_mixer(u_ctx, u_lat, lam_re, lam_im, log_dt, b_re, b_im, c_re, c_im, d_skip, w_glu, need_ctx):
    b_, n_ctx, d = u_ctx.shape
    n_all = n_ctx + u_lat.shape[1]
    bt = S5_BATCH_TILE
    u_all = jnp.concatenate([u_ctx, u_lat], axis=1)
    u_g = u_all.reshape(b_ // bt, bt, n_all, d).transpose(0, 2, 1, 3).reshape(b_ // bt, n_all * bt, d)
    ys = [s5_scan(u_g, lam_re[dr], lam_im[dr], log_dt[dr], b_re[dr], b_im[dr], c_re[dr], c_im[dr],
                  n_ctx, reverse=bool(dr)) for dr in range(2)]
    o_g = s5_glu(ys[0], ys[1], u_g, d_skip, w_glu)
    o = o_g.reshape(b_ // bt, n_all, bt, d).transpose(0, 2, 1, 3).reshape(b_, n_all, d)
    return o[:, n_ctx:], (o[:, :n_ctx] if need_ctx else None)


HG_SCAN_TOKENS = 256
HG_SUB = 32


def _hgrn_scan_kernel(q_ref, f_ref, v_ref, lb_ref, tri_ref, o_ref, q_s, k_s, cum_s, v_s, st, *, reverse):
    tokens = q_ref.shape[1]

    @pl.when(pl.program_id(2) == 0)
    def _():
        st[...] = jnp.zeros_like(st)

    f = f_ref[0]
    log_f = jnp.logaddexp(lb_ref[0:1, :], lb_ref[1:2, :] + jax.nn.log_sigmoid(f))
    k_s[...] = lb_ref[2:3, :] * jax.nn.sigmoid(-f)
    q_s[...] = jax.nn.silu(q_ref[0])
    v_s[...] = v_ref[0]
    cum_s[...] = jnp.dot(tri_ref[...], log_f, precision=lax.Precision.HIGHEST, preferred_element_type=F32)
    tpos = lax.broadcasted_iota(jnp.int32, (HG_SUB, q_ref.shape[2]), 0)
    nsub = tokens // HG_SUB
    ones = jnp.ones((q_ref.shape[2], v_ref.shape[2]), BF16)

    def block(i, carry):
        c = (nsub - 1 - i) if reverse else i
        base = pl.multiple_of(c * HG_SUB, HG_SUB)
        qc = q_s[pl.ds(base, HG_SUB), :]
        cumc = cum_s[pl.ds(base, HG_SUB), :]

        def column(s, o):
            ks = k_s[pl.ds(base + s, 1), :]
            cs = cum_s[pl.ds(base + s, 1), :]
            vs = v_s[pl.ds(base + s, 1), :]
            seen = (tpos <= s) if reverse else (tpos >= s)
            decay = jnp.exp(jnp.where(seen, cumc - cs, -jnp.inf))
            att = jnp.dot((qc * ks * decay).astype(BF16), ones, preferred_element_type=F32)
            return o + att * vs

        o = lax.fori_loop(0, HG_SUB, column, jnp.zeros(qc.shape, F32), unroll=8)
        s_t = st[...]
        o = o + lax.dot_general((qc * jnp.exp(cumc)).astype(BF16), s_t.astype(BF16),
                                (((1,), (1,)), ((), ())), preferred_element_type=F32)
        last = cum_s[pl.ds(base + (0 if reverse else HG_SUB - 1), 1), :]
        kh = (k_s[pl.ds(base, HG_SUB), :] * jnp.exp(last - cumc)).astype(BF16)
        vc = v_s[pl.ds(base, HG_SUB), :].astype(BF16)
        st[...] = s_t * jnp.exp(last) + lax.dot_general(vc, kh, (((0,), (0,)), ((), ())),
                                                        preferred_element_type=F32)
        o_ref[0, pl.ds(base, HG_SUB), :] = o
        return carry

    lax.fori_loop(0, nsub, block, 0)


def hgrn_scan(proj, lb_rows, n_ctx, f_col, reverse):
    b_, n, _ = proj.shape
    t = HG_SCAN_TOKENS
    hk = HG_KEY
    n_chunks, ctx_chunks = n // t, n_ctx // t
    if reverse:
        chunk = lambda s: jnp.where(s < ctx_chunks, ctx_chunks - 1 - s, n_chunks - 1 - (s - ctx_chunks))
    else:
        chunk = lambda s: s
    pos = jnp.arange(t)
    same = (pos[:, None] // HG_SUB) == (pos[None, :] // HG_SUB)
    order = (pos[None, :] >= pos[:, None]) if reverse else (pos[None, :] <= pos[:, None])
    tri = (same & order).astype(F32)
    col = lambda off: pl.BlockSpec((1, t, hk), lambda i, h, s: (i, chunk(s), off * HG_HEADS + h))
    return pl.pallas_call(
        functools.partial(_hgrn_scan_kernel, reverse=reverse),
        grid=(b_, HG_HEADS, n_chunks),
        in_specs=[col(0), col(f_col), col(3),
                  pl.BlockSpec((3, hk), lambda i, h, s: (0, h)),
                  pl.BlockSpec((t, t), lambda i, h, s: (0, 0))],
        out_specs=pl.BlockSpec((1, t, hk), lambda i, h, s: (i, chunk(s), h)),
        out_shape=jax.ShapeDtypeStruct((b_, n, HG_HEADS * HG_VAL), F32),
        scratch_shapes=[pltpu.VMEM((t, hk), F32)] * 4 + [pltpu.VMEM((HG_VAL, hk), F32)],
        compiler_params=pltpu.CompilerParams(dimension_semantics=("arbitrary", "arbitrary", "arbitrary"),
                                             vmem_limit_bytes=MOSAIC_VMEM_LIMIT),
        name="hgrn_scan_bwd" if reverse else "hgrn_scan_fwd",
    )(proj, proj, proj, lb_rows, tri)


def _hgrn_out_kernel(of_ref, ob_ref, gate_ref, g_ref, w_ref, o_ref):
    o = of_ref[0] + ob_ref[0]
    parts = []
    for h in range(HG_HEADS):
        oh = o[:, h * HG_VAL:(h + 1) * HG_VAL]
        parts.append(oh * lax.rsqrt(jnp.mean(oh * oh, axis=-1, keepdims=True) + RMS_EPS))
    y = jnp.concatenate(parts, axis=1) * g_ref[...] * jax.nn.silu(gate_ref[0])
    o_ref[0] = jnp.dot(y.astype(BF16), w_ref[...], preferred_element_type=F32)


def hgrn_out(o_fw, o_bw, proj, norm_g, w_out, block_rows=256):
    b_, n, d = o_fw.shape
    assert n % block_rows == 0
    row = pl.BlockSpec((1, block_rows, d), lambda i, j: (i, j, 0))
    return pl.pallas_call(
        _hgrn_out_kernel,
        grid=(b_, n // block_rows),
        in_specs=[row, row, pl.BlockSpec((1, block_rows, d), lambda i, j: (i, j, 4)),
                  pl.BlockSpec((1, d), lambda i, j: (0, 0)), pl.BlockSpec(w_out.shape, lambda i, j: (0, 0))],
        out_specs=row,
        out_shape=jax.ShapeDtypeStruct((b_, n, d), F32),
        compiler_params=pltpu.CompilerParams(vmem_limit_bytes=MOSAIC_VMEM_LIMIT),
        name="hgrn_out",
    )(o_fw, o_bw, proj, norm_g.reshape(1, d), w_out.astype(BF16))


def hgrn2_mixer(u_ctx, u_lat, w_in, w_out, norm_g, lb, need_ctx):
    n_ctx = u_ctx.shape[1]
    proj = linear(jnp.concatenate([u_ctx, u_lat], axis=1), w_in)
    lb_rows = jnp.stack([jnp.log(lb), jnp.log1p(-lb), 1.0 - lb])
    o_fw = hgrn_scan(proj, lb_rows, n_ctx, 1, reverse=False)
    o_bw = hgrn_scan(proj, lb_rows, n_ctx, 2, reverse=True)
    o = hgrn_out(o_fw, o_bw, proj, norm_g, w_out)
    return o[:, n_ctx:], (o[:, :n_ctx] if need_ctx else None)


PEER_SLOTS = PEER_HEADS * PEER_TOPK
PEER_SEL_TOKENS = 256
PEER_MIX_TOKENS = 128
PEER_ROW_BUFFERS = 4


def _topk_axis0(cur, k, payload=None):
    rows = cur.shape[0]
    iota = lax.broadcasted_iota(jnp.int32, cur.shape, 0)
    vals, picks = [], []
    for _ in range(k):
        m = jnp.max(cur, axis=0, keepdims=True)
        pos = jnp.min(jnp.where(cur == m, iota, rows), axis=0, keepdims=True)
        hit = iota == pos
        vals.append(m)
        if payload is None:
            picks.append(pos)
        else:
            picks.append(jnp.sum(jnp.where(hit, payload, 0), axis=0, keepdims=True))
        cur = jnp.where(hit, -jnp.inf, cur)
    return jnp.concatenate(vals, axis=0), jnp.concatenate(picks, axis=0)


def _peer_select_kernel(h_ref, sc_ref, sh_ref, wq_ref, keys_ref, idx_ref, g_ref):
    half = PEER_QDIM // 2
    x = h_ref[0] * (1.0 + sc_ref[0]) + sh_ref[0]
    q = jnp.dot(x.astype(BF16), wq_ref[...], preferred_element_type=F32)
    tokens = x.shape[0]
    for c0 in range(0, tokens, LANES):
        idx_rows, g_rows = [], []
        for hd in range(PEER_HEADS):
            tops = []
            for c in range(2):
                lo = (hd * 2 + c) * half
                qhc = q[c0:c0 + LANES, lo:lo + half].astype(BF16)
                s_t = lax.dot_general(keys_ref[c], qhc, (((1,), (1,)), ((), ())),
                                      preferred_element_type=F32)
                tops.append(_topk_axis0(s_t, PEER_TOPK))
            (s1, i1), (s2, i2) = tops
            cand_s = jnp.concatenate([s1[a:a + 1] + s2 for a in range(PEER_TOPK)], axis=0)
            cand_i = jnp.concatenate([i1[a:a + 1] * PEER_NKEYS + i2 for a in range(PEER_TOPK)], axis=0)
            top_s, top_i = _topk_axis0(cand_s, PEER_TOPK, payload=cand_i)
            e = jnp.exp(top_s - top_s[0:1])
            g_rows.append(e / jnp.sum(e, axis=0, keepdims=True))
            idx_rows.append(top_i)
        idx_ref[0, c0:c0 + LANES, :] = jnp.concatenate(idx_rows, axis=0).T
        g_ref[0, c0:c0 + LANES, :] = jnp.concatenate(g_rows, axis=0).T


def peer_select(h, sc, sh, w_q, sub_keys):
    b_, n, d = h.shape
    tb = min(PEER_SEL_TOKENS, n)
    mod_map = (lambda i, j: (i, 0, 0)) if sc.shape[0] == b_ else (lambda i, j: (0, 0, 0))
    return pl.pallas_call(
        _peer_select_kernel,
        grid=(b_, n // tb),
        in_specs=[pl.BlockSpec((1, tb, d), lambda i, j: (i, j, 0)),
                  pl.BlockSpec((1, 1, d), mod_map),
                  pl.BlockSpec((1, 1, d), mod_map),
                  pl.BlockSpec(w_q.shape, lambda i, j: (0, 0)),
                  pl.BlockSpec(sub_keys.shape, lambda i, j: (0, 0, 0))],
        out_specs=[pl.BlockSpec((1, tb, PEER_SLOTS), lambda i, j: (i, j, 0)),
                   pl.BlockSpec((1, tb, PEER_SLOTS), lambda i, j: (i, j, 0))],
        out_shape=[jax.ShapeDtypeStruct((b_, n, PEER_SLOTS), jnp.int32),
                   jax.ShapeDtypeStruct((b_, n, PEER_SLOTS), F32)],
        compiler_params=pltpu.CompilerParams(vmem_limit_bytes=48 * 1024 * 1024),
        name="peer_select",
    )(h, sc, sh, w_q.astype(BF16), sub_keys.astype(BF16))


def peer_table(u_tab, v_tab):
    return jnp.concatenate([u_tab, v_tab], axis=1)[:, None, :]


def _peer_mix_kernel(idx_hbm, h_ref, sc_ref, sh_ref, g_ref, tab_hbm, out_ref,
                     idx_smem, x_scr, rows, row_sem, idx_sem):
    tokens, d = x_scr.shape
    nbuf = PEER_ROW_BUFFERS
    ahead = nbuf - 1
    blk = pl.program_id(0) * pl.num_programs(1) + pl.program_id(1)
    per_blk = tokens * PEER_SLOTS
    idx_copy = pltpu.make_async_copy(idx_hbm.at[pl.ds(pl.multiple_of(blk * per_blk, per_blk), per_blk)],
                                     idx_smem, idx_sem)
    idx_copy.start()
    x_scr[...] = h_ref[0] * (1.0 + sc_ref[0]) + sh_ref[0]
    idx_copy.wait()

    def fetch(t, slot):
        for j in range(PEER_SLOTS):
            pltpu.make_async_copy(tab_hbm.at[idx_smem[t * PEER_SLOTS + j]],
                                  rows.at[slot, pl.ds(j, 1), :], row_sem.at[slot]).start()

    def wait_rows(slot):
        pltpu.make_async_copy(rows.at[slot], rows.at[slot], row_sem.at[slot]).wait()

    eye = (lax.broadcasted_iota(jnp.int32, (PEER_SLOTS, PEER_SLOTS), 0)
           == lax.broadcasted_iota(jnp.int32, (PEER_SLOTS, PEER_SLOTS), 1))

    def combine(t, slot):
        x_row = x_scr[pl.ds(t, 1), :]
        act = jnp.sum(rows[slot, :, :d] * x_row, axis=1, keepdims=True)
        g_col = jnp.sum(jnp.where(eye, g_ref[0, pl.ds(t, 1), :], 0.0), axis=1, keepdims=True)
        w = g_col * _gelu_tanh(act)
        out_ref[0, pl.ds(t, 1), :] = jnp.sum(w * rows[slot, :, d:], axis=0, keepdims=True)

    def token(t, slot, prefetch):
        if prefetch:
            fetch(t + ahead, (slot + ahead) % nbuf)
        wait_rows(slot)
        combine(t, slot)

    for t0 in range(ahead):
        fetch(t0, t0)

    def group(i, carry):
        for slot in range(nbuf):
            token(i * nbuf + slot, slot, True)
        return carry

    lax.fori_loop(0, tokens // nbuf - 1, group, 0)
    for slot in range(nbuf):
        token(tokens - nbuf + slot, slot, slot + ahead < nbuf)


def peer_mix(h, sc, sh, idx, g, uv_tab):
    b_, n, d = h.shape
    tb = min(PEER_MIX_TOKENS, n)
    assert n % tb == 0 and tb % PEER_ROW_BUFFERS == 0
    mod_map = (lambda i, j: (i, 0, 0)) if sc.shape[0] == b_ else (lambda i, j: (0, 0, 0))
    return pl.pallas_call(
        _peer_mix_kernel,
        grid=(b_, n // tb),
        in_specs=[pl.BlockSpec(memory_space=pl.ANY),
                  pl.BlockSpec((1, tb, d), lambda i, j: (i, j, 0)),
                  pl.BlockSpec((1, 1, d), mod_map),
                  pl.BlockSpec((1, 1, d), mod_map),
                  pl.BlockSpec((1, tb, PEER_SLOTS), lambda i, j: (i, j, 0)),
                  pl.BlockSpec(memory_space=pl.ANY)],
        out_specs=pl.BlockSpec((1, tb, d), lambda i, j: (i, j, 0)),
        out_shape=jax.ShapeDtypeStruct(h.shape, F32),
        scratch_shapes=[pltpu.SMEM((tb * PEER_SLOTS,), jnp.int32),
                        pltpu.VMEM((tb, d), F32),
                        pltpu.VMEM((PEER_ROW_BUFFERS, PEER_SLOTS, 2 * d), F32),
                        pltpu.SemaphoreType.DMA((PEER_ROW_BUFFERS,)),
                        pltpu.SemaphoreType.DMA(())],
        compiler_params=pltpu.CompilerParams(vmem_limit_bytes=32 * 1024 * 1024),
        name="peer_mix",
    )(idx.reshape(-1), h, sc, sh, g, uv_tab)


def peer_ffn(h, sc, sh, w_q, sub_keys, uv_tab):
    idx, g = peer_select(h, sc, sh, w_q, sub_keys)
    return peer_mix(h, sc, sh, idx, g, uv_tab)


def kernel(x, c, ctx, c_ctx, ada_w, ada_b, ln_g, ln_b, da_w_in, da_w_out, da_lam_q, da_lam_k, da_subln,
           s5_lam_re, s5_lam_im, s5_log_dt, s5_b_re, s5_b_im, s5_c_re, s5_c_im, s5_d, s5_w_glu,
           hg_w_in, hg_w_out, hg_norm, hg_lb, peer_wq, peer_keys, peer_u, peer_v):
    L = x.shape[1]
    cos, sin = axial_rope(L, DA_HEAD_DIM)
    s_c = jax.nn.silu(c)
    s_ctx = jax.nn.silu(c_ctx)
    lb_soft = jax.nn.softmax(hg_lb.astype(F32), axis=0)
    lb_all = jnp.cumsum(lb_soft, axis=0) - lb_soft[0]
    h, hc = x, ctx
    for i in range(DEPTH):
        kind, slot = LAYER_TYPES[i], i // N_MIXERS
        need_ctx = i < DEPTH - 1
        mod = (s_c @ ada_w[i] + ada_b[i])[:, None, :]
        mod_c = s_ctx @ ada_w[i] + ada_b[i]
        sh1, sc1, g1, sh2, sc2, g2 = jnp.split(mod, 6, axis=-1)
        csh1, csc1, cg1, csh2, csc2, cg2 = jnp.split(mod_c, 6, axis=-1)
        u = h * (1.0 + sc1) + sh1
        uc = hc * (1.0 + csc1) + csh1
        if kind == 0:
            lam_init = 0.8 - 0.6 * math.exp(-0.3 * i)
            o, oc = diff_attention(uc, u, da_w_in[slot], da_w_out[slot], da_lam_q[slot], da_lam_k[slot],
                                   da_subln[slot], lam_init, cos, sin, need_ctx)
        elif kind == 1:
            o, oc = s5_mixer(uc, u, s5_lam_re[slot], s5_lam_im[slot], s5_log_dt[slot], s5_b_re[slot],
                             s5_b_im[slot], s5_c_re[slot], s5_c_im[slot], s5_d[slot], s5_w_glu[slot], need_ctx)
        else:
            o, oc = hgrn2_mixer(uc, u, hg_w_in[slot], hg_w_out[slot], hg_norm[slot], lb_all[i], need_ctx)
        h = residual_layer_norm(h, o, g1, ln_g[i, 0], ln_b[i, 0])
        uv_tab = peer_table(peer_u[i], peer_v[i])
        f = peer_ffn(h, sc2, sh2, peer_wq[i], peer_keys[i], uv_tab)
        h = residual_layer_norm(h, f, g2, ln_g[i, 1], ln_b[i, 1])
        if need_ctx:
            hc = residual_layer_norm(hc, oc, cg1.reshape(1, 1, -1), ln_g[i, 0], ln_b[i, 0])
            fc = peer_ffn(hc, csc2.reshape(1, 1, -1), csh2.reshape(1, 1, -1), peer_wq[i], peer_keys[i], uv_tab)
            hc = residual_layer_norm(hc, fc, cg2.reshape(1, 1, -1), ln_g[i, 1], ln_b[i, 1])
    return h
```

```python
import functools
import math
import jax, jax.numpy as jnp
from jax import lax
import numpy as np
from jax.experimental import pallas as pl
from jax.experimental.pallas import tpu as pltpu

D_MODEL = 1024
BATCH = 32
SEQ = 2048
DEPTH = 4

CTX_LEN = 256
GRID_W = 64
N_MIXERS = 3
LAYER_TYPES = tuple(i % N_MIXERS for i in range(DEPTH))
N_ATTN = LAYER_TYPES.count(0)
N_S5 = LAYER_TYPES.count(1)
N_HG = LAYER_TYPES.count(2)

DA_HEADS = 8
DA_HEAD_DIM = 64
DA_V_DIM = 2 * DA_HEAD_DIM
Q_BLOCK = 128
ROPE_THETA = 10000.0
S5_GROUP = 16
S5_GROUPS = D_MODEL // S5_GROUP
S5_STATE = 64
HG_HEADS = 8
HG_KEY = D_MODEL // HG_HEADS
HG_VAL = D_MODEL // HG_HEADS
HG_CHUNK = 32
PEER_HEADS = 8
PEER_NKEYS = 128
PEER_EXPERTS = PEER_NKEYS * PEER_NKEYS
PEER_QDIM = 256
PEER_TOPK = 16
PEER_BLOCK = 128
LN_EPS = 1e-5
RMS_EPS = 1e-6
DN_ALPHA = (2 * DEPTH) ** 0.25
DN_BETA = (8 * DEPTH) ** -0.25

F32 = jnp.float32
BF16 = jnp.bfloat16
LANES = 128
MOSAIC_VMEM_LIMIT = 48 * 1024 * 1024


def _gelu_tanh(x):
    return 0.5 * x * (1.0 + jnp.tanh(math.sqrt(2.0 / math.pi) * (x + 0.044715 * (x * x * x))))


def _res_ln_kernel(h_ref, o_ref, gate_ref, g_ref, b_ref, out_ref):
    y = DN_ALPHA * h_ref[0] + gate_ref[0] * o_ref[0]
    mu = jnp.mean(y, -1, keepdims=True)
    yc = y - mu
    var = jnp.mean(yc * yc, -1, keepdims=True)
    out_ref[0] = yc * lax.rsqrt(var + LN_EPS) * g_ref[...] + b_ref[...]


def residual_layer_norm(h, o, gate, g, b, block_n=512):
    b_, n, d = h.shape
    bn = min(block_n, n)
    per_batch_gate = gate.shape[0] == b_
    gate_map = (lambda i, j: (i, 0, 0)) if per_batch_gate else (lambda i, j: (0, 0, 0))
    return pl.pallas_call(
        _res_ln_kernel,
        grid=(b_, n // bn),
        in_specs=[pl.BlockSpec((1, bn, d), lambda i, j: (i, j, 0)),
                  pl.BlockSpec((1, bn, d), lambda i, j: (i, j, 0)),
                  pl.BlockSpec((1, 1, d), gate_map),
                  pl.BlockSpec((1, d), lambda i, j: (0, 0)),
                  pl.BlockSpec((1, d), lambda i, j: (0, 0))],
        out_specs=pl.BlockSpec((1, bn, d), lambda i, j: (i, j, 0)),
        out_shape=jax.ShapeDtypeStruct(h.shape, h.dtype),
        name="residual_layer_norm",
    )(h, o, gate, g.reshape(1, d), b.reshape(1, d))


def axial_rope(length, dim):
    rows = length // GRID_W
    row = jnp.repeat(jnp.arange(rows, dtype=F32), GRID_W)
    col = jnp.tile(jnp.arange(GRID_W, dtype=F32), rows)
    n_freq = dim // 4
    inv = ROPE_THETA ** (-jnp.arange(n_freq, dtype=F32) / n_freq)
    ang = jnp.concatenate([row[:, None] * inv, col[:, None] * inv], axis=-1)
    return jnp.cos(ang), jnp.sin(ang)


def _linear_kernel(x_ref, w_ref, o_ref):
    o_ref[0] = jnp.dot(x_ref[0].astype(BF16), w_ref[...], preferred_element_type=F32).astype(o_ref.dtype)


def linear(x, w, out_dtype=F32, block_rows=256, block_cols=1024):
    b_, n, kdim = x.shape
    ncols = w.shape[1]
    br, bc = min(block_rows, n), min(block_cols, ncols)
    assert n % br == 0 and ncols % bc == 0
    return pl.pallas_call(
        _linear_kernel,
        grid=(b_, n // br, ncols // bc),
        in_specs=[pl.BlockSpec((1, br, kdim), lambda i, j, c: (i, j, 0)),
                  pl.BlockSpec((kdim, bc), lambda i, j, c: (0, c))],
        out_specs=pl.BlockSpec((1, br, bc), lambda i, j, c: (i, j, c)),
        out_shape=jax.ShapeDtypeStruct((b_, n, ncols), out_dtype),
        compiler_params=pltpu.CompilerParams(vmem_limit_bytes=MOSAIC_VMEM_LIMIT),
        name="linear",
    )(x, w.astype(BF16))


DA_Q_TOKENS = 256
DA_PROJ_TOKENS = 256


def _da_qkv_kernel(x_ref, w_ref, wsw_ref, cos_ref, sin_ref, q_ref, k_ref, v_ref):
    d = q_ref.shape[-1]
    x = x_ref[0].astype(BF16)
    reps = 2 * d // cos_ref.shape[-1]
    c = jnp.concatenate([cos_ref[...]] * reps, axis=1)
    s = jnp.concatenate([sin_ref[...]] * reps, axis=1)
    qk = (jnp.dot(x, w_ref[:, :2 * d], preferred_element_type=F32) * c
          + jnp.dot(x, wsw_ref[...], preferred_element_type=F32) * s)
    q_ref[0] = (qk[:, :d] * DA_HEAD_DIM ** -0.5).astype(BF16)
    k_ref[0] = qk[:, d:].astype(BF16)
    v_ref[0] = jnp.dot(x, w_ref[:, 2 * d:], preferred_element_type=F32).astype(BF16)


def da_qkv(u_all, w_in, cos_t, sin_t):
    b_, n, d = u_all.shape
    tb = DA_PROJ_TOKENS
    swap = jnp.arange(2 * d) ^ 1
    w = w_in.astype(BF16)
    w_sw = w[:, :2 * d][:, swap]
    row = pl.BlockSpec((1, tb, d), lambda i, j: (i, j, 0))
    tab = pl.BlockSpec((tb, cos_t.shape[1]), lambda i, j: (j, 0))
    return pl.pallas_call(
        _da_qkv_kernel,
        grid=(b_, n // tb),
        in_specs=[row, pl.BlockSpec(w.shape, lambda i, j: (0, 0)), pl.BlockSpec(w_sw.shape, lambda i, j: (0, 0)),
                  tab, tab],
        out_specs=[row, row, row],
        out_shape=[jax.ShapeDtypeStruct((b_, n, d), BF16)] * 3,
        compiler_params=pltpu.CompilerParams(vmem_limit_bytes=MOSAIC_VMEM_LIMIT),
        name="da_qkv",
    )(u_all, w, w_sw, cos_t, sin_t)


def _da_attn_kernel(lam_ref, q_ref, k_ref, v_ref, g_ref, o_ref, *, post_scale):
    q, k, v = q_ref[0], k_ref[0], v_ref[0]
    lane = lax.broadcasted_iota(jnp.int32, q.shape, 1)
    zero = jnp.zeros_like(q)
    contract_last = (((1,), (1,)), ((), ()))

    def softmax_parts(qm):
        s = lax.dot_general(qm, k, contract_last, preferred_element_type=F32)
        e = jnp.exp(s - jnp.max(s, axis=-1, keepdims=True))
        return e, jnp.sum(e, axis=-1, keepdims=True)

    e0, z0 = softmax_parts(jnp.where(lane < DA_HEAD_DIM, q, zero))
    e1, z1 = softmax_parts(jnp.where(lane >= DA_HEAD_DIM, q, zero))
    a = e0 / z0 - lam_ref[0] * (e1 / z1)
    o = jnp.dot(a.astype(BF16), v, preferred_element_type=F32)
    o = o * lax.rsqrt(jnp.mean(o * o, axis=-1, keepdims=True) + RMS_EPS)
    o_ref[0] = (o * g_ref[...] * post_scale).astype(o_ref.dtype)


def da_attend(q, k, v, lam, subln_g, post_scale, q_start, n_q, n_k):
    b_, _, d = q.shape
    hd = d // DA_HEADS
    tq = DA_Q_TOKENS
    q0 = q_start // tq
    return pl.pallas_call(
        functools.partial(_da_attn_kernel, post_scale=post_scale),
        grid=(b_, DA_HEADS, n_q // tq),
        in_specs=[pl.BlockSpec(memory_space=pltpu.SMEM),
                  pl.BlockSpec((1, tq, hd), lambda i, h, j: (i, q0 + j, h)),
                  pl.BlockSpec((1, n_k, hd), lambda i, h, j: (i, 0, h)),
                  pl.BlockSpec((1, n_k, hd), lambda i, h, j: (i, 0, h)),
                  pl.BlockSpec((1, hd), lambda i, h, j: (0, 0))],
        out_specs=pl.BlockSpec((1, tq, hd), lambda i, h, j: (i, j, h)),
        out_shape=jax.ShapeDtypeStruct((b_, n_q, d), BF16),
        compiler_params=pltpu.CompilerParams(vmem_limit_bytes=MOSAIC_VMEM_LIMIT),
        name="da_attend",
    )(lam.reshape(1), q, k, v, subln_g.reshape(1, hd))


def diff_attention(u_ctx, u_lat, w_in, w_out, lam_q, lam_k, subln_g, lam_init, cos, sin, need_ctx):
    b_, n_ctx, d = u_ctx.shape
    n_lat = u_lat.shape[1]
    lanes_cos = jnp.tile(jnp.repeat(cos, 2, axis=1), (1, 2))
    lanes_sin = jnp.tile(jnp.stack([-sin, sin], axis=-1).reshape(n_lat, -1), (1, 2))
    cos_t = jnp.concatenate([jnp.ones((n_ctx, lanes_cos.shape[1]), F32), lanes_cos], axis=0)
    sin_t = jnp.concatenate([jnp.zeros((n_ctx, lanes_sin.shape[1]), F32), lanes_sin], axis=0)
    u_all = jnp.concatenate([u_ctx, u_lat], axis=1)
    q, k, v = da_qkv(u_all, w_in, cos_t, sin_t)
    lq, lk = lam_q.astype(F32), lam_k.astype(F32)
    lam = jnp.exp(jnp.sum(lq[0] * lk[0])) - jnp.exp(jnp.sum(lq[1] * lk[1])) + lam_init
    post = 1.0 - lam_init
    o_lat = linear(da_attend(q, k, v, lam, subln_g, post, n_ctx, n_lat, n_ctx + n_lat), w_out)
    o_ctx = linear(da_attend(q, k, v, lam, subln_g, post, 0, n_ctx, n_ctx), w_out) if need_ctx else None
    return o_lat, o_ctx


def s5_discretize(lam_re, lam_im, log_dt, b_re, b_im):
    lam_re, lam_im = lam_re.astype(F32), lam_im.astype(F32)
    b_re, b_im = b_re.astype(F32), b_im.astype(F32)
    dt = jnp.exp(log_dt.astype(F32))[:, None]
    mag = jnp.exp(lam_re * dt)
    abar_re, abar_im = mag * jnp.cos(lam_im * dt), mag * jnp.sin(lam_im * dt)
    nr, ni = abar_re - 1.0, abar_im
    den = lam_re * lam_re + lam_im * lam_im
    k_re = (nr * lam_re + ni * lam_im) / den
    k_im = (ni * lam_re - nr * lam_im) / den
    bb_re = k_re[..., None] * b_re - k_im[..., None] * b_im
    bb_im = k_re[..., None] * b_im + k_im[..., None] * b_re
    return abar_re, abar_im, bb_re, bb_im


S5_BATCH_TILE = 8
S5_SCAN_TOKENS = 128
S5_PANEL_GROUPS = 16
S5_PANELS = S5_GROUPS // S5_PANEL_GROUPS


def _s5_panels(bb, c):
    g, p, m = bb.shape
    pg = S5_PANEL_GROUPS
    eye = jnp.eye(pg, dtype=bb.dtype)
    w_in = jnp.einsum('qgpm,gh->qgmhp', bb.reshape(g // pg, pg, p, m), eye).reshape(g // pg, pg * m, pg * p)
    w_out = jnp.einsum('qgmp,gh->qgphm', c.reshape(g // pg, pg, m, p), eye).reshape(g // pg, pg * p, pg * m)
    return w_in.astype(BF16), w_out.astype(BF16)


def _s5_scan_kernel(u_ref, wb_re_ref, wb_im_ref, a_re_ref, a_im_ref, c_re_ref, c_im_ref, y_ref,
                    x_re, x_im, state, *, reverse):
    rows = u_ref.shape[1]
    tokens = rows // S5_BATCH_TILE
    cin = S5_PANEL_GROUPS * S5_GROUP

    @pl.when(pl.program_id(1) == 0)
    def _():
        state[...] = jnp.zeros_like(state)

    for p in range(S5_PANELS):
        ub = u_ref[0, :, p * cin:(p + 1) * cin].astype(BF16)
        x_re[...] = jnp.dot(ub, wb_re_ref[p], preferred_element_type=F32)
        x_im[...] = jnp.dot(ub, wb_im_ref[p], preferred_element_type=F32)
        a_re, a_im = a_re_ref[p], a_im_ref[p]

        def step(i, carry):
            xr, xi = carry
            t = (tokens - 1 - i) if reverse else i
            r0 = pl.multiple_of(t * S5_BATCH_TILE, S5_BATCH_TILE)
            nr = a_re * xr - a_im * xi + x_re[pl.ds(r0, S5_BATCH_TILE), :]
            ni = a_re * xi + a_im * xr + x_im[pl.ds(r0, S5_BATCH_TILE), :]
            x_re[pl.ds(r0, S5_BATCH_TILE), :] = nr
            x_im[pl.ds(r0, S5_BATCH_TILE), :] = ni
            return nr, ni

        xr, xi = lax.fori_loop(0, tokens, step, (state[p, 0], state[p, 1]), unroll=2)
        state[p, 0] = xr
        state[p, 1] = xi
        y_ref[0, :, p * cin:(p + 1) * cin] = (
            jnp.dot(x_re[...].astype(BF16), c_re_ref[p], preferred_element_type=F32)
            - jnp.dot(x_im[...].astype(BF16), c_im_ref[p], preferred_element_type=F32))


def s5_scan(u_g, lam_re, lam_im, log_dt, b_re, b_im, c_re, c_im, n_ctx, reverse):
    nbg, rows_total, d = u_g.shape
    abar_re, abar_im, bb_re, bb_im = s5_discretize(lam_re, lam_im, log_dt, b_re, b_im)
    wb_re, cp_re = _s5_panels(bb_re, c_re.astype(F32))
    wb_im, cp_im = _s5_panels(bb_im, c_im.astype(F32))
    states = S5_PANEL_GROUPS * S5_STATE
    tile = lambda a: jnp.broadcast_to(a.reshape(S5_PANELS, 1, states), (S5_PANELS, S5_BATCH_TILE, states))
    rows = S5_SCAN_TOKENS * S5_BATCH_TILE
    n_chunks = rows_total // rows
    ctx_chunks = n_ctx // S5_SCAN_TOKENS
    if reverse:
        chunk = lambda s: jnp.where(s < ctx_chunks, ctx_chunks - 1 - s, n_chunks - 1 - (s - ctx_chunks))
    else:
        chunk = lambda s: s
    full = lambda a: pl.BlockSpec(a.shape, lambda i, s: (0,) * a.ndim)
    a_re_t, a_im_t = tile(abar_re), tile(abar_im)
    return pl.pallas_call(
        functools.partial(_s5_scan_kernel, reverse=reverse),
        grid=(nbg, n_chunks),
        in_specs=[pl.BlockSpec((1, rows, d), lambda i, s: (i, chunk(s), 0)),
                  full(wb_re), full(wb_im), full(a_re_t), full(a_im_t), full(cp_re), full(cp_im)],
        out_specs=pl.BlockSpec((1, rows, d), lambda i, s: (i, chunk(s), 0)),
        out_shape=jax.ShapeDtypeStruct(u_g.shape, F32),
        scratch_shapes=[pltpu.VMEM((rows, states), F32), pltpu.VMEM((rows, states), F32),
                        pltpu.VMEM((S5_PANELS, 2, S5_BATCH_TILE, states), F32)],
        compiler_params=pltpu.CompilerParams(dimension_semantics=("arbitrary", "arbitrary"),
                                             vmem_limit_bytes=MOSAIC_VMEM_LIMIT),
        name="s5_scan_bwd" if reverse else "s5_scan_fwd",
    )(u_g, wb_re, wb_im, a_re_t, a_im_t, cp_re, cp_im)


def _s5_glu_kernel(yf_ref, yb_ref, u_ref, d_ref, w_ref, o_ref):
    d = o_ref.shape[-1]
    y = yf_ref[0] + yb_ref[0] + d_ref[...] * u_ref[0]
    r = jnp.dot(_gelu_tanh(y).astype(BF16), w_ref[...], preferred_element_type=F32)
    o_ref[0] = r[:, :d] * jax.nn.sigmoid(r[:, d:])


def s5_glu(y_fw, y_bw, u_g, d_skip, w_glu, block_rows=512):
    nbg, rows_total, d = u_g.shape
    blk = pl.BlockSpec((1, block_rows, d), lambda i, j: (i, j, 0))
    return pl.pallas_call(
        _s5_glu_kernel,
        grid=(nbg, rows_total // block_rows),
        in_specs=[blk, blk, blk,
                  pl.BlockSpec((1, d), lambda i, j: (0, 0)),
                  pl.BlockSpec(w_glu.shape, lambda i, j: (0, 0))],
        out_specs=blk,
        out_shape=jax.ShapeDtypeStruct(u_g.shape, F32),
        compiler_params=pltpu.CompilerParams(vmem_limit_bytes=MOSAIC_VMEM_LIMIT),
        name="s5_glu",
    )(y_fw, y_bw, u_g, d_skip.reshape(1, d), w_glu.astype(BF16))


def s5_mixer(u_ctx, u_lat, lam_re, lam_im, log_dt, b_re, b_im, c_re, c_im, d_skip, w_glu, need_ctx):
    b_, n_ctx, d = u_ctx.shape
    n_all = n_ctx + u_lat.shape[1]
    bt = S5_BATCH_TILE
    u_all = jnp.concatenate([u_ctx, u_lat], axis=1)
    u_g = u_all.reshape(b_ // bt, bt, n_all, d).transpose(0, 2, 1, 3).reshape(b_ // bt, n_all * bt, d)
    ys = [s5_scan(u_g, lam_re[dr], lam_im[dr], log_dt[dr], b_re[dr], b_im[dr], c_re[dr], c_im[dr],
                  n_ctx, reverse=bool(dr)) for dr in range(2)]
    o_g = s5_glu(ys[0], ys[1], u_g, d_skip, w_glu)
    o = o_g.reshape(b_ // bt, n_all, bt, d).transpose(0, 2, 1, 3).reshape(b_, n_all, d)
    return o[:, n_ctx:], (o[:, :n_ctx] if need_ctx else None)


HG_SCAN_TOKENS = 256
HG_SUB = 32


def _hgrn_scan_kernel(q_ref, f_ref, v_ref, lb_ref, tri_ref, o_ref, q_s, k_s, cum_s, v_s, st, *, reverse):
    tokens = q_ref.shape[1]

    @pl.when(pl.program_id(2) == 0)
    def _():
        st[...] = jnp.zeros_like(st)

    f = f_ref[0]
    log_f = jnp.logaddexp(lb_ref[0:1, :], lb_ref[1:2, :] + jax.nn.log_sigmoid(f))
    k_s[...] = lb_ref[2:3, :] * jax.nn.sigmoid(-f)
    q_s[...] = jax.nn.silu(q_ref[0])
    v_s[...] = v_ref[0]
    cum_s[...] = jnp.dot(tri_ref[...], log_f, precision=lax.Precision.HIGHEST, preferred_element_type=F32)
    tpos = lax.broadcasted_iota(jnp.int32, (HG_SUB, q_ref.shape[2]), 0)
    nsub = tokens // HG_SUB
    ones = jnp.ones((q_ref.shape[2], v_ref.shape[2]), BF16)

    def block(i, carry):
        c = (nsub - 1 - i) if reverse else i
        base = pl.multiple_of(c * HG_SUB, HG_SUB)
        qc = q_s[pl.ds(base, HG_SUB), :]
        cumc = cum_s[pl.ds(base, HG_SUB), :]

        def column(s, o):
            ks = k_s[pl.ds(base + s, 1), :]
            cs = cum_s[pl.ds(base + s, 1), :]
            vs = v_s[pl.ds(base + s, 1), :]
            seen = (tpos <= s) if reverse else (tpos >= s)
            decay = jnp.exp(jnp.where(seen, cumc - cs, -jnp.inf))
            att = jnp.dot((qc * ks * decay).astype(BF16), ones, preferred_element_type=F32)
            return o + att * vs

        o = lax.fori_loop(0, HG_SUB, column, jnp.zeros(qc.shape, F32), unroll=True)
        s_t = st[...]
        o = o + lax.dot_general((qc * jnp.exp(cumc)).astype(BF16), s_t.astype(BF16),
                                (((1,), (1,)), ((), ())), preferred_element_type=F32)
        last = cum_s[pl.ds(base + (0 if reverse else HG_SUB - 1), 1), :]
        kh = (k_s[pl.ds(base, HG_SUB), :] * jnp.exp(last - cumc)).astype(BF16)
        vc = v_s[pl.ds(base, HG_SUB), :].astype(BF16)
        st[...] = s_t * jnp.exp(last) + lax.dot_general(vc, kh, (((0,), (0,)), ((), ())),
                                                        preferred_element_type=F32)
        o_ref[0, pl.ds(base, HG_SUB), :] = o
        return carry

    lax.fori_loop(0, nsub, block, 0)


def hgrn_scan(proj, lb_rows, n_ctx, f_col, reverse):
    b_, n, _ = proj.shape
    t = HG_SCAN_TOKENS
    hk = HG_KEY
    n_chunks, ctx_chunks = n // t, n_ctx // t
    if reverse:
        chunk = lambda s: jnp.where(s < ctx_chunks, ctx_chunks - 1 - s, n_chunks - 1 - (s - ctx_chunks))
    else:
        chunk = lambda s: s
    pos = jnp.arange(t)
    same = (pos[:, None] // HG_SUB) == (pos[None, :] // HG_SUB)
    order = (pos[None, :] >= pos[:, None]) if reverse else (pos[None, :] <= pos[:, None])
    tri = (same & order).astype(F32)
    col = lambda off: pl.BlockSpec((1, t, hk), lambda i, h, s: (i, chunk(s), off * HG_HEADS + h))
    return pl.pallas_call(
        functools.partial(_hgrn_scan_kernel, reverse=reverse),
        grid=(b_, HG_HEADS, n_chunks),
        in_specs=[col(0), col(f_col), col(3),
                  pl.BlockSpec((3, hk), lambda i, h, s: (0, h)),
                  pl.BlockSpec((t, t), lambda i, h, s: (0, 0))],
        out_specs=pl.BlockSpec((1, t, hk), lambda i, h, s: (i, chunk(s), h)),
        out_shape=jax.ShapeDtypeStruct((b_, n, HG_HEADS * HG_VAL), F32),
        scratch_shapes=[pltpu.VMEM((t, hk), F32)] * 4 + [pltpu.VMEM((HG_VAL, hk), F32)],
        compiler_params=pltpu.CompilerParams(dimension_semantics=("arbitrary", "arbitrary", "arbitrary"),
                                             vmem_limit_bytes=MOSAIC_VMEM_LIMIT),
        name="hgrn_scan_bwd" if reverse else "hgrn_scan_fwd",
    )(proj, proj, proj, lb_rows, tri)


def _hgrn_out_kernel(of_ref, ob_ref, gate_ref, g_ref, w_ref, o_ref):
    o = of_ref[0] + ob_ref[0]
    parts = []
    for h in range(HG_HEADS):
        oh = o[:, h * HG_VAL:(h + 1) * HG_VAL]
        parts.append(oh * lax.rsqrt(jnp.mean(oh * oh, axis=-1, keepdims=True) + RMS_EPS))
    y = jnp.concatenate(parts, axis=1) * g_ref[...] * jax.nn.silu(gate_ref[0])
    o_ref[0] = jnp.dot(y.astype(BF16), w_ref[...], preferred_element_type=F32)


def hgrn_out(o_fw, o_bw, proj, norm_g, w_out, block_rows=256):
    b_, n, d = o_fw.shape
    assert n % block_rows == 0
    row = pl.BlockSpec((1, block_rows, d), lambda i, j: (i, j, 0))
    return pl.pallas_call(
        _hgrn_out_kernel,
        grid=(b_, n // block_rows),
        in_specs=[row, row, pl.BlockSpec((1, block_rows, d), lambda i, j: (i, j, 4)),
                  pl.BlockSpec((1, d), lambda i, j: (0, 0)), pl.BlockSpec(w_out.shape, lambda i, j: (0, 0))],
        out_specs=row,
        out_shape=jax.ShapeDtypeStruct((b_, n, d), F32),
        compiler_params=pltpu.CompilerParams(vmem_limit_bytes=MOSAIC_VMEM_LIMIT),
        name="hgrn_out",
    )(o_fw, o_bw, proj, norm_g.reshape(1, d), w_out.astype(BF16))


def hgrn2_mixer(u_ctx, u_lat, w_in, w_out, norm_g, lb, need_ctx):
    n_ctx = u_ctx.shape[1]
    proj = linear(jnp.concatenate([u_ctx, u_lat], axis=1), w_in)
    lb_rows = jnp.stack([jnp.log(lb), jnp.log1p(-lb), 1.0 - lb])
    o_fw = hgrn_scan(proj, lb_rows, n_ctx, 1, reverse=False)
    o_bw = hgrn_scan(proj, lb_rows, n_ctx, 2, reverse=True)
    o = hgrn_out(o_fw, o_bw, proj, norm_g, w_out)
    return o[:, n_ctx:], (o[:, :n_ctx] if need_ctx else None)


PEER_SLOTS = PEER_HEADS * PEER_TOPK
PEER_SEL_TOKENS = 256
PEER_MIX_TOKENS = 128
PEER_ROW_BUFFERS = 4


def _topk_axis0(cur, k, payload=None):
    rows = cur.shape[0]
    iota = lax.broadcasted_iota(jnp.int32, cur.shape, 0)
    vals, picks = [], []
    for _ in range(k):
        m = jnp.max(cur, axis=0, keepdims=True)
        pos = jnp.min(jnp.where(cur == m, iota, rows), axis=0, keepdims=True)
        hit = iota == pos
        vals.append(m)
        if payload is None:
            picks.append(pos)
        else:
            picks.append(jnp.sum(jnp.where(hit, payload, 0), axis=0, keepdims=True))
        cur = jnp.where(hit, -jnp.inf, cur)
    return jnp.concatenate(vals, axis=0), jnp.concatenate(picks, axis=0)


def _peer_select_kernel(h_ref, sc_ref, sh_ref, wq_ref, keys_ref, idx_ref, g_ref):
    half = PEER_QDIM // 2
    x = h_ref[0] * (1.0 + sc_ref[0]) + sh_ref[0]
    q = jnp.dot(x.astype(BF16), wq_ref[...], preferred_element_type=F32)
    tokens = x.shape[0]
    for c0 in range(0, tokens, LANES):
        idx_rows, g_rows = [], []
        for hd in range(PEER_HEADS):
            tops = []
            for c in range(2):
                lo = (hd * 2 + c) * half
                qhc = q[c0:c0 + LANES, lo:lo + half].astype(BF16)
                s_t = lax.dot_general(keys_ref[c], qhc, (((1,), (1,)), ((), ())),
                                      preferred_element_type=F32)
                tops.append(_topk_axis0(s_t, PEER_TOPK))
            (s1, i1), (s2, i2) = tops
            width = [PEER_TOPK // (a + 1) for a in range(PEER_TOPK)]
            pad = -sum(width) % 8
            cand_s = jnp.concatenate([s1[a:a + 1] + s2[:width[a]] for a in range(PEER_TOPK)]
                                     + [jnp.full((pad, LANES), -jnp.inf, F32)], axis=0)
            cand_i = jnp.concatenate([i1[a:a + 1] * PEER_NKEYS + i2[:width[a]] for a in range(PEER_TOPK)]
                                     + [jnp.zeros((pad, LANES), jnp.int32)], axis=0)
            top_s, top_i = _topk_axis0(cand_s, PEER_TOPK, payload=cand_i)
            e = jnp.exp(top_s - top_s[0:1])
            g_rows.append(e / jnp.sum(e, axis=0, keepdims=True))
            idx_rows.append(top_i)
        idx_ref[0, c0:c0 + LANES, :] = jnp.concatenate(idx_rows, axis=0).T
        g_ref[0, c0:c0 + LANES, :] = jnp.concatenate(g_rows, axis=0).T


def peer_select(h, sc, sh, w_q, sub_keys):
    b_, n, d = h.shape
    tb = min(PEER_SEL_TOKENS, n)
    mod_map = (lambda i, j: (i, 0, 0)) if sc.shape[0] == b_ else (lambda i, j: (0, 0, 0))
    return pl.pallas_call(
        _peer_select_kernel,
        grid=(b_, n // tb),
        in_specs=[pl.BlockSpec((1, tb, d), lambda i, j: (i, j, 0)),
                  pl.BlockSpec((1, 1, d), mod_map),
                  pl.BlockSpec((1, 1, d), mod_map),
                  pl.BlockSpec(w_q.shape, lambda i, j: (0, 0)),
                  pl.BlockSpec(sub_keys.shape, lambda i, j: (0, 0, 0))],
        out_specs=[pl.BlockSpec((1, tb, PEER_SLOTS), lambda i, j: (i, j, 0)),
                   pl.BlockSpec((1, tb, PEER_SLOTS), lambda i, j: (i, j, 0))],
        out_shape=[jax.ShapeDtypeStruct((b_, n, PEER_SLOTS), jnp.int32),
                   jax.ShapeDtypeStruct((b_, n, PEER_SLOTS), F32)],
        compiler_params=pltpu.CompilerParams(vmem_limit_bytes=48 * 1024 * 1024),
        name="peer_select",
    )(h, sc, sh, w_q.astype(BF16), sub_keys.astype(BF16))


def peer_table(u_tab, v_tab):
    bits = lambda t: lax.bitcast_convert_type(t.astype(BF16), jnp.uint16).astype(jnp.uint32)
    return ((bits(u_tab) << 16) | bits(v_tab))[:, None, :]


def _peer_mix_kernel(idx_hbm, h_ref, sc_ref, sh_ref, g_ref, tab_hbm, out_ref,
                     idx_smem, x_scr, rows, row_sem, idx_sem):
    tokens, d = x_scr.shape
    nbuf = PEER_ROW_BUFFERS
    ahead = nbuf - 1
    blk = pl.program_id(0) * pl.num_programs(1) + pl.program_id(1)
    per_blk = tokens * PEER_SLOTS
    idx_copy = pltpu.make_async_copy(idx_hbm.at[pl.ds(pl.multiple_of(blk * per_blk, per_blk), per_blk)],
                                     idx_smem, idx_sem)
    idx_copy.start()
    x_scr[...] = h_ref[0] * (1.0 + sc_ref[0]) + sh_ref[0]
    idx_copy.wait()

    def fetch(t, slot):
        for j in range(PEER_SLOTS):
            pltpu.make_async_copy(tab_hbm.at[idx_smem[t * PEER_SLOTS + j]],
                                  rows.at[slot, pl.ds(j, 1), :], row_sem.at[slot]).start()

    def wait_rows(slot):
        pltpu.make_async_copy(rows.at[slot], rows.at[slot], row_sem.at[slot]).wait()

    eye = (lax.broadcasted_iota(jnp.int32, (PEER_SLOTS, PEER_SLOTS), 0)
           == lax.broadcasted_iota(jnp.int32, (PEER_SLOTS, PEER_SLOTS), 1))

    def combine(t, slot):
        x_row = x_scr[pl.ds(t, 1), :]
        words = rows[slot]
        u_rows = lax.bitcast_convert_type(words & jnp.uint32(0xFFFF0000), F32)
        v_rows = lax.bitcast_convert_type(words << 16, F32)
        act = jnp.sum(u_rows * x_row, axis=1, keepdims=True)
        g_col = jnp.sum(jnp.where(eye, g_ref[0, pl.ds(t, 1), :], 0.0), axis=1, keepdims=True)
        w = g_col * _gelu_tanh(act)
        out_ref[0, pl.ds(t, 1), :] = jnp.sum(w * v_rows, axis=0, keepdims=True)

    def token(t, slot, prefetch):
        if prefetch:
            fetch(t + ahead, (slot + ahead) % nbuf)
        wait_rows(slot)
        combine(t, slot)

    for t0 in range(ahead):
        fetch(t0, t0)

    def group(i, carry):
        for slot in range(nbuf):
            token(i * nbuf + slot, slot, True)
        return carry

    lax.fori_loop(0, tokens // nbuf - 1, group, 0)
    for slot in range(nbuf):
        token(tokens - nbuf + slot, slot, slot + ahead < nbuf)


def peer_mix(h, sc, sh, idx, g, uv_tab):
    b_, n, d = h.shape
    tb = min(PEER_MIX_TOKENS, n)
    assert n % tb == 0 and tb % PEER_ROW_BUFFERS == 0
    mod_map = (lambda i, j: (i, 0, 0)) if sc.shape[0] == b_ else (lambda i, j: (0, 0, 0))
    return pl.pallas_call(
        _peer_mix_kernel,
        grid=(b_, n // tb),
        in_specs=[pl.BlockSpec(memory_space=pl.ANY),
                  pl.BlockSpec((1, tb, d), lambda i, j: (i, j, 0)),
                  pl.BlockSpec((1, 1, d), mod_map),
                  pl.BlockSpec((1, 1, d), mod_map),
                  pl.BlockSpec((1, tb, PEER_SLOTS), lambda i, j: (i, j, 0)),
                  pl.BlockSpec(memory_space=pl.ANY)],
        out_specs=pl.BlockSpec((1, tb, d), lambda i, j: (i, j, 0)),
        out_shape=jax.ShapeDtypeStruct(h.shape, F32),
        scratch_shapes=[pltpu.SMEM((tb * PEER_SLOTS,), jnp.int32),
                        pltpu.VMEM((tb, d), F32),
                        pltpu.VMEM((PEER_ROW_BUFFERS, PEER_SLOTS, d), jnp.uint32),
                        pltpu.SemaphoreType.DMA((PEER_ROW_BUFFERS,)),
                        pltpu.SemaphoreType.DMA(())],
        compiler_params=pltpu.CompilerParams(vmem_limit_bytes=32 * 1024 * 1024),
        name="peer_mix",
    )(idx.reshape(-1), h, sc, sh, g, uv_tab)


def peer_ffn(h, sc, sh, w_q, sub_keys, uv_tab):
    idx, g = peer_select(h, sc, sh, w_q, sub_keys)
    return peer_mix(h, sc, sh, idx, g, uv_tab)


def kernel(x, c, ctx, c_ctx, ada_w, ada_b, ln_g, ln_b, da_w_in, da_w_out, da_lam_q, da_lam_k, da_subln,
           s5_lam_re, s5_lam_im, s5_log_dt, s5_b_re, s5_b_im, s5_c_re, s5_c_im, s5_d, s5_w_glu,
           hg_w_in, hg_w_out, hg_norm, hg_lb, peer_wq, peer_keys, peer_u, peer_v):
    L = x.shape[1]
    cos, sin = axial_rope(L, DA_HEAD_DIM)
    s_c = jax.nn.silu(c)
    s_ctx = jax.nn.silu(c_ctx)
    lb_soft = jax.nn.softmax(hg_lb.astype(F32), axis=0)
    lb_all = jnp.cumsum(lb_soft, axis=0) - lb_soft[0]
    h, hc = x, ctx
    for i in range(DEPTH):
        kind, slot = LAYER_TYPES[i], i // N_MIXERS
        need_ctx = i < DEPTH - 1
        mod = (s_c @ ada_w[i] + ada_b[i])[:, None, :]
        mod_c = s_ctx @ ada_w[i] + ada_b[i]
        sh1, sc1, g1, sh2, sc2, g2 = jnp.split(mod, 6, axis=-1)
        csh1, csc1, cg1, csh2, csc2, cg2 = jnp.split(mod_c, 6, axis=-1)
        u = h * (1.0 + sc1) + sh1
        uc = hc * (1.0 + csc1) + csh1
        if kind == 0:
            lam_init = 0.8 - 0.6 * math.exp(-0.3 * i)
            o, oc = diff_attention(uc, u, da_w_in[slot], da_w_out[slot], da_lam_q[slot], da_lam_k[slot],
                                   da_subln[slot], lam_init, cos, sin, need_ctx)
        elif kind == 1:
            o, oc = s5_mixer(uc, u, s5_lam_re[slot], s5_lam_im[slot], s5_log_dt[slot], s5_b_re[slot],
                             s5_b_im[slot], s5_c_re[slot], s5_c_im[slot], s5_d[slot], s5_w_glu[slot], need_ctx)
        else:
            o, oc = hgrn2_mixer(uc, u, hg_w_in[slot], hg_w_out[slot], hg_norm[slot], lb_all[i], need_ctx)
        h = residual_layer_norm(h, o, g1, ln_g[i, 0], ln_b[i, 0])
        uv_tab = peer_table(peer_u[i], peer_v[i])
        f = peer_ffn(h, sc2, sh2, peer_wq[i], peer_keys[i], uv_tab)
        h = residual_layer_norm(h, f, g2, ln_g[i, 1], ln_b[i, 1])
        if need_ctx:
            hc = residual_layer_norm(hc, oc, cg1.reshape(1, 1, -1), ln_g[i, 0], ln_b[i, 0])
            fc = peer_ffn(hc, csc2.reshape(1, 1, -1), csh2.reshape(1, 1, -1), peer_wq[i], peer_keys[i], uv_tab)
            hc = residual_layer_norm(hc, fc, cg2.reshape(1, 1, -1), ln_g[i, 1], ln_b[i, 1])
    return h
```

```python
import functools
import math
import jax, jax.numpy as jnp
from jax import lax
import numpy as np
from jax.experimental import pallas as pl
from jax.experimental.pallas import tpu as pltpu

D_MODEL = 1024
BATCH = 32
SEQ = 2048
DEPTH = 4

CTX_LEN = 256
GRID_W = 64
N_MIXERS = 3
LAYER_TYPES = tuple(i % N_MIXERS for i in range(DEPTH))
N_ATTN = LAYER_TYPES.count(0)
N_S5 = LAYER_TYPES.count(1)
N_HG = LAYER_TYPES.count(2)

DA_HEADS = 8
DA_HEAD_DIM = 64
DA_V_DIM = 2 * DA_HEAD_DIM
Q_BLOCK = 128
ROPE_THETA = 10000.0
S5_GROUP = 16
S5_GROUPS = D_MODEL // S5_GROUP
S5_STATE = 64
HG_HEADS = 8
HG_KEY = D_MODEL // HG_HEADS
HG_VAL = D_MODEL // HG_HEADS
HG_CHUNK = 32
PEER_HEADS = 8
PEER_NKEYS = 128
PEER_EXPERTS = PEER_NKEYS * PEER_NKEYS
PEER_QDIM = 256
PEER_TOPK = 16
PEER_BLOCK = 128
LN_EPS = 1e-5
RMS_EPS = 1e-6
DN_ALPHA = (2 * DEPTH) ** 0.25
DN_BETA = (8 * DEPTH) ** -0.25

F32 = jnp.float32
BF16 = jnp.bfloat16
LANES = 128
MOSAIC_VMEM_LIMIT = 48 * 1024 * 1024


def _gelu_tanh(x):
    return 0.5 * x * (1.0 + jnp.tanh(math.sqrt(2.0 / math.pi) * (x + 0.044715 * (x * x * x))))


def _res_ln_kernel(h_ref, o_ref, gate_ref, g_ref, b_ref, out_ref):
    y = DN_ALPHA * h_ref[0] + gate_ref[0] * o_ref[0]
    mu = jnp.mean(y, -1, keepdims=True)
    yc = y - mu
    var = jnp.mean(yc * yc, -1, keepdims=True)
    out_ref[0] = yc * lax.rsqrt(var + LN_EPS) * g_ref[...] + b_ref[...]


def residual_layer_norm(h, o, gate, g, b, block_n=512):
    b_, n, d = h.shape
    bn = min(block_n, n)
    per_batch_gate = gate.shape[0] == b_
    gate_map = (lambda i, j: (i, 0, 0)) if per_batch_gate else (lambda i, j: (0, 0, 0))
    return pl.pallas_call(
        _res_ln_kernel,
        grid=(b_, n // bn),
        in_specs=[pl.BlockSpec((1, bn, d), lambda i, j: (i, j, 0)),
                  pl.BlockSpec((1, bn, d), lambda i, j: (i, j, 0)),
                  pl.BlockSpec((1, 1, d), gate_map),
                  pl.BlockSpec((1, d), lambda i, j: (0, 0)),
                  pl.BlockSpec((1, d), lambda i, j: (0, 0))],
        out_specs=pl.BlockSpec((1, bn, d), lambda i, j: (i, j, 0)),
        out_shape=jax.ShapeDtypeStruct(h.shape, h.dtype),
        name="residual_layer_norm",
    )(h, o, gate, g.reshape(1, d), b.reshape(1, d))


def axial_rope(length, dim):
    rows = length // GRID_W
    row = jnp.repeat(jnp.arange(rows, dtype=F32), GRID_W)
    col = jnp.tile(jnp.arange(GRID_W, dtype=F32), rows)
    n_freq = dim // 4
    inv = ROPE_THETA ** (-jnp.arange(n_freq, dtype=F32) / n_freq)
    ang = jnp.concatenate([row[:, None] * inv, col[:, None] * inv], axis=-1)
    return jnp.cos(ang), jnp.sin(ang)


def _linear_kernel(x_ref, w_ref, o_ref):
    o_ref[0] = jnp.dot(x_ref[0].astype(BF16), w_ref[...], preferred_element_type=F32).astype(o_ref.dtype)


def linear(x, w, out_dtype=F32, block_rows=256, block_cols=1024):
    b_, n, kdim = x.shape
    ncols = w.shape[1]
    br, bc = min(block_rows, n), min(block_cols, ncols)
    assert n % br == 0 and ncols % bc == 0
    return pl.pallas_call(
        _linear_kernel,
        grid=(b_, n // br, ncols // bc),
        in_specs=[pl.BlockSpec((1, br, kdim), lambda i, j, c: (i, j, 0)),
                  pl.BlockSpec((kdim, bc), lambda i, j, c: (0, c))],
        out_specs=pl.BlockSpec((1, br, bc), lambda i, j, c: (i, j, c)),
        out_shape=jax.ShapeDtypeStruct((b_, n, ncols), out_dtype),
        compiler_params=pltpu.CompilerParams(vmem_limit_bytes=MOSAIC_VMEM_LIMIT),
        name="linear",
    )(x, w.astype(BF16))


DA_Q_TOKENS = 256
DA_PROJ_TOKENS = 256


def _da_qkv_kernel(x_ref, w_ref, wsw_ref, cos_ref, sin_ref, q_ref, k_ref, v_ref):
    d = q_ref.shape[-1]
    x = x_ref[0].astype(BF16)
    reps = 2 * d // cos_ref.shape[-1]
    c = jnp.concatenate([cos_ref[...]] * reps, axis=1)
    s = jnp.concatenate([sin_ref[...]] * reps, axis=1)
    qk = (jnp.dot(x, w_ref[:, :2 * d], preferred_element_type=F32) * c
          + jnp.dot(x, wsw_ref[...], preferred_element_type=F32) * s)
    q_ref[0] = (qk[:, :d] * DA_HEAD_DIM ** -0.5).astype(BF16)
    k_ref[0] = qk[:, d:].astype(BF16)
    v_ref[0] = jnp.dot(x, w_ref[:, 2 * d:], preferred_element_type=F32).astype(BF16)


def da_qkv(u_all, w_in, cos_t, sin_t):
    b_, n, d = u_all.shape
    tb = DA_PROJ_TOKENS
    swap = jnp.arange(2 * d) ^ 1
    w = w_in.astype(BF16)
    w_sw = w[:, :2 * d][:, swap]
    row = pl.BlockSpec((1, tb, d), lambda i, j: (i, j, 0))
    tab = pl.BlockSpec((tb, cos_t.shape[1]), lambda i, j: (j, 0))
    return pl.pallas_call(
        _da_qkv_kernel,
        grid=(b_, n // tb),
        in_specs=[row, pl.BlockSpec(w.shape, lambda i, j: (0, 0)), pl.BlockSpec(w_sw.shape, lambda i, j: (0, 0)),
                  tab, tab],
        out_specs=[row, row, row],
        out_shape=[jax.ShapeDtypeStruct((b_, n, d), BF16)] * 3,
        compiler_params=pltpu.CompilerParams(vmem_limit_bytes=MOSAIC_VMEM_LIMIT),
        name="da_qkv",
    )(u_all, w, w_sw, cos_t, sin_t)


def _da_attn_kernel(lam_ref, q_ref, k_ref, v_ref, g_ref, o_ref, *, post_scale):
    q, k, v = q_ref[0], k_ref[0], v_ref[0]
    lane = lax.broadcasted_iota(jnp.int32, q.shape, 1)
    zero = jnp.zeros_like(q)
    contract_last = (((1,), (1,)), ((), ()))

    def softmax_parts(qm):
        s = lax.dot_general(qm, k, contract_last, preferred_element_type=F32)
        e = jnp.exp(s - jnp.max(s, axis=-1, keepdims=True))
        return e, jnp.sum(e, axis=-1, keepdims=True)

    e0, z0 = softmax_parts(jnp.where(lane < DA_HEAD_DIM, q, zero))
    e1, z1 = softmax_parts(jnp.where(lane >= DA_HEAD_DIM, q, zero))
    a = e0 / z0 - lam_ref[0] * (e1 / z1)
    o = jnp.dot(a.astype(BF16), v, preferred_element_type=F32)
    o = o * lax.rsqrt(jnp.mean(o * o, axis=-1, keepdims=True) + RMS_EPS)
    o_ref[0] = (o * g_ref[...] * post_scale).astype(o_ref.dtype)


def da_attend(q, k, v, lam, subln_g, post_scale, q_start, n_q, n_k):
    b_, _, d = q.shape
    hd = d // DA_HEADS
    tq = DA_Q_TOKENS
    q0 = q_start // tq
    return pl.pallas_call(
        functools.partial(_da_attn_kernel, post_scale=post_scale),
        grid=(b_, DA_HEADS, n_q // tq),
        in_specs=[pl.BlockSpec(memory_space=pltpu.SMEM),
                  pl.BlockSpec((1, tq, hd), lambda i, h, j: (i, q0 + j, h)),
                  pl.BlockSpec((1, n_k, hd), lambda i, h, j: (i, 0, h)),
                  pl.BlockSpec((1, n_k, hd), lambda i, h, j: (i, 0, h)),
                  pl.BlockSpec((1, hd), lambda i, h, j: (0, 0))],
        out_specs=pl.BlockSpec((1, tq, hd), lambda i, h, j: (i, j, h)),
        out_shape=jax.ShapeDtypeStruct((b_, n_q, d), BF16),
        compiler_params=pltpu.CompilerParams(vmem_limit_bytes=MOSAIC_VMEM_LIMIT),
        name="da_attend",
    )(lam.reshape(1), q, k, v, subln_g.reshape(1, hd))


def diff_attention(u_ctx, u_lat, w_in, w_out, lam_q, lam_k, subln_g, lam_init, cos, sin, need_ctx):
    b_, n_ctx, d = u_ctx.shape
    n_lat = u_lat.shape[1]
    lanes_cos = jnp.tile(jnp.repeat(cos, 2, axis=1), (1, 2))
    lanes_sin = jnp.tile(jnp.stack([-sin, sin], axis=-1).reshape(n_lat, -1), (1, 2))
    cos_t = jnp.concatenate([jnp.ones((n_ctx, lanes_cos.shape[1]), F32), lanes_cos], axis=0)
    sin_t = jnp.concatenate([jnp.zeros((n_ctx, lanes_sin.shape[1]), F32), lanes_sin], axis=0)
    u_all = jnp.concatenate([u_ctx, u_lat], axis=1)
    q, k, v = da_qkv(u_all, w_in, cos_t, sin_t)
    lq, lk = lam_q.astype(F32), lam_k.astype(F32)
    lam = jnp.exp(jnp.sum(lq[0] * lk[0])) - jnp.exp(jnp.sum(lq[1] * lk[1])) + lam_init
    post = 1.0 - lam_init
    o_lat = linear(da_attend(q, k, v, lam, subln_g, post, n_ctx, n_lat, n_ctx + n_lat), w_out)
    o_ctx = linear(da_attend(q, k, v, lam, subln_g, post, 0, n_ctx, n_ctx), w_out) if need_ctx else None
    return o_lat, o_ctx


def s5_discretize(lam_re, lam_im, log_dt, b_re, b_im):
    lam_re, lam_im = lam_re.astype(F32), lam_im.astype(F32)
    b_re, b_im = b_re.astype(F32), b_im.astype(F32)
    dt = jnp.exp(log_dt.astype(F32))[:, None]
    mag = jnp.exp(lam_re * dt)
    abar_re, abar_im = mag * jnp.cos(lam_im * dt), mag * jnp.sin(lam_im * dt)
    nr, ni = abar_re - 1.0, abar_im
    den = lam_re * lam_re + lam_im * lam_im
    k_re = (nr * lam_re + ni * lam_im) / den
    k_im = (ni * lam_re - nr * lam_im) / den
    bb_re = k_re[..., None] * b_re - k_im[..., None] * b_im
    bb_im = k_re[..., None] * b_im + k_im[..., None] * b_re
    return abar_re, abar_im, bb_re, bb_im


S5_BATCH_TILE = 8
S5_SCAN_TOKENS = 128
S5_PANEL_GROUPS = 16
S5_PANELS = S5_GROUPS // S5_PANEL_GROUPS


def _s5_panels(bb, c):
    g, p, m = bb.shape
    pg = S5_PANEL_GROUPS
    eye = jnp.eye(pg, dtype=bb.dtype)
    w_in = jnp.einsum('qgpm,gh->qgmhp', bb.reshape(g // pg, pg, p, m), eye).reshape(g // pg, pg * m, pg * p)
    w_out = jnp.einsum('qgmp,gh->qgphm', c.reshape(g // pg, pg, m, p), eye).reshape(g // pg, pg * p, pg * m)
    return w_in.astype(BF16), w_out.astype(BF16)


def _s5_scan_kernel(u_ref, wb_re_ref, wb_im_ref, a_re_ref, a_im_ref, c_re_ref, c_im_ref, y_ref,
                    x_re, x_im, state, *, reverse):
    rows = u_ref.shape[1]
    tokens = rows // S5_BATCH_TILE
    cin = S5_PANEL_GROUPS * S5_GROUP

    @pl.when(pl.program_id(1) == 0)
    def _():
        state[...] = jnp.zeros_like(state)

    for p in range(S5_PANELS):
        ub = u_ref[0, :, p * cin:(p + 1) * cin].astype(BF16)
        x_re[...] = jnp.dot(ub, wb_re_ref[p], preferred_element_type=F32)
        x_im[...] = jnp.dot(ub, wb_im_ref[p], preferred_element_type=F32)
        a_re, a_im = a_re_ref[p], a_im_ref[p]

        def step(i, carry):
            xr, xi = carry
            t = (tokens - 1 - i) if reverse else i
            r0 = pl.multiple_of(t * S5_BATCH_TILE, S5_BATCH_TILE)
            nr = a_re * xr - a_im * xi + x_re[pl.ds(r0, S5_BATCH_TILE), :]
            ni = a_re * xi + a_im * xr + x_im[pl.ds(r0, S5_BATCH_TILE), :]
            x_re[pl.ds(r0, S5_BATCH_TILE), :] = nr
            x_im[pl.ds(r0, S5_BATCH_TILE), :] = ni
            return nr, ni

        xr, xi = lax.fori_loop(0, tokens, step, (state[p, 0], state[p, 1]), unroll=2)
        state[p, 0] = xr
        state[p, 1] = xi
        y_ref[0, :, p * cin:(p + 1) * cin] = (
            jnp.dot(x_re[...].astype(BF16), c_re_ref[p], preferred_element_type=F32)
            - jnp.dot(x_im[...].astype(BF16), c_im_ref[p], preferred_element_type=F32))


def s5_scan(u_g, lam_re, lam_im, log_dt, b_re, b_im, c_re, c_im, n_ctx, reverse):
    nbg, rows_total, d = u_g.shape
    abar_re, abar_im, bb_re, bb_im = s5_discretize(lam_re, lam_im, log_dt, b_re, b_im)
    wb_re, cp_re = _s5_panels(bb_re, c_re.astype(F32))
    wb_im, cp_im = _s5_panels(bb_im, c_im.astype(F32))
    states = S5_PANEL_GROUPS * S5_STATE
    tile = lambda a: jnp.broadcast_to(a.reshape(S5_PANELS, 1, states), (S5_PANELS, S5_BATCH_TILE, states))
    rows = S5_SCAN_TOKENS * S5_BATCH_TILE
    n_chunks = rows_total // rows
    ctx_chunks = n_ctx // S5_SCAN_TOKENS
    if reverse:
        chunk = lambda s: jnp.where(s < ctx_chunks, ctx_chunks - 1 - s, n_chunks - 1 - (s - ctx_chunks))
    else:
        chunk = lambda s: s
    full = lambda a: pl.BlockSpec(a.shape, lambda i, s: (0,) * a.ndim)
    a_re_t, a_im_t = tile(abar_re), tile(abar_im)
    return pl.pallas_call(
        functools.partial(_s5_scan_kernel, reverse=reverse),
        grid=(nbg, n_chunks),
        in_specs=[pl.BlockSpec((1, rows, d), lambda i, s: (i, chunk(s), 0)),
                  full(wb_re), full(wb_im), full(a_re_t), full(a_im_t), full(cp_re), full(cp_im)],
        out_specs=pl.BlockSpec((1, rows, d), lambda i, s: (i, chunk(s), 0)),
        out_shape=jax.ShapeDtypeStruct(u_g.shape, F32),
        scratch_shapes=[pltpu.VMEM((rows, states), F32), pltpu.VMEM((rows, states), F32),
                        pltpu.VMEM((S5_PANELS, 2, S5_BATCH_TILE, states), F32)],
        compiler_params=pltpu.CompilerParams(dimension_semantics=("arbitrary", "arbitrary"),
                                             vmem_limit_bytes=MOSAIC_VMEM_LIMIT),
        name="s5_scan_bwd" if reverse else "s5_scan_fwd",
    )(u_g, wb_re, wb_im, a_re_t, a_im_t, cp_re, cp_im)


def _s5_glu_kernel(yf_ref, yb_ref, u_ref, d_ref, w_ref, o_ref):
    d = o_ref.shape[-1]
    y = yf_ref[0] + yb_ref[0] + d_ref[...] * u_ref[0]
    r = jnp.dot(_gelu_tanh(y).astype(BF16), w_ref[...], preferred_element_type=F32)
    o_ref[0] = r[:, :d] * jax.nn.sigmoid(r[:, d:])


def s5_glu(y_fw, y_bw, u_g, d_skip, w_glu, block_rows=512):
    nbg, rows_total, d = u_g.shape
    blk = pl.BlockSpec((1, block_rows, d), lambda i, j: (i, j, 0))
    return pl.pallas_call(
        _s5_glu_kernel,
        grid=(nbg, rows_total // block_rows),
        in_specs=[blk, blk, blk,
                  pl.BlockSpec((1, d), lambda i, j: (0, 0)),
                  pl.BlockSpec(w_glu.shape, lambda i, j: (0, 0))],
        out_specs=blk,
        out_shape=jax.ShapeDtypeStruct(u_g.shape, F32),
        compiler_params=pltpu.CompilerParams(vmem_limit_bytes=MOSAIC_VMEM_LIMIT),
        name="s5_glu",
    )(y_fw, y_bw, u_g, d_skip.reshape(1, d), w_glu.astype(BF16))


def s5_mixer(u_ctx, u_lat, lam_re, lam_im, log_dt, b_re, b_im, c_re, c_im, d_skip, w_glu, need_ctx):
    b_, n_ctx, d = u_ctx.shape
    n_all = n_ctx + u_lat.shape[1]
    bt = S5_BATCH_TILE
    u_all = jnp.concatenate([u_ctx, u_lat], axis=1)
    u_g = u_all.reshape(b_ // bt, bt, n_all, d).transpose(0, 2, 1, 3).reshape(b_ // bt, n_all * bt, d)
    ys = [s5_scan(u_g, lam_re[dr], lam_im[dr], log_dt[dr], b_re[dr], b_im[dr], c_re[dr], c_im[dr],
                  n_ctx, reverse=bool(dr)) for dr in range(2)]
    o_g = s5_glu(ys[0], ys[1], u_g, d_skip, w_glu)
    o = o_g.reshape(b_ // bt, n_all, bt, d).transpose(0, 2, 1, 3).reshape(b_, n_all, d)
    return o[:, n_ctx:], (o[:, :n_ctx] if need_ctx else None)


HG_SCAN_TOKENS = 256
HG_SUB = 32


def _hgrn_scan_kernel(q_ref, f_ref, v_ref, lb_ref, tri_ref, o_ref, q_s, k_s, cum_s, v_s, st, *, reverse):
    tokens = q_ref.shape[1]

    @pl.when(pl.program_id(2) == 0)
    def _():
        st[...] = jnp.zeros_like(st)

    f = f_ref[0]
    log_f = jnp.logaddexp(lb_ref[0:1, :], lb_ref[1:2, :] + jax.nn.log_sigmoid(f))
    k_s[...] = lb_ref[2:3, :] * jax.nn.sigmoid(-f)
    q_s[...] = jax.nn.silu(q_ref[0])
    v_s[...] = v_ref[0]
    cum_s[...] = jnp.dot(tri_ref[...], log_f, precision=lax.Precision.HIGHEST, preferred_element_type=F32)
    tpos = lax.broadcasted_iota(jnp.int32, (HG_SUB, q_ref.shape[2]), 0)
    nsub = tokens // HG_SUB
    ones = jnp.ones((q_ref.shape[2], v_ref.shape[2]), BF16)

    def block(i, carry):
        c = (nsub - 1 - i) if reverse else i
        base = pl.multiple_of(c * HG_SUB, HG_SUB)
        qc = q_s[pl.ds(base, HG_SUB), :]
        cumc = cum_s[pl.ds(base, HG_SUB), :]

        def column(s, o):
            ks = k_s[pl.ds(base + s, 1), :]
            cs = cum_s[pl.ds(base + s, 1), :]
            vs = v_s[pl.ds(base + s, 1), :]
            seen = (tpos <= s) if reverse else (tpos >= s)
            decay = jnp.exp(jnp.where(seen, cumc - cs, -jnp.inf))
            att = jnp.dot((qc * ks * decay).astype(BF16), ones, preferred_element_type=F32)
            return o + att * vs

        o = lax.fori_loop(0, HG_SUB, column, jnp.zeros(qc.shape, F32), unroll=True)
        s_t = st[...]
        o = o + lax.dot_general((qc * jnp.exp(cumc)).astype(BF16), s_t.astype(BF16),
                                (((1,), (1,)), ((), ())), preferred_element_type=F32)
        last = cum_s[pl.ds(base + (0 if reverse else HG_SUB - 1), 1), :]
        kh = (k_s[pl.ds(base, HG_SUB), :] * jnp.exp(last - cumc)).astype(BF16)
        vc = v_s[pl.ds(base, HG_SUB), :].astype(BF16)
        st[...] = s_t * jnp.exp(last) + lax.dot_general(vc, kh, (((0,), (0,)), ((), ())),
                                                        preferred_element_type=F32)
        o_ref[0, pl.ds(base, HG_SUB), :] = o
        return carry

    lax.fori_loop(0, nsub, block, 0)


def hgrn_scan(proj, lb_rows, n_ctx, f_col, reverse):
    b_, n, _ = proj.shape
    t = HG_SCAN_TOKENS
    hk = HG_KEY
    n_chunks, ctx_chunks = n // t, n_ctx // t
    if reverse:
        chunk = lambda s: jnp.where(s < ctx_chunks, ctx_chunks - 1 - s, n_chunks - 1 - (s - ctx_chunks))
    else:
        chunk = lambda s: s
    pos = jnp.arange(t)
    same = (pos[:, None] // HG_SUB) == (pos[None, :] // HG_SUB)
    order = (pos[None, :] >= pos[:, None]) if reverse else (pos[None, :] <= pos[:, None])
    tri = (same & order).astype(F32)
    col = lambda off: pl.BlockSpec((1, t, hk), lambda i, h, s: (i, chunk(s), off * HG_HEADS + h))
    return pl.pallas_call(
        functools.partial(_hgrn_scan_kernel, reverse=reverse),
        grid=(b_, HG_HEADS, n_chunks),
        in_specs=[col(0), col(f_col), col(3),
                  pl.BlockSpec((3, hk), lambda i, h, s: (0, h)),
                  pl.BlockSpec((t, t), lambda i, h, s: (0, 0))],
        out_specs=pl.BlockSpec((1, t, hk), lambda i, h, s: (i, chunk(s), h)),
        out_shape=jax.ShapeDtypeStruct((b_, n, HG_HEADS * HG_VAL), F32),
        scratch_shapes=[pltpu.VMEM((t, hk), F32)] * 4 + [pltpu.VMEM((HG_VAL, hk), F32)],
        compiler_params=pltpu.CompilerParams(dimension_semantics=("arbitrary", "arbitrary", "arbitrary"),
                                             vmem_limit_bytes=MOSAIC_VMEM_LIMIT),
        name="hgrn_scan_bwd" if reverse else "hgrn_scan_fwd",
    )(proj, proj, proj, lb_rows, tri)


def _hgrn_out_kernel(of_ref, ob_ref, gate_ref, g_ref, w_ref, o_ref):
    o = of_ref[0] + ob_ref[0]
    parts = []
    for h in range(HG_HEADS):
        oh = o[:, h * HG_VAL:(h + 1) * HG_VAL]
        parts.append(oh * lax.rsqrt(jnp.mean(oh * oh, axis=-1, keepdims=True) + RMS_EPS))
    y = jnp.concatenate(parts, axis=1) * g_ref[...] * jax.nn.silu(gate_ref[0])
    o_ref[0] = jnp.dot(y.astype(BF16), w_ref[...], preferred_element_type=F32)


def hgrn_out(o_fw, o_bw, proj, norm_g, w_out, block_rows=256):
    b_, n, d = o_fw.shape
    assert n % block_rows == 0
    row = pl.BlockSpec((1, block_rows, d), lambda i, j: (i, j, 0))
    return pl.pallas_call(
        _hgrn_out_kernel,
        grid=(b_, n // block_rows),
        in_specs=[row, row, pl.BlockSpec((1, block_rows, d), lambda i, j: (i, j, 4)),
                  pl.BlockSpec((1, d), lambda i, j: (0, 0)), pl.BlockSpec(w_out.shape, lambda i, j: (0, 0))],
        out_specs=row,
        out_shape=jax.ShapeDtypeStruct((b_, n, d), F32),
        compiler_params=pltpu.CompilerParams(vmem_limit_bytes=MOSAIC_VMEM_LIMIT),
        name="hgrn_out",
    )(o_fw, o_bw, proj, norm_g.reshape(1, d), w_out.astype(BF16))


def hgrn2_mixer(u_ctx, u_lat, w_in, w_out, norm_g, lb, need_ctx):
    n_ctx = u_ctx.shape[1]
    proj = linear(jnp.concatenate([u_ctx, u_lat], axis=1), w_in)
    lb_rows = jnp.stack([jnp.log(lb), jnp.log1p(-lb), 1.0 - lb])
    o_fw = hgrn_scan(proj, lb_rows, n_ctx, 1, reverse=False)
    o_bw = hgrn_scan(proj, lb_rows, n_ctx, 2, reverse=True)
    o = hgrn_out(o_fw, o_bw, proj, norm_g, w_out)
    return o[:, n_ctx:], (o[:, :n_ctx] if need_ctx else None)


PEER_SLOTS = PEER_HEADS * PEER_TOPK
PEER_SEL_TOKENS = 256
PEER_MIX_TOKENS = 128
PEER_ROW_BUFFERS = 4


def _topk_axis0(cur, k, payload=None):
    rows = cur.shape[0]
    iota = lax.broadcasted_iota(jnp.int32, cur.shape, 0)
    vals, picks = [], []
    for _ in range(k):
        m = jnp.max(cur, axis=0, keepdims=True)
        pos = jnp.min(jnp.where(cur == m, iota, rows), axis=0, keepdims=True)
        hit = iota == pos
        vals.append(m)
        if payload is None:
            picks.append(pos)
        else:
            picks.append(jnp.sum(jnp.where(hit, payload, 0), axis=0, keepdims=True))
        cur = jnp.where(hit, -jnp.inf, cur)
    return jnp.concatenate(vals, axis=0), jnp.concatenate(picks, axis=0)


def _peer_select_kernel(h_ref, sc_ref, sh_ref, wq_ref, keys_ref, idx_ref, g_ref):
    half = PEER_QDIM // 2
    x = h_ref[0] * (1.0 + sc_ref[0]) + sh_ref[0]
    q = jnp.dot(x.astype(BF16), wq_ref[...], preferred_element_type=F32)
    tokens = x.shape[0]
    for c0 in range(0, tokens, LANES):
        idx_rows, g_rows = [], []
        for hd in range(PEER_HEADS):
            tops = []
            for c in range(2):
                lo = (hd * 2 + c) * half
                qhc = q[c0:c0 + LANES, lo:lo + half].astype(BF16)
                s_t = lax.dot_general(keys_ref[c], qhc, (((1,), (1,)), ((), ())),
                                      preferred_element_type=F32)
                tops.append(_topk_axis0(s_t, PEER_TOPK))
            (s1, i1), (s2, i2) = tops
            width = [PEER_TOPK // (a + 1) for a in range(PEER_TOPK)]
            pad = -sum(width) % 8
            cand_s = jnp.concatenate([s1[a:a + 1] + s2[:width[a]] for a in range(PEER_TOPK)]
                                     + [jnp.full((pad, LANES), -jnp.inf, F32)], axis=0)
            cand_i = jnp.concatenate([i1[a:a + 1] * PEER_NKEYS + i2[:width[a]] for a in range(PEER_TOPK)]
                                     + [jnp.zeros((pad, LANES), jnp.int32)], axis=0)
            top_s, top_i = _topk_axis0(cand_s, PEER_TOPK, payload=cand_i)
            e = jnp.exp(top_s - top_s[0:1])
            g_rows.append(e / jnp.sum(e, axis=0, keepdims=True))
            idx_rows.append(top_i)
        idx_ref[0, c0:c0 + LANES, :] = jnp.concatenate(idx_rows, axis=0).T
        g_ref[0, c0:c0 + LANES, :] = jnp.concatenate(g_rows, axis=0).T


def peer_select(h, sc, sh, w_q, sub_keys):
    b_, n, d = h.shape
    tb = min(PEER_SEL_TOKENS, n)
    mod_map = (lambda i, j: (i, 0, 0)) if sc.shape[0] == b_ else (lambda i, j: (0, 0, 0))
    return pl.pallas_call(
        _peer_select_kernel,
        grid=(b_, n // tb),
        in_specs=[pl.BlockSpec((1, tb, d), lambda i, j: (i, j, 0)),
                  pl.BlockSpec((1, 1, d), mod_map),
                  pl.BlockSpec((1, 1, d), mod_map),
                  pl.BlockSpec(w_q.shape, lambda i, j: (0, 0)),
                  pl.BlockSpec(sub_keys.shape, lambda i, j: (0, 0, 0))],
        out_specs=[pl.BlockSpec((1, tb, PEER_SLOTS), lambda i, j: (i, j, 0)),
                   pl.BlockSpec((1, tb, PEER_SLOTS), lambda i, j: (i, j, 0))],
        out_shape=[jax.ShapeDtypeStruct((b_, n, PEER_SLOTS), jnp.int32),
                   jax.ShapeDtypeStruct((b_, n, PEER_SLOTS), F32)],
        compiler_params=pltpu.CompilerParams(vmem_limit_bytes=48 * 1024 * 1024),
        name="peer_select",
    )(h, sc, sh, w_q.astype(BF16), sub_keys.astype(BF16))


def peer_table(u_tab, v_tab):
    bits = lambda t: lax.bitcast_convert_type(t.astype(BF16), jnp.uint16).astype(jnp.uint32)
    return ((bits(u_tab) << 16) | bits(v_tab))[:, None, :]


def _peer_mix_kernel(idx_hbm, h_ref, sc_ref, sh_ref, g_ref, tab_hbm, out_ref,
                     idx_smem, x_scr, rows, row_sem, idx_sem):
    tokens, d = x_scr.shape
    nbuf = PEER_ROW_BUFFERS
    ahead = nbuf - 1
    blk = pl.program_id(0) * pl.num_programs(1) + pl.program_id(1)
    per_blk = tokens * PEER_SLOTS
    idx_copy = pltpu.make_async_copy(idx_hbm.at[pl.ds(pl.multiple_of(blk * per_blk, per_blk), per_blk)],
                                     idx_smem, idx_sem)
    idx_copy.start()
    x_scr[...] = h_ref[0] * (1.0 + sc_ref[0]) + sh_ref[0]
    idx_copy.wait()

    def fetch(t, slot):
        for j in range(PEER_SLOTS):
            pltpu.make_async_copy(tab_hbm.at[idx_smem[t * PEER_SLOTS + j]],
                                  rows.at[slot, pl.ds(j, 1), :], row_sem.at[slot]).start(priority=j % 2)

    def wait_rows(slot):
        pltpu.make_async_copy(rows.at[slot], rows.at[slot], row_sem.at[slot]).wait()

    eye = (lax.broadcasted_iota(jnp.int32, (PEER_SLOTS, PEER_SLOTS), 0)
           == lax.broadcasted_iota(jnp.int32, (PEER_SLOTS, PEER_SLOTS), 1))

    def combine(t, slot):
        x_row = x_scr[pl.ds(t, 1), :]
        words = rows[slot]
        u_rows = lax.bitcast_convert_type(words & jnp.uint32(0xFFFF0000), F32)
        v_rows = lax.bitcast_convert_type(words << 16, F32)
        act = jnp.sum(u_rows * x_row, axis=1, keepdims=True)
        g_col = jnp.sum(jnp.where(eye, g_ref[0, pl.ds(t, 1), :], 0.0), axis=1, keepdims=True)
        w = g_col * _gelu_tanh(act)
        out_ref[0, pl.ds(t, 1), :] = jnp.sum(w * v_rows, axis=0, keepdims=True)

    def token(t, slot, prefetch):
        if prefetch:
            fetch(t + ahead, (slot + ahead) % nbuf)
        wait_rows(slot)
        combine(t, slot)

    for t0 in range(ahead):
        fetch(t0, t0)

    def group(i, carry):
        for slot in range(nbuf):
            token(i * nbuf + slot, slot, True)
        return carry

    lax.fori_loop(0, tokens // nbuf - 1, group, 0)
    for slot in range(nbuf):
        token(tokens - nbuf + slot, slot, slot + ahead < nbuf)


def peer_mix(h, sc, sh, idx, g, uv_tab):
    b_, n, d = h.shape
    tb = min(PEER_MIX_TOKENS, n)
    assert n % tb == 0 and tb % PEER_ROW_BUFFERS == 0
    mod_map = (lambda i, j: (i, 0, 0)) if sc.shape[0] == b_ else (lambda i, j: (0, 0, 0))
    return pl.pallas_call(
        _peer_mix_kernel,
        grid=(b_, n // tb),
        in_specs=[pl.BlockSpec(memory_space=pl.ANY),
                  pl.BlockSpec((1, tb, d), lambda i, j: (i, j, 0)),
                  pl.BlockSpec((1, 1, d), mod_map),
                  pl.BlockSpec((1, 1, d), mod_map),
                  pl.BlockSpec((1, tb, PEER_SLOTS), lambda i, j: (i, j, 0)),
                  pl.BlockSpec(memory_space=pl.ANY)],
        out_specs=pl.BlockSpec((1, tb, d), lambda i, j: (i, j, 0)),
        out_shape=jax.ShapeDtypeStruct(h.shape, F32),
        scratch_shapes=[pltpu.SMEM((tb * PEER_SLOTS,), jnp.int32),
                        pltpu.VMEM((tb, d), F32),
                        pltpu.VMEM((PEER_ROW_BUFFERS, PEER_SLOTS, d), jnp.uint32),
                        pltpu.SemaphoreType.DMA((PEER_ROW_BUFFERS,)),
                        pltpu.SemaphoreType.DMA(())],
        compiler_params=pltpu.CompilerParams(vmem_limit_bytes=32 * 1024 * 1024),
        name="peer_mix",
    )(idx.reshape(-1), h, sc, sh, g, uv_tab)


def peer_ffn(h, sc, sh, w_q, sub_keys, uv_tab):
    idx, g = peer_select(h, sc, sh, w_q, sub_keys)
    return peer_mix(h, sc, sh, idx, g, uv_tab)


def kernel(x, c, ctx, c_ctx, ada_w, ada_b, ln_g, ln_b, da_w_in, da_w_out, da_lam_q, da_lam_k, da_subln,
           s5_lam_re, s5_lam_im, s5_log_dt, s5_b_re, s5_b_im, s5_c_re, s5_c_im, s5_d, s5_w_glu,
           hg_w_in, hg_w_out, hg_norm, hg_lb, peer_wq, peer_keys, peer_u, peer_v):
    L = x.shape[1]
    cos, sin = axial_rope(L, DA_HEAD_DIM)
    s_c = jax.nn.silu(c)
    s_ctx = jax.nn.silu(c_ctx)
    lb_soft = jax.nn.softmax(hg_lb.astype(F32), axis=0)
    lb_all = jnp.cumsum(lb_soft, axis=0) - lb_soft[0]
    h, hc = x, ctx
    for i in range(DEPTH):
        kind, slot = LAYER_TYPES[i], i // N_MIXERS
        need_ctx = i < DEPTH - 1
        mod = (s_c @ ada_w[i] + ada_b[i])[:, None, :]
        mod_c = s_ctx @ ada_w[i] + ada_b[i]
        sh1, sc1, g1, sh2, sc2, g2 = jnp.split(mod, 6, axis=-1)
        csh1, csc1, cg1, csh2, csc2, cg2 = jnp.split(mod_c, 6, axis=-1)
        u = h * (1.0 + sc1) + sh1
        uc = hc * (1.0 + csc1) + csh1
        if kind == 0:
            lam_init = 0.8 - 0.6 * math.exp(-0.3 * i)
            o, oc = diff_attention(uc, u, da_w_in[slot], da_w_out[slot], da_lam_q[slot], da_lam_k[slot],
                                   da_subln[slot], lam_init, cos, sin, need_ctx)
        elif kind == 1:
            o, oc = s5_mixer(uc, u, s5_lam_re[slot], s5_lam_im[slot], s5_log_dt[slot], s5_b_re[slot],
                             s5_b_im[slot], s5_c_re[slot], s5_c_im[slot], s5_d[slot], s5_w_glu[slot], need_ctx)
        else:
            o, oc = hgrn2_mixer(uc, u, hg_w_in[slot], hg_w_out[slot], hg_norm[slot], lb_all[i], need_ctx)
        h = residual_layer_norm(h, o, g1, ln_g[i, 0], ln_b[i, 0])
        uv_tab = peer_table(peer_u[i], peer_v[i])
        f = peer_ffn(h, sc2, sh2, peer_wq[i], peer_keys[i], uv_tab)
        h = residual_layer_norm(h, f, g2, ln_g[i, 1], ln_b[i, 1])
        if need_ctx:
            hc = residual_layer_norm(hc, oc, cg1.reshape(1, 1, -1), ln_g[i, 0], ln_b[i, 0])
            fc = peer_ffn(hc, csc2.reshape(1, 1, -1), csh2.reshape(1, 1, -1), peer_wq[i], peer_keys[i], uv_tab)
            hc = residual_layer_norm(hc, fc, cg2.reshape(1, 1, -1), ln_g[i, 1], ln_b[i, 1])
    return h
```

```python
import functools
import math
import jax, jax.numpy as jnp
from jax import lax
import numpy as np
from jax.experimental import pallas as pl
from jax.experimental.pallas import tpu as pltpu

D_MODEL = 1024
BATCH = 32
SEQ = 2048
DEPTH = 4

CTX_LEN = 256
GRID_W = 64
N_MIXERS = 3
LAYER_TYPES = tuple(i % N_MIXERS for i in range(DEPTH))
N_ATTN = LAYER_TYPES.count(0)
N_S5 = LAYER_TYPES.count(1)
N_HG = LAYER_TYPES.count(2)

DA_HEADS = 8
DA_HEAD_DIM = 64
DA_V_DIM = 2 * DA_HEAD_DIM
Q_BLOCK = 128
ROPE_THETA = 10000.0
S5_GROUP = 16
S5_GROUPS = D_MODEL // S5_GROUP
S5_STATE = 64
HG_HEADS = 8
HG_KEY = D_MODEL // HG_HEADS
HG_VAL = D_MODEL // HG_HEADS
HG_CHUNK = 32
PEER_HEADS = 8
PEER_NKEYS = 128
PEER_EXPERTS = PEER_NKEYS * PEER_NKEYS
PEER_QDIM = 256
PEER_TOPK = 16
PEER_BLOCK = 128
LN_EPS = 1e-5
RMS_EPS = 1e-6
DN_ALPHA = (2 * DEPTH) ** 0.25
DN_BETA = (8 * DEPTH) ** -0.25

F32 = jnp.float32
BF16 = jnp.bfloat16
LANES = 128
MOSAIC_VMEM_LIMIT = 48 * 1024 * 1024


def _gelu_tanh(x):
    return 0.5 * x * (1.0 + jnp.tanh(math.sqrt(2.0 / math.pi) * (x + 0.044715 * (x * x * x))))


def _res_ln_kernel(h_ref, o_ref, gate_ref, g_ref, b_ref, out_ref):
    y = DN_ALPHA * h_ref[0] + gate_ref[0] * o_ref[0]
    mu = jnp.mean(y, -1, keepdims=True)
    yc = y - mu
    var = jnp.mean(yc * yc, -1, keepdims=True)
    out_ref[0] = yc * lax.rsqrt(var + LN_EPS) * g_ref[...] + b_ref[...]


def residual_layer_norm(h, o, gate, g, b, block_n=512):
    b_, n, d = h.shape
    bn = min(block_n, n)
    per_batch_gate = gate.shape[0] == b_
    gate_map = (lambda i, j: (i, 0, 0)) if per_batch_gate else (lambda i, j: (0, 0, 0))
    return pl.pallas_call(
        _res_ln_kernel,
        grid=(b_, n // bn),
        in_specs=[pl.BlockSpec((1, bn, d), lambda i, j: (i, j, 0)),
                  pl.BlockSpec((1, bn, d), lambda i, j: (i, j, 0)),
                  pl.BlockSpec((1, 1, d), gate_map),
                  pl.BlockSpec((1, d), lambda i, j: (0, 0)),
                  pl.BlockSpec((1, d), lambda i, j: (0, 0))],
        out_specs=pl.BlockSpec((1, bn, d), lambda i, j: (i, j, 0)),
        out_shape=jax.ShapeDtypeStruct(h.shape, h.dtype),
        name="residual_layer_norm",
    )(h, o, gate, g.reshape(1, d), b.reshape(1, d))


def axial_rope(length, dim):
    rows = length // GRID_W
    row = jnp.repeat(jnp.arange(rows, dtype=F32), GRID_W)
    col = jnp.tile(jnp.arange(GRID_W, dtype=F32), rows)
    n_freq = dim // 4
    inv = ROPE_THETA ** (-jnp.arange(n_freq, dtype=F32) / n_freq)
    ang = jnp.concatenate([row[:, None] * inv, col[:, None] * inv], axis=-1)
    return jnp.cos(ang), jnp.sin(ang)


def _linear_kernel(x_ref, w_ref, o_ref):
    o_ref[0] = jnp.dot(x_ref[0].astype(BF16), w_ref[...], preferred_element_type=F32).astype(o_ref.dtype)


def linear(x, w, out_dtype=F32, block_rows=256, block_cols=1024):
    b_, n, kdim = x.shape
    ncols = w.shape[1]
    br, bc = min(block_rows, n), min(block_cols, ncols)
    assert n % br == 0 and ncols % bc == 0
    return pl.pallas_call(
        _linear_kernel,
        grid=(b_, n // br, ncols // bc),
        in_specs=[pl.BlockSpec((1, br, kdim), lambda i, j, c: (i, j, 0)),
                  pl.BlockSpec((kdim, bc), lambda i, j, c: (0, c))],
        out_specs=pl.BlockSpec((1, br, bc), lambda i, j, c: (i, j, c)),
        out_shape=jax.ShapeDtypeStruct((b_, n, ncols), out_dtype),
        compiler_params=pltpu.CompilerParams(vmem_limit_bytes=MOSAIC_VMEM_LIMIT),
        name="linear",
    )(x, w.astype(BF16))


DA_Q_TOKENS = 256
DA_PROJ_TOKENS = 256


def _da_qkv_kernel(x_ref, w_ref, wsw_ref, cos_ref, sin_ref, q_ref, k_ref, v_ref):
    d = q_ref.shape[-1]
    x = x_ref[0].astype(BF16)
    reps = 2 * d // cos_ref.shape[-1]
    c = jnp.concatenate([cos_ref[...]] * reps, axis=1)
    s = jnp.concatenate([sin_ref[...]] * reps, axis=1)
    qk = (jnp.dot(x, w_ref[:, :2 * d], preferred_element_type=F32) * c
          + jnp.dot(x, wsw_ref[...], preferred_element_type=F32) * s)
    q_ref[0] = (qk[:, :d] * DA_HEAD_DIM ** -0.5).astype(BF16)
    k_ref[0] = qk[:, d:].astype(BF16)
    v_ref[0] = jnp.dot(x, w_ref[:, 2 * d:], preferred_element_type=F32).astype(BF16)


def da_qkv(u_all, w_in, cos_t, sin_t):
    b_, n, d = u_all.shape
    tb = DA_PROJ_TOKENS
    swap = jnp.arange(2 * d) ^ 1
    w = w_in.astype(BF16)
    w_sw = w[:, :2 * d][:, swap]
    row = pl.BlockSpec((1, tb, d), lambda i, j: (i, j, 0))
    tab = pl.BlockSpec((tb, cos_t.shape[1]), lambda i, j: (j, 0))
    return pl.pallas_call(
        _da_qkv_kernel,
        grid=(b_, n // tb),
        in_specs=[row, pl.BlockSpec(w.shape, lambda i, j: (0, 0)), pl.BlockSpec(w_sw.shape, lambda i, j: (0, 0)),
                  tab, tab],
        out_specs=[row, row, row],
        out_shape=[jax.ShapeDtypeStruct((b_, n, d), BF16)] * 3,
        compiler_params=pltpu.CompilerParams(vmem_limit_bytes=MOSAIC_VMEM_LIMIT),
        name="da_qkv",
    )(u_all, w, w_sw, cos_t, sin_t)


def _da_attn_kernel(lam_ref, q_ref, k_ref, v_ref, g_ref, o_ref, *, post_scale):
    q, k, v = q_ref[0], k_ref[0], v_ref[0]
    lane = lax.broadcasted_iota(jnp.int32, q.shape, 1)
    zero = jnp.zeros_like(q)
    contract_last = (((1,), (1,)), ((), ()))

    def softmax_parts(qm):
        s = lax.dot_general(qm, k, contract_last, preferred_element_type=F32)
        e = jnp.exp(s - jnp.max(s, axis=-1, keepdims=True))
        return e, jnp.sum(e, axis=-1, keepdims=True)

    e0, z0 = softmax_parts(jnp.where(lane < DA_HEAD_DIM, q, zero))
    e1, z1 = softmax_parts(jnp.where(lane >= DA_HEAD_DIM, q, zero))
    a = e0 / z0 - lam_ref[0] * (e1 / z1)
    o = jnp.dot(a.astype(BF16), v, preferred_element_type=F32)
    o = o * lax.rsqrt(jnp.mean(o * o, axis=-1, keepdims=True) + RMS_EPS)
    o_ref[0] = (o * g_ref[...] * post_scale).astype(o_ref.dtype)


def da_attend(q, k, v, lam, subln_g, post_scale, q_start, n_q, n_k):
    b_, _, d = q.shape
    hd = d // DA_HEADS
    tq = DA_Q_TOKENS
    q0 = q_start // tq
    return pl.pallas_call(
        functools.partial(_da_attn_kernel, post_scale=post_scale),
        grid=(b_, DA_HEADS, n_q // tq),
        in_specs=[pl.BlockSpec(memory_space=pltpu.SMEM),
                  pl.BlockSpec((1, tq, hd), lambda i, h, j: (i, q0 + j, h)),
                  pl.BlockSpec((1, n_k, hd), lambda i, h, j: (i, 0, h)),
                  pl.BlockSpec((1, n_k, hd), lambda i, h, j: (i, 0, h)),
                  pl.BlockSpec((1, hd), lambda i, h, j: (0, 0))],
        out_specs=pl.BlockSpec((1, tq, hd), lambda i, h, j: (i, j, h)),
        out_shape=jax.ShapeDtypeStruct((b_, n_q, d), BF16),
        compiler_params=pltpu.CompilerParams(vmem_limit_bytes=MOSAIC_VMEM_LIMIT),
        name="da_attend",
    )(lam.reshape(1), q, k, v, subln_g.reshape(1, hd))


def diff_attention(u_ctx, u_lat, w_in, w_out, lam_q, lam_k, subln_g, lam_init, cos, sin, need_ctx):
    b_, n_ctx, d = u_ctx.shape
    n_lat = u_lat.shape[1]
    lanes_cos = jnp.tile(jnp.repeat(cos, 2, axis=1), (1, 2))
    lanes_sin = jnp.tile(jnp.stack([-sin, sin], axis=-1).reshape(n_lat, -1), (1, 2))
    cos_t = jnp.concatenate([jnp.ones((n_ctx, lanes_cos.shape[1]), F32), lanes_cos], axis=0)
    sin_t = jnp.concatenate([jnp.zeros((n_ctx, lanes_sin.shape[1]), F32), lanes_sin], axis=0)
    u_all = jnp.concatenate([u_ctx, u_lat], axis=1)
    q, k, v = da_qkv(u_all, w_in, cos_t, sin_t)
    lq, lk = lam_q.astype(F32), lam_k.astype(F32)
    lam = jnp.exp(jnp.sum(lq[0] * lk[0])) - jnp.exp(jnp.sum(lq[1] * lk[1])) + lam_init
    post = 1.0 - lam_init
    o_lat = linear(da_attend(q, k, v, lam, subln_g, post, n_ctx, n_lat, n_ctx + n_lat), w_out)
    o_ctx = linear(da_attend(q, k, v, lam, subln_g, post, 0, n_ctx, n_ctx), w_out) if need_ctx else None
    return o_lat, o_ctx


def s5_discretize(lam_re, lam_im, log_dt, b_re, b_im):
    lam_re, lam_im = lam_re.astype(F32), lam_im.astype(F32)
    b_re, b_im = b_re.astype(F32), b_im.astype(F32)
    dt = jnp.exp(log_dt.astype(F32))[:, None]
    mag = jnp.exp(lam_re * dt)
    abar_re, abar_im = mag * jnp.cos(lam_im * dt), mag * jnp.sin(lam_im * dt)
    nr, ni = abar_re - 1.0, abar_im
    den = lam_re * lam_re + lam_im * lam_im
    k_re = (nr * lam_re + ni * lam_im) / den
    k_im = (ni * lam_re - nr * lam_im) / den
    bb_re = k_re[..., None] * b_re - k_im[..., None] * b_im
    bb_im = k_re[..., None] * b_im + k_im[..., None] * b_re
    return abar_re, abar_im, bb_re, bb_im


S5_BATCH_TILE = 8
S5_SCAN_TOKENS = 128
S5_PANEL_GROUPS = 16
S5_PANELS = S5_GROUPS // S5_PANEL_GROUPS


def _s5_panels(bb, c):
    g, p, m = bb.shape
    pg = S5_PANEL_GROUPS
    eye = jnp.eye(pg, dtype=bb.dtype)
    w_in = jnp.einsum('qgpm,gh->qgmhp', bb.reshape(g // pg, pg, p, m), eye).reshape(g // pg, pg * m, pg * p)
    w_out = jnp.einsum('qgmp,gh->qgphm', c.reshape(g // pg, pg, m, p), eye).reshape(g // pg, pg * p, pg * m)
    return w_in.astype(BF16), w_out.astype(BF16)


def _s5_scan_kernel(u_ref, wb_re_ref, wb_im_ref, a_re_ref, a_im_ref, c_re_ref, c_im_ref, y_ref,
                    x_re, x_im, state, *, reverse):
    rows = u_ref.shape[1]
    tokens = rows // S5_BATCH_TILE
    cin = S5_PANEL_GROUPS * S5_GROUP

    @pl.when(pl.program_id(1) == 0)
    def _():
        state[...] = jnp.zeros_like(state)

    for p in range(S5_PANELS):
        ub = u_ref[0, :, p * cin:(p + 1) * cin].astype(BF16)
        x_re[...] = jnp.dot(ub, wb_re_ref[p], preferred_element_type=F32)
        x_im[...] = jnp.dot(ub, wb_im_ref[p], preferred_element_type=F32)
        a_re, a_im = a_re_ref[p], a_im_ref[p]

        def step(i, carry):
            xr, xi = carry
            t = (tokens - 1 - i) if reverse else i
            r0 = pl.multiple_of(t * S5_BATCH_TILE, S5_BATCH_TILE)
            nr = a_re * xr - a_im * xi + x_re[pl.ds(r0, S5_BATCH_TILE), :]
            ni = a_re * xi + a_im * xr + x_im[pl.ds(r0, S5_BATCH_TILE), :]
            x_re[pl.ds(r0, S5_BATCH_TILE), :] = nr
            x_im[pl.ds(r0, S5_BATCH_TILE), :] = ni
            return nr, ni

        xr, xi = lax.fori_loop(0, tokens, step, (state[p, 0], state[p, 1]), unroll=2)
        state[p, 0] = xr
        state[p, 1] = xi
        y_ref[0, :, p * cin:(p + 1) * cin] = (
            jnp.dot(x_re[...].astype(BF16), c_re_ref[p], preferred_element_type=F32)
            - jnp.dot(x_im[...].astype(BF16), c_im_ref[p], preferred_element_type=F32))


def s5_scan(u_g, lam_re, lam_im, log_dt, b_re, b_im, c_re, c_im, n_ctx, reverse):
    nbg, rows_total, d = u_g.shape
    abar_re, abar_im, bb_re, bb_im = s5_discretize(lam_re, lam_im, log_dt, b_re, b_im)
    wb_re, cp_re = _s5_panels(bb_re, c_re.astype(F32))
    wb_im, cp_im = _s5_panels(bb_im, c_im.astype(F32))
    states = S5_PANEL_GROUPS * S5_STATE
    tile = lambda a: jnp.broadcast_to(a.reshape(S5_PANELS, 1, states), (S5_PANELS, S5_BATCH_TILE, states))
    rows = S5_SCAN_TOKENS * S5_BATCH_TILE
    n_chunks = rows_total // rows
    ctx_chunks = n_ctx // S5_SCAN_TOKENS
    if reverse:
        chunk = lambda s: jnp.where(s < ctx_chunks, ctx_chunks - 1 - s, n_chunks - 1 - (s - ctx_chunks))
    else:
        chunk = lambda s: s
    full = lambda a: pl.BlockSpec(a.shape, lambda i, s: (0,) * a.ndim)
    a_re_t, a_im_t = tile(abar_re), tile(abar_im)
    return pl.pallas_call(
        functools.partial(_s5_scan_kernel, reverse=reverse),
        grid=(nbg, n_chunks),
        in_specs=[pl.BlockSpec((1, rows, d), lambda i, s: (i, chunk(s), 0)),
                  full(wb_re), full(wb_im), full(a_re_t), full(a_im_t), full(cp_re), full(cp_im)],
        out_specs=pl.BlockSpec((1, rows, d), lambda i, s: (i, chunk(s), 0)),
        out_shape=jax.ShapeDtypeStruct(u_g.shape, F32),
        scratch_shapes=[pltpu.VMEM((rows, states), F32), pltpu.VMEM((rows, states), F32),
                        pltpu.VMEM((S5_PANELS, 2, S5_BATCH_TILE, states), F32)],
        compiler_params=pltpu.CompilerParams(dimension_semantics=("arbitrary", "arbitrary"),
                                             vmem_limit_bytes=MOSAIC_VMEM_LIMIT),
        name="s5_scan_bwd" if reverse else "s5_scan_fwd",
    )(u_g, wb_re, wb_im, a_re_t, a_im_t, cp_re, cp_im)


def _s5_glu_kernel(yf_ref, yb_ref, u_ref, d_ref, w_ref, o_ref):
    d = o_ref.shape[-1]
    y = yf_ref[0] + yb_ref[0] + d_ref[...] * u_ref[0]
    r = jnp.dot(_gelu_tanh(y).astype(BF16), w_ref[...], preferred_element_type=F32)
    o_ref[0] = r[:, :d] * jax.nn.sigmoid(r[:, d:])


def s5_glu(y_fw, y_bw, u_g, d_skip, w_glu, block_rows=512):
    nbg, rows_total, d = u_g.shape
    blk = pl.BlockSpec((1, block_rows, d), lambda i, j: (i, j, 0))
    return pl.pallas_call(
        _s5_glu_kernel,
        grid=(nbg, rows_total // block_rows),
        in_specs=[blk, blk, blk,
                  pl.BlockSpec((1, d), lambda i, j: (0, 0)),
                  pl.BlockSpec(w_glu.shape, lambda i, j: (0, 0))],
        out_specs=blk,
        out_shape=jax.ShapeDtypeStruct(u_g.shape, F32),
        compiler_params=pltpu.CompilerParams(vmem_limit_bytes=MOSAIC_VMEM_LIMIT),
        name="s5_glu",
    )(y_fw, y_bw, u_g, d_skip.reshape(1, d), w_glu.astype(BF16))


def s5_mixer(u_ctx, u_lat, lam_re, lam_im, log_dt, b_re, b_im, c_re, c_im, d_skip, w_glu, need_ctx):
    b_, n_ctx, d = u_ctx.shape
    n_all = n_ctx + u_lat.shape[1]
    bt = S5_BATCH_TILE
    u_all = jnp.concatenate([u_ctx, u_lat], axis=1)
    u_g = u_all.reshape(b_ // bt, bt, n_all, d).transpose(0, 2, 1, 3).reshape(b_ // bt, n_all * bt, d)
    ys = [s5_scan(u_g, lam_re[dr], lam_im[dr], log_dt[dr], b_re[dr], b_im[dr], c_re[dr], c_im[dr],
                  n_ctx, reverse=bool(dr)) for dr in range(2)]
    o_g = s5_glu(ys[0], ys[1], u_g, d_skip, w_glu)
    o = o_g.reshape(b_ // bt, n_all, bt, d).transpose(0, 2, 1, 3).reshape(b_, n_all, d)
    return o[:, n_ctx:], (o[:, :n_ctx] if need_ctx else None)


HG_SCAN_TOKENS = 256
HG_SUB = 32
HG_STEP_HEADS = 4


def _hgrn_scan_kernel(q_ref, f_ref, v_ref, lb_ref, tri_ref, o_ref, q_s, k_s, cum_s, v_s, st, *, reverse):
    tokens = q_ref.shape[1]

    @pl.when(pl.program_id(2) == 0)
    def _():
        st[...] = jnp.zeros_like(st)

    f = f_ref[0]
    log_f = jnp.logaddexp(lb_ref[0:1, :], lb_ref[1:2, :] + jax.nn.log_sigmoid(f))
    k_all = lb_ref[2:3, :] * jax.nn.sigmoid(-f)
    q_all = jax.nn.silu(q_ref[0])
    cum_all = jnp.dot(tri_ref[...], log_f, precision=lax.Precision.HIGHEST, preferred_element_type=F32)
    for hd in range(HG_STEP_HEADS):
        lanes = slice(hd * HG_KEY, (hd + 1) * HG_KEY)
        k_s[hd] = k_all[:, lanes]
        q_s[hd] = q_all[:, lanes]
        cum_s[hd] = cum_all[:, lanes]
        v_s[hd] = v_ref[0, :, lanes]
    tpos = lax.broadcasted_iota(jnp.int32, (HG_SUB, HG_KEY), 0)
    nsub = tokens // HG_SUB
    ones = jnp.ones((HG_KEY, HG_VAL), BF16)

    def head_block(base, hd):
        lanes = slice(hd * HG_KEY, (hd + 1) * HG_KEY)
        qc = q_s[hd, pl.ds(base, HG_SUB), :]
        cumc = cum_s[hd, pl.ds(base, HG_SUB), :]

        def column(s, o):
            ks = k_s[hd, pl.ds(base + s, 1), :]
            cs = cum_s[hd, pl.ds(base + s, 1), :]
            vs = v_s[hd, pl.ds(base + s, 1), :]
            seen = (tpos <= s) if reverse else (tpos >= s)
            decay = jnp.exp(jnp.where(seen, cumc - cs, -jnp.inf))
            att = jnp.dot((qc * ks * decay).astype(BF16), ones, preferred_element_type=F32)
            return o + att * vs

        o = lax.fori_loop(0, HG_SUB, column, jnp.zeros(qc.shape, F32), unroll=True)
        s_t = st[hd]
        o = o + lax.dot_general((qc * jnp.exp(cumc)).astype(BF16), s_t.astype(BF16),
                                (((1,), (1,)), ((), ())), preferred_element_type=F32)
        last = cum_s[hd, pl.ds(base + (0 if reverse else HG_SUB - 1), 1), :]
        kh = (k_s[hd, pl.ds(base, HG_SUB), :] * jnp.exp(last - cumc)).astype(BF16)
        vc = v_s[hd, pl.ds(base, HG_SUB), :].astype(BF16)
        st[hd] = s_t * jnp.exp(last) + lax.dot_general(vc, kh, (((0,), (0,)), ((), ())),
                                                       preferred_element_type=F32)
        o_ref[0, pl.ds(base, HG_SUB), lanes] = o

    def block(i, carry):
        c = (nsub - 1 - i) if reverse else i
        base = pl.multiple_of(c * HG_SUB, HG_SUB)
        for hd in range(HG_STEP_HEADS):
            head_block(base, hd)
        return carry

    lax.fori_loop(0, nsub, block, 0)


def hgrn_scan(proj, lb_rows, n_ctx, f_col, reverse):
    b_, n, _ = proj.shape
    t = HG_SCAN_TOKENS
    hk = HG_STEP_HEADS * HG_KEY
    steps = HG_HEADS // HG_STEP_HEADS
    n_chunks, ctx_chunks = n // t, n_ctx // t
    if reverse:
        chunk = lambda s: jnp.where(s < ctx_chunks, ctx_chunks - 1 - s, n_chunks - 1 - (s - ctx_chunks))
    else:
        chunk = lambda s: s
    pos = jnp.arange(t)
    same = (pos[:, None] // HG_SUB) == (pos[None, :] // HG_SUB)
    order = (pos[None, :] >= pos[:, None]) if reverse else (pos[None, :] <= pos[:, None])
    tri = (same & order).astype(F32)
    col = lambda off: pl.BlockSpec((1, t, hk), lambda i, h, s: (i, chunk(s), off * steps + h))
    return pl.pallas_call(
        functools.partial(_hgrn_scan_kernel, reverse=reverse),
        grid=(b_, steps, n_chunks),
        in_specs=[col(0), col(f_col), col(3),
                  pl.BlockSpec((3, hk), lambda i, h, s: (0, h)),
                  pl.BlockSpec((t, t), lambda i, h, s: (0, 0))],
        out_specs=pl.BlockSpec((1, t, hk), lambda i, h, s: (i, chunk(s), h)),
        out_shape=jax.ShapeDtypeStruct((b_, n, HG_HEADS * HG_VAL), F32),
        scratch_shapes=[pltpu.VMEM((HG_STEP_HEADS, t, HG_KEY), F32)] * 4
                       + [pltpu.VMEM((HG_STEP_HEADS, HG_VAL, HG_KEY), F32)],
        compiler_params=pltpu.CompilerParams(dimension_semantics=("arbitrary", "arbitrary", "arbitrary"),
                                             vmem_limit_bytes=MOSAIC_VMEM_LIMIT),
        name="hgrn_scan_bwd" if reverse else "hgrn_scan_fwd",
    )(proj, proj, proj, lb_rows, tri)


def _hgrn_out_kernel(of_ref, ob_ref, gate_ref, g_ref, w_ref, o_ref):
    o = of_ref[0] + ob_ref[0]
    parts = []
    for h in range(HG_HEADS):
        oh = o[:, h * HG_VAL:(h + 1) * HG_VAL]
        parts.append(oh * lax.rsqrt(jnp.mean(oh * oh, axis=-1, keepdims=True) + RMS_EPS))
    y = jnp.concatenate(parts, axis=1) * g_ref[...] * jax.nn.silu(gate_ref[0])
    o_ref[0] = jnp.dot(y.astype(BF16), w_ref[...], preferred_element_type=F32)


def hgrn_out(o_fw, o_bw, proj, norm_g, w_out, block_rows=256):
    b_, n, d = o_fw.shape
    assert n % block_rows == 0
    row = pl.BlockSpec((1, block_rows, d), lambda i, j: (i, j, 0))
    return pl.pallas_call(
        _hgrn_out_kernel,
        grid=(b_, n // block_rows),
        in_specs=[row, row, pl.BlockSpec((1, block_rows, d), lambda i, j: (i, j, 4)),
                  pl.BlockSpec((1, d), lambda i, j: (0, 0)), pl.BlockSpec(w_out.shape, lambda i, j: (0, 0))],
        out_specs=row,
        out_shape=jax.ShapeDtypeStruct((b_, n, d), F32),
        compiler_params=pltpu.CompilerParams(vmem_limit_bytes=MOSAIC_VMEM_LIMIT),
        name="hgrn_out",
    )(o_fw, o_bw, proj, norm_g.reshape(1, d), w_out.astype(BF16))


def hgrn2_mixer(u_ctx, u_lat, w_in, w_out, norm_g, lb, need_ctx):
    n_ctx = u_ctx.shape[1]
    proj = linear(jnp.concatenate([u_ctx, u_lat], axis=1), w_in)
    lb_rows = jnp.stack([jnp.log(lb), jnp.log1p(-lb), 1.0 - lb])
    o_fw = hgrn_scan(proj, lb_rows, n_ctx, 1, reverse=False)
    o_bw = hgrn_scan(proj, lb_rows, n_ctx, 2, reverse=True)
    o = hgrn_out(o_fw, o_bw, proj, norm_g, w_out)
    return o[:, n_ctx:], (o[:, :n_ctx] if need_ctx else None)


PEER_SLOTS = PEER_HEADS * PEER_TOPK
PEER_SEL_TOKENS = 256
PEER_MIX_TOKENS = 256
PEER_ROW_BUFFERS = 4


def _topk_axis0(cur, k, payload=None):
    rows = cur.shape[0]
    iota = lax.broadcasted_iota(jnp.int32, cur.shape, 0)
    vals, picks = [], []
    for _ in range(k):
        m = jnp.max(cur, axis=0, keepdims=True)
        pos = jnp.min(jnp.where(cur == m, iota, rows), axis=0, keepdims=True)
        hit = iota == pos
        vals.append(m)
        if payload is None:
            picks.append(pos)
        else:
            picks.append(jnp.sum(jnp.where(hit, payload, 0), axis=0, keepdims=True))
        cur = jnp.where(hit, -jnp.inf, cur)
    return jnp.concatenate(vals, axis=0), jnp.concatenate(picks, axis=0)


def _peer_select_kernel(h_ref, sc_ref, sh_ref, wq_ref, keys_ref, idx_ref, g_ref):
    half = PEER_QDIM // 2
    x = h_ref[0] * (1.0 + sc_ref[0]) + sh_ref[0]
    q = jnp.dot(x.astype(BF16), wq_ref[...], preferred_element_type=F32)
    tokens = x.shape[0]
    for c0 in range(0, tokens, LANES):
        idx_rows, g_rows = [], []
        for hd in range(PEER_HEADS):
            tops = []
            for c in range(2):
                lo = (hd * 2 + c) * half
                qhc = q[c0:c0 + LANES, lo:lo + half].astype(BF16)
                s_t = lax.dot_general(keys_ref[c], qhc, (((1,), (1,)), ((), ())),
                                      preferred_element_type=F32)
                tops.append(_topk_axis0(s_t, PEER_TOPK))
            (s1, i1), (s2, i2) = tops
            width = [PEER_TOPK // (a + 1) for a in range(PEER_TOPK)]
            pad = -sum(width) % 8
            cand_s = jnp.concatenate([s1[a:a + 1] + s2[:width[a]] for a in range(PEER_TOPK)]
                                     + [jnp.full((pad, LANES), -jnp.inf, F32)], axis=0)
            cand_i = jnp.concatenate([i1[a:a + 1] * PEER_NKEYS + i2[:width[a]] for a in range(PEER_TOPK)]
                                     + [jnp.zeros((pad, LANES), jnp.int32)], axis=0)
            top_s, top_i = _topk_axis0(cand_s, PEER_TOPK, payload=cand_i)
            e = jnp.exp(top_s - top_s[0:1])
            g_rows.append(e / jnp.sum(e, axis=0, keepdims=True))
            idx_rows.append(top_i)
        idx_ref[0, c0:c0 + LANES, :] = jnp.concatenate(idx_rows, axis=0).T
        g_ref[0, c0:c0 + LANES, :] = jnp.concatenate(g_rows, axis=0).T


def peer_select(h, sc, sh, w_q, sub_keys):
    b_, n, d = h.shape
    tb = min(PEER_SEL_TOKENS, n)
    mod_map = (lambda i, j: (i, 0, 0)) if sc.shape[0] == b_ else (lambda i, j: (0, 0, 0))
    return pl.pallas_call(
        _peer_select_kernel,
        grid=(b_, n // tb),
        in_specs=[pl.BlockSpec((1, tb, d), lambda i, j: (i, j, 0)),
                  pl.BlockSpec((1, 1, d), mod_map),
                  pl.BlockSpec((1, 1, d), mod_map),
                  pl.BlockSpec(w_q.shape, lambda i, j: (0, 0)),
                  pl.BlockSpec(sub_keys.shape, lambda i, j: (0, 0, 0))],
        out_specs=[pl.BlockSpec((1, tb, PEER_SLOTS), lambda i, j: (i, j, 0)),
                   pl.BlockSpec((1, tb, PEER_SLOTS), lambda i, j: (i, j, 0))],
        out_shape=[jax.ShapeDtypeStruct((b_, n, PEER_SLOTS), jnp.int32),
                   jax.ShapeDtypeStruct((b_, n, PEER_SLOTS), F32)],
        compiler_params=pltpu.CompilerParams(vmem_limit_bytes=48 * 1024 * 1024),
        name="peer_select",
    )(h, sc, sh, w_q.astype(BF16), sub_keys.astype(BF16))


def peer_table(u_tab, v_tab):
    bits = lambda t: lax.bitcast_convert_type(t.astype(BF16), jnp.uint16).astype(jnp.uint32)
    return ((bits(u_tab) << 16) | bits(v_tab))[:, None, :]


def _peer_mix_kernel(idx_hbm, h_ref, sc_ref, sh_ref, g_ref, tab_hbm, out_ref,
                     idx_smem, x_scr, rows, row_sem, idx_sem):
    tokens, d = x_scr.shape
    nbuf = PEER_ROW_BUFFERS
    ahead = nbuf - 1
    blk = pl.program_id(0) * pl.num_programs(1) + pl.program_id(1)
    per_blk = tokens * PEER_SLOTS
    idx_copy = pltpu.make_async_copy(idx_hbm.at[pl.ds(pl.multiple_of(blk * per_blk, per_blk), per_blk)],
                                     idx_smem, idx_sem)
    idx_copy.start()
    x_scr[...] = h_ref[0] * (1.0 + sc_ref[0]) + sh_ref[0]
    idx_copy.wait()

    def fetch(t, slot):
        for j in range(PEER_SLOTS):
            pltpu.make_async_copy(tab_hbm.at[idx_smem[t * PEER_SLOTS + j]],
                                  rows.at[slot, pl.ds(j, 1), :], row_sem.at[slot]).start(priority=j % 2)

    def wait_rows(slot):
        pltpu.make_async_copy(rows.at[slot], rows.at[slot], row_sem.at[slot]).wait()

    eye = (lax.broadcasted_iota(jnp.int32, (PEER_SLOTS, PEER_SLOTS), 0)
           == lax.broadcasted_iota(jnp.int32, (PEER_SLOTS, PEER_SLOTS), 1))

    def combine(t, slot):
        x_row = x_scr[pl.ds(t, 1), :]
        words = rows[slot]
        u_rows = lax.bitcast_convert_type(words & jnp.uint32(0xFFFF0000), F32)
        v_rows = lax.bitcast_convert_type(words << 16, F32)
        act = jnp.sum(u_rows * x_row, axis=1, keepdims=True)
        g_col = jnp.sum(jnp.where(eye, g_ref[0, pl.ds(t, 1), :], 0.0), axis=1, keepdims=True)
        w = g_col * _gelu_tanh(act)
        out_ref[0, pl.ds(t, 1), :] = jnp.sum(w * v_rows, axis=0, keepdims=True)

    def token(t, slot, prefetch):
        if prefetch:
            fetch(t + ahead, (slot + ahead) % nbuf)
        wait_rows(slot)
        combine(t, slot)

    for t0 in range(ahead):
        fetch(t0, t0)

    def group(i, carry):
        for slot in range(nbuf):
            token(i * nbuf + slot, slot, True)
        return carry

    lax.fori_loop(0, tokens // nbuf - 1, group, 0)
    for slot in range(nbuf):
        token(tokens - nbuf + slot, slot, slot + ahead < nbuf)


def peer_mix(h, sc, sh, idx, g, uv_tab):
    b_, n, d = h.shape
    tb = min(PEER_MIX_TOKENS, n)
    assert n % tb == 0 and tb % PEER_ROW_BUFFERS == 0
    mod_map = (lambda i, j: (i, 0, 0)) if sc.shape[0] == b_ else (lambda i, j: (0, 0, 0))
    return pl.pallas_call(
        _peer_mix_kernel,
        grid=(b_, n // tb),
        in_specs=[pl.BlockSpec(memory_space=pl.ANY),
                  pl.BlockSpec((1, tb, d), lambda i, j: (i, j, 0)),
                  pl.BlockSpec((1, 1, d), mod_map),
                  pl.BlockSpec((1, 1, d), mod_map),
                  pl.BlockSpec((1, tb, PEER_SLOTS), lambda i, j: (i, j, 0)),
                  pl.BlockSpec(memory_space=pl.ANY)],
        out_specs=pl.BlockSpec((1, tb, d), lambda i, j: (i, j, 0)),
        out_shape=jax.ShapeDtypeStruct(h.shape, F32),
        scratch_shapes=[pltpu.SMEM((tb * PEER_SLOTS,), jnp.int32),
                        pltpu.VMEM((tb, d), F32),
                        pltpu.VMEM((PEER_ROW_BUFFERS, PEER_SLOTS, d), jnp.uint32),
                        pltpu.SemaphoreType.DMA((PEER_ROW_BUFFERS,)),
                        pltpu.SemaphoreType.DMA(())],
        compiler_params=pltpu.CompilerParams(vmem_limit_bytes=32 * 1024 * 1024),
        name="peer_mix",
    )(idx.reshape(-1), h, sc, sh, g, uv_tab)


def peer_ffn(h, sc, sh, w_q, sub_keys, uv_tab):
    idx, g = peer_select(h, sc, sh, w_q, sub_keys)
    return peer_mix(h, sc, sh, idx, g, uv_tab)


def kernel(x, c, ctx, c_ctx, ada_w, ada_b, ln_g, ln_b, da_w_in, da_w_out, da_lam_q, da_lam_k, da_subln,
           s5_lam_re, s5_lam_im, s5_log_dt, s5_b_re, s5_b_im, s5_c_re, s5_c_im, s5_d, s5_w_glu,
           hg_w_in, hg_w_out, hg_norm, hg_lb, peer_wq, peer_keys, peer_u, peer_v):
    L = x.shape[1]
    cos, sin = axial_rope(L, DA_HEAD_DIM)
    s_c = jax.nn.silu(c)
    s_ctx = jax.nn.silu(c_ctx)
    lb_soft = jax.nn.softmax(hg_lb.astype(F32), axis=0)
    lb_all = jnp.cumsum(lb_soft, axis=0) - lb_soft[0]
    h, hc = x, ctx
    for i in range(DEPTH):
        kind, slot = LAYER_TYPES[i], i // N_MIXERS
        need_ctx = i < DEPTH - 1
        mod = (s_c @ ada_w[i] + ada_b[i])[:, None, :]
        mod_c = s_ctx @ ada_w[i] + ada_b[i]
        sh1, sc1, g1, sh2, sc2, g2 = jnp.split(mod, 6, axis=-1)
        csh1, csc1, cg1, csh2, csc2, cg2 = jnp.split(mod_c, 6, axis=-1)
        u = h * (1.0 + sc1) + sh1
        uc = hc * (1.0 + csc1) + csh1
        if kind == 0:
            lam_init = 0.8 - 0.6 * math.exp(-0.3 * i)
            o, oc = diff_attention(uc, u, da_w_in[slot], da_w_out[slot], da_lam_q[slot], da_lam_k[slot],
                                   da_subln[slot], lam_init, cos, sin, need_ctx)
        elif kind == 1:
            o, oc = s5_mixer(uc, u, s5_lam_re[slot], s5_lam_im[slot], s5_log_dt[slot], s5_b_re[slot],
                             s5_b_im[slot], s5_c_re[slot], s5_c_im[slot], s5_d[slot], s5_w_glu[slot], need_ctx)
        else:
            o, oc = hgrn2_mixer(uc, u, hg_w_in[slot], hg_w_out[slot], hg_norm[slot], lb_all[i], need_ctx)
        h = residual_layer_norm(h, o, g1, ln_g[i, 0], ln_b[i, 0])
        uv_tab = peer_table(peer_u[i], peer_v[i])
        f = peer_ffn(h, sc2, sh2, peer_wq[i], peer_keys[i], uv_tab)
        h = residual_layer_norm(h, f, g2, ln_g[i, 1], ln_b[i, 1])
        if need_ctx:
            hc = residual_layer_norm(hc, oc, cg1.reshape(1, 1, -1), ln_g[i, 0], ln_b[i, 0])
            fc = peer_ffn(hc, csc2.reshape(1, 1, -1), csh2.reshape(1, 1, -1), peer_wq[i], peer_keys[i], uv_tab)
            hc = residual_layer_norm(hc, fc, cg2.reshape(1, 1, -1), ln_g[i, 1], ln_b[i, 1])
    return h
```

```python
import functools
import math
import jax, jax.numpy as jnp
from jax import lax
import numpy as np
from jax.experimental import pallas as pl
from jax.experimental.pallas import tpu as pltpu

D_MODEL = 1024
BATCH = 32
SEQ = 2048
DEPTH = 4

CTX_LEN = 256
GRID_W = 64
N_MIXERS = 3
LAYER_TYPES = tuple(i % N_MIXERS for i in range(DEPTH))
N_ATTN = LAYER_TYPES.count(0)
N_S5 = LAYER_TYPES.count(1)
N_HG = LAYER_TYPES.count(2)

DA_HEADS = 8
DA_HEAD_DIM = 64
DA_V_DIM = 2 * DA_HEAD_DIM
Q_BLOCK = 128
ROPE_THETA = 10000.0
S5_GROUP = 16
S5_GROUPS = D_MODEL // S5_GROUP
S5_STATE = 64
HG_HEADS = 8
HG_KEY = D_MODEL // HG_HEADS
HG_VAL = D_MODEL // HG_HEADS
HG_CHUNK = 32
PEER_HEADS = 8
PEER_NKEYS = 128
PEER_EXPERTS = PEER_NKEYS * PEER_NKEYS
PEER_QDIM = 256
PEER_TOPK = 16
PEER_BLOCK = 128
LN_EPS = 1e-5
RMS_EPS = 1e-6
DN_ALPHA = (2 * DEPTH) ** 0.25
DN_BETA = (8 * DEPTH) ** -0.25

F32 = jnp.float32
BF16 = jnp.bfloat16
LANES = 128
MOSAIC_VMEM_LIMIT = 48 * 1024 * 1024


def _gelu_tanh(x):
    return 0.5 * x * (1.0 + jnp.tanh(math.sqrt(2.0 / math.pi) * (x + 0.044715 * (x * x * x))))


def _res_ln_kernel(h_ref, o_ref, gate_ref, g_ref, b_ref, out_ref):
    y = DN_ALPHA * h_ref[0] + gate_ref[0] * o_ref[0]
    mu = jnp.mean(y, -1, keepdims=True)
    yc = y - mu
    var = jnp.mean(yc * yc, -1, keepdims=True)
    out_ref[0] = yc * lax.rsqrt(var + LN_EPS) * g_ref[...] + b_ref[...]


def residual_layer_norm(h, o, gate, g, b, block_n=512):
    b_, n, d = h.shape
    bn = min(block_n, n)
    per_batch_gate = gate.shape[0] == b_
    gate_map = (lambda i, j: (i, 0, 0)) if per_batch_gate else (lambda i, j: (0, 0, 0))
    return pl.pallas_call(
        _res_ln_kernel,
        grid=(b_, n // bn),
        in_specs=[pl.BlockSpec((1, bn, d), lambda i, j: (i, j, 0)),
                  pl.BlockSpec((1, bn, d), lambda i, j: (i, j, 0)),
                  pl.BlockSpec((1, 1, d), gate_map),
                  pl.BlockSpec((1, d), lambda i, j: (0, 0)),
                  pl.BlockSpec((1, d), lambda i, j: (0, 0))],
        out_specs=pl.BlockSpec((1, bn, d), lambda i, j: (i, j, 0)),
        out_shape=jax.ShapeDtypeStruct(h.shape, h.dtype),
        name="residual_layer_norm",
    )(h, o, gate, g.reshape(1, d), b.reshape(1, d))


def axial_rope(length, dim):
    rows = length // GRID_W
    row = jnp.repeat(jnp.arange(rows, dtype=F32), GRID_W)
    col = jnp.tile(jnp.arange(GRID_W, dtype=F32), rows)
    n_freq = dim // 4
    inv = ROPE_THETA ** (-jnp.arange(n_freq, dtype=F32) / n_freq)
    ang = jnp.concatenate([row[:, None] * inv, col[:, None] * inv], axis=-1)
    return jnp.cos(ang), jnp.sin(ang)


def _linear_kernel(x_ref, w_ref, o_ref):
    o_ref[0] = jnp.dot(x_ref[0].astype(BF16), w_ref[...], preferred_element_type=F32).astype(o_ref.dtype)


def linear(x, w, out_dtype=F32, block_rows=256, block_cols=1024):
    b_, n, kdim = x.shape
    ncols = w.shape[1]
    br, bc = min(block_rows, n), min(block_cols, ncols)
    assert n % br == 0 and ncols % bc == 0
    return pl.pallas_call(
        _linear_kernel,
        grid=(b_, n // br, ncols // bc),
        in_specs=[pl.BlockSpec((1, br, kdim), lambda i, j, c: (i, j, 0)),
                  pl.BlockSpec((kdim, bc), lambda i, j, c: (0, c))],
        out_specs=pl.BlockSpec((1, br, bc), lambda i, j, c: (i, j, c)),
        out_shape=jax.ShapeDtypeStruct((b_, n, ncols), out_dtype),
        compiler_params=pltpu.CompilerParams(vmem_limit_bytes=MOSAIC_VMEM_LIMIT),
        name="linear",
    )(x, w.astype(BF16))


DA_Q_TOKENS = 256
DA_PROJ_TOKENS = 256


def _da_qkv_kernel(x_ref, w_ref, wsw_ref, cos_ref, sin_ref, q_ref, k_ref, v_ref):
    d = q_ref.shape[-1]
    x = x_ref[0].astype(BF16)
    reps = 2 * d // cos_ref.shape[-1]
    c = jnp.concatenate([cos_ref[...]] * reps, axis=1)
    s = jnp.concatenate([sin_ref[...]] * reps, axis=1)
    qk = (jnp.dot(x, w_ref[:, :2 * d], preferred_element_type=F32) * c
          + jnp.dot(x, wsw_ref[...], preferred_element_type=F32) * s)
    q_ref[0] = (qk[:, :d] * DA_HEAD_DIM ** -0.5).astype(BF16)
    k_ref[0] = qk[:, d:].astype(BF16)
    v_ref[0] = jnp.dot(x, w_ref[:, 2 * d:], preferred_element_type=F32).astype(BF16)


def da_qkv(u_all, w_in, cos_t, sin_t):
    b_, n, d = u_all.shape
    tb = DA_PROJ_TOKENS
    swap = jnp.arange(2 * d) ^ 1
    w = w_in.astype(BF16)
    w_sw = w[:, :2 * d][:, swap]
    row = pl.BlockSpec((1, tb, d), lambda i, j: (i, j, 0))
    tab = pl.BlockSpec((tb, cos_t.shape[1]), lambda i, j: (j, 0))
    return pl.pallas_call(
        _da_qkv_kernel,
        grid=(b_, n // tb),
        in_specs=[row, pl.BlockSpec(w.shape, lambda i, j: (0, 0)), pl.BlockSpec(w_sw.shape, lambda i, j: (0, 0)),
                  tab, tab],
        out_specs=[row, row, row],
        out_shape=[jax.ShapeDtypeStruct((b_, n, d), BF16)] * 3,
        compiler_params=pltpu.CompilerParams(vmem_limit_bytes=MOSAIC_VMEM_LIMIT),
        name="da_qkv",
    )(u_all, w, w_sw, cos_t, sin_t)


def _da_attn_kernel(lam_ref, q_ref, k_ref, v_ref, g_ref, o_ref, *, post_scale):
    q, k, v = q_ref[0], k_ref[0], v_ref[0]
    lane = lax.broadcasted_iota(jnp.int32, q.shape, 1)
    zero = jnp.zeros_like(q)
    contract_last = (((1,), (1,)), ((), ()))

    def softmax_parts(qm):
        s = lax.dot_general(qm, k, contract_last, preferred_element_type=F32)
        e = jnp.exp(s - jnp.max(s, axis=-1, keepdims=True))
        return e, jnp.sum(e, axis=-1, keepdims=True)

    e0, z0 = softmax_parts(jnp.where(lane < DA_HEAD_DIM, q, zero))
    e1, z1 = softmax_parts(jnp.where(lane >= DA_HEAD_DIM, q, zero))
    a = e0 / z0 - lam_ref[0] * (e1 / z1)
    o = jnp.dot(a.astype(BF16), v, preferred_element_type=F32)
    o = o * lax.rsqrt(jnp.mean(o * o, axis=-1, keepdims=True) + RMS_EPS)
    o_ref[0] = (o * g_ref[...] * post_scale).astype(o_ref.dtype)


def da_attend(q, k, v, lam, subln_g, post_scale, q_start, n_q, n_k):
    b_, _, d = q.shape
    hd = d // DA_HEADS
    tq = DA_Q_TOKENS
    q0 = q_start // tq
    return pl.pallas_call(
        functools.partial(_da_attn_kernel, post_scale=post_scale),
        grid=(b_, DA_HEADS, n_q // tq),
        in_specs=[pl.BlockSpec(memory_space=pltpu.SMEM),
                  pl.BlockSpec((1, tq, hd), lambda i, h, j: (i, q0 + j, h)),
                  pl.BlockSpec((1, n_k, hd), lambda i, h, j: (i, 0, h)),
                  pl.BlockSpec((1, n_k, hd), lambda i, h, j: (i, 0, h)),
                  pl.BlockSpec((1, hd), lambda i, h, j: (0, 0))],
        out_specs=pl.BlockSpec((1, tq, hd), lambda i, h, j: (i, j, h)),
        out_shape=jax.ShapeDtypeStruct((b_, n_q, d), BF16),
        compiler_params=pltpu.CompilerParams(vmem_limit_bytes=MOSAIC_VMEM_LIMIT),
        name="da_attend",
    )(lam.reshape(1), q, k, v, subln_g.reshape(1, hd))


def diff_attention(u_ctx, u_lat, w_in, w_out, lam_q, lam_k, subln_g, lam_init, cos, sin, need_ctx):
    b_, n_ctx, d = u_ctx.shape
    n_lat = u_lat.shape[1]
    lanes_cos = jnp.tile(jnp.repeat(cos, 2, axis=1), (1, 2))
    lanes_sin = jnp.tile(jnp.stack([-sin, sin], axis=-1).reshape(n_lat, -1), (1, 2))
    cos_t = jnp.concatenate([jnp.ones((n_ctx, lanes_cos.shape[1]), F32), lanes_cos], axis=0)
    sin_t = jnp.concatenate([jnp.zeros((n_ctx, lanes_sin.shape[1]), F32), lanes_sin], axis=0)
    u_all = jnp.concatenate([u_ctx, u_lat], axis=1)
    q, k, v = da_qkv(u_all, w_in, cos_t, sin_t)
    lq, lk = lam_q.astype(F32), lam_k.astype(F32)
    lam = jnp.exp(jnp.sum(lq[0] * lk[0])) - jnp.exp(jnp.sum(lq[1] * lk[1])) + lam_init
    post = 1.0 - lam_init
    o_lat = linear(da_attend(q, k, v, lam, subln_g, post, n_ctx, n_lat, n_ctx + n_lat), w_out)
    o_ctx = linear(da_attend(q, k, v, lam, subln_g, post, 0, n_ctx, n_ctx), w_out) if need_ctx else None
    return o_lat, o_ctx


def s5_discretize(lam_re, lam_im, log_dt, b_re, b_im):
    lam_re, lam_im = lam_re.astype(F32), lam_im.astype(F32)
    b_re, b_im = b_re.astype(F32), b_im.astype(F32)
    dt = jnp.exp(log_dt.astype(F32))[:, None]
    mag = jnp.exp(lam_re * dt)
    abar_re, abar_im = mag * jnp.cos(lam_im * dt), mag * jnp.sin(lam_im * dt)
    nr, ni = abar_re - 1.0, abar_im
    den = lam_re * lam_re + lam_im * lam_im
    k_re = (nr * lam_re + ni * lam_im) / den
    k_im = (ni * lam_re - nr * lam_im) / den
    bb_re = k_re[..., None] * b_re - k_im[..., None] * b_im
    bb_im = k_re[..., None] * b_im + k_im[..., None] * b_re
    return abar_re, abar_im, bb_re, bb_im


S5_BATCH_TILE = 8
S5_SCAN_TOKENS = 128
S5_PANEL_GROUPS = 16
S5_PANELS = S5_GROUPS // S5_PANEL_GROUPS


def _s5_panels(bb, c):
    g, p, m = bb.shape
    pg = S5_PANEL_GROUPS
    eye = jnp.eye(pg, dtype=bb.dtype)
    w_in = jnp.einsum('qgpm,gh->qgmhp', bb.reshape(g // pg, pg, p, m), eye).reshape(g // pg, pg * m, pg * p)
    w_out = jnp.einsum('qgmp,gh->qgphm', c.reshape(g // pg, pg, m, p), eye).reshape(g // pg, pg * p, pg * m)
    return w_in.astype(BF16), w_out.astype(BF16)


def _s5_scan_kernel(u_ref, wb_re_ref, wb_im_ref, a_re_ref, a_im_ref, c_re_ref, c_im_ref, y_ref,
                    x_re, x_im, state, *, reverse):
    rows = u_ref.shape[1]
    tokens = rows // S5_BATCH_TILE
    cin = S5_PANEL_GROUPS * S5_GROUP

    @pl.when(pl.program_id(1) == 0)
    def _():
        state[...] = jnp.zeros_like(state)

    for p in range(S5_PANELS):
        ub = u_ref[0, :, p * cin:(p + 1) * cin].astype(BF16)
        x_re[...] = jnp.dot(ub, wb_re_ref[p], preferred_element_type=F32)
        x_im[...] = jnp.dot(ub, wb_im_ref[p], preferred_element_type=F32)
        a_re, a_im = a_re_ref[p], a_im_ref[p]

        def step(i, carry):
            xr, xi = carry
            t = (tokens - 1 - i) if reverse else i
            r0 = pl.multiple_of(t * S5_BATCH_TILE, S5_BATCH_TILE)
            nr = a_re * xr - a_im * xi + x_re[pl.ds(r0, S5_BATCH_TILE), :]
            ni = a_re * xi + a_im * xr + x_im[pl.ds(r0, S5_BATCH_TILE), :]
            x_re[pl.ds(r0, S5_BATCH_TILE), :] = nr
            x_im[pl.ds(r0, S5_BATCH_TILE), :] = ni
            return nr, ni

        xr, xi = lax.fori_loop(0, tokens, step, (state[p, 0], state[p, 1]), unroll=2)
        state[p, 0] = xr
        state[p, 1] = xi
        y_ref[0, :, p * cin:(p + 1) * cin] = (
            jnp.dot(x_re[...].astype(BF16), c_re_ref[p], preferred_element_type=F32)
            - jnp.dot(x_im[...].astype(BF16), c_im_ref[p], preferred_element_type=F32))


def s5_scan(u_g, lam_re, lam_im, log_dt, b_re, b_im, c_re, c_im, n_ctx, reverse):
    nbg, rows_total, d = u_g.shape
    abar_re, abar_im, bb_re, bb_im = s5_discretize(lam_re, lam_im, log_dt, b_re, b_im)
    wb_re, cp_re = _s5_panels(bb_re, c_re.astype(F32))
    wb_im, cp_im = _s5_panels(bb_im, c_im.astype(F32))
    states = S5_PANEL_GROUPS * S5_STATE
    tile = lambda a: jnp.broadcast_to(a.reshape(S5_PANELS, 1, states), (S5_PANELS, S5_BATCH_TILE, states))
    rows = S5_SCAN_TOKENS * S5_BATCH_TILE
    n_chunks = rows_total // rows
    ctx_chunks = n_ctx // S5_SCAN_TOKENS
    if reverse:
        chunk = lambda s: jnp.where(s < ctx_chunks, ctx_chunks - 1 - s, n_chunks - 1 - (s - ctx_chunks))
    else:
        chunk = lambda s: s
    full = lambda a: pl.BlockSpec(a.shape, lambda i, s: (0,) * a.ndim)
    a_re_t, a_im_t = tile(abar_re), tile(abar_im)
    return pl.pallas_call(
        functools.partial(_s5_scan_kernel, reverse=reverse),
        grid=(nbg, n_chunks),
        in_specs=[pl.BlockSpec((1, rows, d), lambda i, s: (i, chunk(s), 0)),
                  full(wb_re), full(wb_im), full(a_re_t), full(a_im_t), full(cp_re), full(cp_im)],
        out_specs=pl.BlockSpec((1, rows, d), lambda i, s: (i, chunk(s), 0)),
        out_shape=jax.ShapeDtypeStruct(u_g.shape, F32),
        scratch_shapes=[pltpu.VMEM((rows, states), F32), pltpu.VMEM((rows, states), F32),
                        pltpu.VMEM((S5_PANELS, 2, S5_BATCH_TILE, states), F32)],
        compiler_params=pltpu.CompilerParams(dimension_semantics=("arbitrary", "arbitrary"),
                                             vmem_limit_bytes=MOSAIC_VMEM_LIMIT),
        name="s5_scan_bwd" if reverse else "s5_scan_fwd",
    )(u_g, wb_re, wb_im, a_re_t, a_im_t, cp_re, cp_im)


def _s5_glu_kernel(yf_ref, yb_ref, u_ref, d_ref, w_ref, o_ref):
    d = o_ref.shape[-1]
    y = yf_ref[0] + yb_ref[0] + d_ref[...] * u_ref[0]
    r = jnp.dot(_gelu_tanh(y).astype(BF16), w_ref[...], preferred_element_type=F32)
    o_ref[0] = r[:, :d] * jax.nn.sigmoid(r[:, d:])


def s5_glu(y_fw, y_bw, u_g, d_skip, w_glu, block_rows=512):
    nbg, rows_total, d = u_g.shape
    blk = pl.BlockSpec((1, block_rows, d), lambda i, j: (i, j, 0))
    return pl.pallas_call(
        _s5_glu_kernel,
        grid=(nbg, rows_total // block_rows),
        in_specs=[blk, blk, blk,
                  pl.BlockSpec((1, d), lambda i, j: (0, 0)),
                  pl.BlockSpec(w_glu.shape, lambda i, j: (0, 0))],
        out_specs=blk,
        out_shape=jax.ShapeDtypeStruct(u_g.shape, F32),
        compiler_params=pltpu.CompilerParams(vmem_limit_bytes=MOSAIC_VMEM_LIMIT),
        name="s5_glu",
    )(y_fw, y_bw, u_g, d_skip.reshape(1, d), w_glu.astype(BF16))


def s5_mixer(u_ctx, u_lat, lam_re, lam_im, log_dt, b_re, b_im, c_re, c_im, d_skip, w_glu, need_ctx):
    b_, n_ctx, d = u_ctx.shape
    n_all = n_ctx + u_lat.shape[1]
    bt = S5_BATCH_TILE
    u_all = jnp.concatenate([u_ctx, u_lat], axis=1)
    u_g = u_all.reshape(b_ // bt, bt, n_all, d).transpose(0, 2, 1, 3).reshape(b_ // bt, n_all * bt, d)
    ys = [s5_scan(u_g, lam_re[dr], lam_im[dr], log_dt[dr], b_re[dr], b_im[dr], c_re[dr], c_im[dr],
                  n_ctx, reverse=bool(dr)) for dr in range(2)]
    o_g = s5_glu(ys[0], ys[1], u_g, d_skip, w_glu)
    o = o_g.reshape(b_ // bt, n_all, bt, d).transpose(0, 2, 1, 3).reshape(b_, n_all, d)
    return o[:, n_ctx:], (o[:, :n_ctx] if need_ctx else None)


HG_SCAN_TOKENS = 256
HG_SUB = 32
HG_STEP_HEADS = 4


def _hgrn_scan_kernel(q_ref, f_ref, v_ref, lb_ref, tri_ref, o_ref, q_s, k_s, cum_s, v_s, st, *, reverse):
    tokens = q_ref.shape[1]

    @pl.when(pl.program_id(2) == 0)
    def _():
        st[...] = jnp.zeros_like(st)

    f = f_ref[0]
    log_f = jnp.logaddexp(lb_ref[0:1, :], lb_ref[1:2, :] + jax.nn.log_sigmoid(f))
    k_all = lb_ref[2:3, :] * jax.nn.sigmoid(-f)
    q_all = jax.nn.silu(q_ref[0])
    cum_all = jnp.dot(tri_ref[...], log_f, precision=lax.Precision.HIGHEST, preferred_element_type=F32)
    for hd in range(HG_STEP_HEADS):
        lanes = slice(hd * HG_KEY, (hd + 1) * HG_KEY)
        k_s[hd] = k_all[:, lanes]
        q_s[hd] = q_all[:, lanes]
        cum_s[hd] = cum_all[:, lanes]
        v_s[hd] = v_ref[0, :, lanes]
    tpos = lax.broadcasted_iota(jnp.int32, (HG_SUB, HG_KEY), 0)
    nsub = tokens // HG_SUB
    ones = jnp.ones((HG_KEY, HG_VAL), BF16)

    def head_block(base, hd):
        lanes = slice(hd * HG_KEY, (hd + 1) * HG_KEY)
        qc = q_s[hd, pl.ds(base, HG_SUB), :]
        cumc = cum_s[hd, pl.ds(base, HG_SUB), :]

        def column(s, o):
            ks = k_s[hd, pl.ds(base + s, 1), :]
            cs = cum_s[hd, pl.ds(base + s, 1), :]
            vs = v_s[hd, pl.ds(base + s, 1), :]
            seen = (tpos <= s) if reverse else (tpos >= s)
            decay = jnp.exp(jnp.where(seen, cumc - cs, -jnp.inf))
            att = jnp.dot((qc * ks * decay).astype(BF16), ones, preferred_element_type=F32)
            return o + att * vs

        o = lax.fori_loop(0, HG_SUB, column, jnp.zeros(qc.shape, F32), unroll=True)
        s_t = st[hd]
        o = o + lax.dot_general((qc * jnp.exp(cumc)).astype(BF16), s_t.astype(BF16),
                                (((1,), (1,)), ((), ())), preferred_element_type=F32)
        last = cum_s[hd, pl.ds(base + (0 if reverse else HG_SUB - 1), 1), :]
        kh = (k_s[hd, pl.ds(base, HG_SUB), :] * jnp.exp(last - cumc)).astype(BF16)
        vc = v_s[hd, pl.ds(base, HG_SUB), :].astype(BF16)
        st[hd] = s_t * jnp.exp(last) + lax.dot_general(vc, kh, (((0,), (0,)), ((), ())),
                                                       preferred_element_type=F32)
        o_ref[0, pl.ds(base, HG_SUB), lanes] = o

    def block(i, carry):
        c = (nsub - 1 - i) if reverse else i
        base = pl.multiple_of(c * HG_SUB, HG_SUB)
        for hd in range(HG_STEP_HEADS):
            head_block(base, hd)
        return carry

    lax.fori_loop(0, nsub, block, 0)


def hgrn_scan(proj, lb_rows, n_ctx, f_col, reverse):
    b_, n, _ = proj.shape
    t = HG_SCAN_TOKENS
    hk = HG_STEP_HEADS * HG_KEY
    steps = HG_HEADS // HG_STEP_HEADS
    n_chunks, ctx_chunks = n // t, n_ctx // t
    if reverse:
        chunk = lambda s: jnp.where(s < ctx_chunks, ctx_chunks - 1 - s, n_chunks - 1 - (s - ctx_chunks))
    else:
        chunk = lambda s: s
    pos = jnp.arange(t)
    same = (pos[:, None] // HG_SUB) == (pos[None, :] // HG_SUB)
    order = (pos[None, :] >= pos[:, None]) if reverse else (pos[None, :] <= pos[:, None])
    tri = (same & order).astype(F32)
    col = lambda off: pl.BlockSpec((1, t, hk), lambda i, h, s: (i, chunk(s), off * steps + h))
    return pl.pallas_call(
        functools.partial(_hgrn_scan_kernel, reverse=reverse),
        grid=(b_, steps, n_chunks),
        in_specs=[col(0), col(f_col), col(3),
                  pl.BlockSpec((3, hk), lambda i, h, s: (0, h)),
                  pl.BlockSpec((t, t), lambda i, h, s: (0, 0))],
        out_specs=pl.BlockSpec((1, t, hk), lambda i, h, s: (i, chunk(s), h)),
        out_shape=jax.ShapeDtypeStruct((b_, n, HG_HEADS * HG_VAL), F32),
        scratch_shapes=[pltpu.VMEM((HG_STEP_HEADS, t, HG_KEY), F32)] * 4
                       + [pltpu.VMEM((HG_STEP_HEADS, HG_VAL, HG_KEY), F32)],
        compiler_params=pltpu.CompilerParams(dimension_semantics=("arbitrary", "arbitrary", "arbitrary"),
                                             vmem_limit_bytes=MOSAIC_VMEM_LIMIT),
        name="hgrn_scan_bwd" if reverse else "hgrn_scan_fwd",
    )(proj, proj, proj, lb_rows, tri)


def _hgrn_out_kernel(of_ref, ob_ref, gate_ref, g_ref, w_ref, o_ref):
    o = of_ref[0] + ob_ref[0]
    parts = []
    for h in range(HG_HEADS):
        oh = o[:, h * HG_VAL:(h + 1) * HG_VAL]
        parts.append(oh * lax.rsqrt(jnp.mean(oh * oh, axis=-1, keepdims=True) + RMS_EPS))
    y = jnp.concatenate(parts, axis=1) * g_ref[...] * jax.nn.silu(gate_ref[0])
    o_ref[0] = jnp.dot(y.astype(BF16), w_ref[...], preferred_element_type=F32)


def hgrn_out(o_fw, o_bw, proj, norm_g, w_out, block_rows=256):
    b_, n, d = o_fw.shape
    assert n % block_rows == 0
    row = pl.BlockSpec((1, block_rows, d), lambda i, j: (i, j, 0))
    return pl.pallas_call(
        _hgrn_out_kernel,
        grid=(b_, n // block_rows),
        in_specs=[row, row, pl.BlockSpec((1, block_rows, d), lambda i, j: (i, j, 4)),
                  pl.BlockSpec((1, d), lambda i, j: (0, 0)), pl.BlockSpec(w_out.shape, lambda i, j: (0, 0))],
        out_specs=row,
        out_shape=jax.ShapeDtypeStruct((b_, n, d), F32),
        compiler_params=pltpu.CompilerParams(vmem_limit_bytes=MOSAIC_VMEM_LIMIT),
        name="hgrn_out",
    )(o_fw, o_bw, proj, norm_g.reshape(1, d), w_out.astype(BF16))


def hgrn2_mixer(u_ctx, u_lat, w_in, w_out, norm_g, lb, need_ctx):
    n_ctx = u_ctx.shape[1]
    proj = linear(jnp.concatenate([u_ctx, u_lat], axis=1), w_in)
    lb_rows = jnp.stack([jnp.log(lb), jnp.log1p(-lb), 1.0 - lb])
    o_fw = hgrn_scan(proj, lb_rows, n_ctx, 1, reverse=False)
    o_bw = hgrn_scan(proj, lb_rows, n_ctx, 2, reverse=True)
    o = hgrn_out(o_fw, o_bw, proj, norm_g, w_out)
    return o[:, n_ctx:], (o[:, :n_ctx] if need_ctx else None)


PEER_SLOTS = PEER_HEADS * PEER_TOPK
PEER_SEL_TOKENS = 256
PEER_MIX_TOKENS = 256
PEER_ROW_BUFFERS = 4
PEER_QUEUE_PATTERN = (0, 0, 1, 0, 0, 1, 0, 1)


def _topk_axis0(cur, k, payload=None):
    rows = cur.shape[0]
    iota = lax.broadcasted_iota(jnp.int32, cur.shape, 0)
    vals, picks = [], []
    for _ in range(k):
        m = jnp.max(cur, axis=0, keepdims=True)
        pos = jnp.min(jnp.where(cur == m, iota, rows), axis=0, keepdims=True)
        hit = iota == pos
        vals.append(m)
        if payload is None:
            picks.append(pos)
        else:
            picks.append(jnp.sum(jnp.where(hit, payload, 0), axis=0, keepdims=True))
        cur = jnp.where(hit, -jnp.inf, cur)
    return jnp.concatenate(vals, axis=0), jnp.concatenate(picks, axis=0)


def _peer_select_kernel(h_ref, sc_ref, sh_ref, wq_ref, keys_ref, idx_ref, g_ref):
    half = PEER_QDIM // 2
    x = h_ref[0] * (1.0 + sc_ref[0]) + sh_ref[0]
    q = jnp.dot(x.astype(BF16), wq_ref[...], preferred_element_type=F32)
    tokens = x.shape[0]
    for c0 in range(0, tokens, LANES):
        idx_rows, g_rows = [], []
        for hd in range(PEER_HEADS):
            tops = []
            for c in range(2):
                lo = (hd * 2 + c) * half
                qhc = q[c0:c0 + LANES, lo:lo + half].astype(BF16)
                s_t = lax.dot_general(keys_ref[c], qhc, (((1,), (1,)), ((), ())),
                                      preferred_element_type=F32)
                tops.append(_topk_axis0(s_t, PEER_TOPK))
            (s1, i1), (s2, i2) = tops
            width = [PEER_TOPK // (a + 1) for a in range(PEER_TOPK)]
            pad = -sum(width) % 8
            cand_s = jnp.concatenate([s1[a:a + 1] + s2[:width[a]] for a in range(PEER_TOPK)]
                                     + [jnp.full((pad, LANES), -jnp.inf, F32)], axis=0)
            cand_i = jnp.concatenate([i1[a:a + 1] * PEER_NKEYS + i2[:width[a]] for a in range(PEER_TOPK)]
                                     + [jnp.zeros((pad, LANES), jnp.int32)], axis=0)
            top_s, top_i = _topk_axis0(cand_s, PEER_TOPK, payload=cand_i)
            e = jnp.exp(top_s - top_s[0:1])
            g_rows.append(e / jnp.sum(e, axis=0, keepdims=True))
            idx_rows.append(top_i)
        idx_ref[0, c0:c0 + LANES, :] = jnp.concatenate(idx_rows, axis=0).T
        g_ref[0, c0:c0 + LANES, :] = jnp.concatenate(g_rows, axis=0).T


def peer_select(h, sc, sh, w_q, sub_keys):
    b_, n, d = h.shape
    tb = min(PEER_SEL_TOKENS, n)
    mod_map = (lambda i, j: (i, 0, 0)) if sc.shape[0] == b_ else (lambda i, j: (0, 0, 0))
    return pl.pallas_call(
        _peer_select_kernel,
        grid=(b_, n // tb),
        in_specs=[pl.BlockSpec((1, tb, d), lambda i, j: (i, j, 0)),
                  pl.BlockSpec((1, 1, d), mod_map),
                  pl.BlockSpec((1, 1, d), mod_map),
                  pl.BlockSpec(w_q.shape, lambda i, j: (0, 0)),
                  pl.BlockSpec(sub_keys.shape, lambda i, j: (0, 0, 0))],
        out_specs=[pl.BlockSpec((1, tb, PEER_SLOTS), lambda i, j: (i, j, 0)),
                   pl.BlockSpec((1, tb, PEER_SLOTS), lambda i, j: (i, j, 0))],
        out_shape=[jax.ShapeDtypeStruct((b_, n, PEER_SLOTS), jnp.int32),
                   jax.ShapeDtypeStruct((b_, n, PEER_SLOTS), F32)],
        compiler_params=pltpu.CompilerParams(vmem_limit_bytes=48 * 1024 * 1024),
        name="peer_select",
    )(h, sc, sh, w_q.astype(BF16), sub_keys.astype(BF16))


def peer_table(u_tab, v_tab):
    bits = lambda t: lax.bitcast_convert_type(t.astype(BF16), jnp.uint16).astype(jnp.uint32)
    return ((bits(u_tab) << 16) | bits(v_tab))[:, None, :]


def _peer_mix_kernel(idx_hbm, h_ref, sc_ref, sh_ref, g_ref, tab_hbm, out_ref,
                     idx_smem, x_scr, rows, row_sem, idx_sem):
    tokens, d = x_scr.shape
    nbuf = PEER_ROW_BUFFERS
    ahead = nbuf - 1
    blk = pl.program_id(0) * pl.num_programs(1) + pl.program_id(1)
    per_blk = tokens * PEER_SLOTS
    idx_copy = pltpu.make_async_copy(idx_hbm.at[pl.ds(pl.multiple_of(blk * per_blk, per_blk), per_blk)],
                                     idx_smem, idx_sem)
    idx_copy.start()
    x_scr[...] = h_ref[0] * (1.0 + sc_ref[0]) + sh_ref[0]
    idx_copy.wait()

    def fetch(t, slot):
        for j in range(PEER_SLOTS):
            pltpu.make_async_copy(tab_hbm.at[idx_smem[t * PEER_SLOTS + j]], rows.at[slot, pl.ds(j, 1), :],
                                  row_sem.at[slot]).start(priority=PEER_QUEUE_PATTERN[j % len(PEER_QUEUE_PATTERN)])

    def wait_rows(slot):
        pltpu.make_async_copy(rows.at[slot], rows.at[slot], row_sem.at[slot]).wait()

    eye = (lax.broadcasted_iota(jnp.int32, (PEER_SLOTS, PEER_SLOTS), 0)
           == lax.broadcasted_iota(jnp.int32, (PEER_SLOTS, PEER_SLOTS), 1))

    def combine(t, slot):
        x_row = x_scr[pl.ds(t, 1), :]
        words = rows[slot]
        u_rows = lax.bitcast_convert_type(words & jnp.uint32(0xFFFF0000), F32)
        v_rows = lax.bitcast_convert_type(words << 16, F32)
        act = jnp.sum(u_rows * x_row, axis=1, keepdims=True)
        g_col = jnp.sum(jnp.where(eye, g_ref[0, pl.ds(t, 1), :], 0.0), axis=1, keepdims=True)
        w = g_col * _gelu_tanh(act)
        out_ref[0, pl.ds(t, 1), :] = jnp.sum(w * v_rows, axis=0, keepdims=True)

    def token(t, slot, prefetch):
        if prefetch:
            fetch(t + ahead, (slot + ahead) % nbuf)
        wait_rows(slot)
        combine(t, slot)

    for t0 in range(ahead):
        fetch(t0, t0)

    def group(i, carry):
        for slot in range(nbuf):
            token(i * nbuf + slot, slot, True)
        return carry

    lax.fori_loop(0, tokens // nbuf - 1, group, 0)
    for slot in range(nbuf):
        token(tokens - nbuf + slot, slot, slot + ahead < nbuf)


def peer_mix(h, sc, sh, idx, g, uv_tab):
    b_, n, d = h.shape
    tb = min(PEER_MIX_TOKENS, n)
    assert n % tb == 0 and tb % PEER_ROW_BUFFERS == 0
    mod_map = (lambda i, j: (i, 0, 0)) if sc.shape[0] == b_ else (lambda i, j: (0, 0, 0))
    return pl.pallas_call(
        _peer_mix_kernel,
        grid=(b_, n // tb),
        in_specs=[pl.BlockSpec(memory_space=pl.ANY),
                  pl.BlockSpec((1, tb, d), lambda i, j: (i, j, 0)),
                  pl.BlockSpec((1, 1, d), mod_map),
                  pl.BlockSpec((1, 1, d), mod_map),
                  pl.BlockSpec((1, tb, PEER_SLOTS), lambda i, j: (i, j, 0)),
                  pl.BlockSpec(memory_space=pl.ANY)],
        out_specs=pl.BlockSpec((1, tb, d), lambda i, j: (i, j, 0)),
        out_shape=jax.ShapeDtypeStruct(h.shape, F32),
        scratch_shapes=[pltpu.SMEM((tb * PEER_SLOTS,), jnp.int32),
                        pltpu.VMEM((tb, d), F32),
                        pltpu.VMEM((PEER_ROW_BUFFERS, PEER_SLOTS, d), jnp.uint32),
                        pltpu.SemaphoreType.DMA((PEER_ROW_BUFFERS,)),
                        pltpu.SemaphoreType.DMA(())],
        compiler_params=pltpu.CompilerParams(vmem_limit_bytes=32 * 1024 * 1024),
        name="peer_mix",
    )(idx.reshape(-1), h, sc, sh, g, uv_tab)


def peer_ffn(h, sc, sh, w_q, sub_keys, uv_tab):
    idx, g = peer_select(h, sc, sh, w_q, sub_keys)
    return peer_mix(h, sc, sh, idx, g, uv_tab)


def kernel(x, c, ctx, c_ctx, ada_w, ada_b, ln_g, ln_b, da_w_in, da_w_out, da_lam_q, da_lam_k, da_subln,
           s5_lam_re, s5_lam_im, s5_log_dt, s5_b_re, s5_b_im, s5_c_re, s5_c_im, s5_d, s5_w_glu,
           hg_w_in, hg_w_out, hg_norm, hg_lb, peer_wq, peer_keys, peer_u, peer_v):
    L = x.shape[1]
    cos, sin = axial_rope(L, DA_HEAD_DIM)
    s_c = jax.nn.silu(c)
    s_ctx = jax.nn.silu(c_ctx)
    lb_soft = jax.nn.softmax(hg_lb.astype(F32), axis=0)
    lb_all = jnp.cumsum(lb_soft, axis=0) - lb_soft[0]
    h, hc = x, ctx
    for i in range(DEPTH):
        kind, slot = LAYER_TYPES[i], i // N_MIXERS
        need_ctx = i < DEPTH - 1
        mod = (s_c @ ada_w[i] + ada_b[i])[:, None, :]
        mod_c = s_ctx @ ada_w[i] + ada_b[i]
        sh1, sc1, g1, sh2, sc2, g2 = jnp.split(mod, 6, axis=-1)
        csh1, csc1, cg1, csh2, csc2, cg2 = jnp.split(mod_c, 6, axis=-1)
        u = h * (1.0 + sc1) + sh1
        uc = hc * (1.0 + csc1) + csh1
        if kind == 0:
            lam_init = 0.8 - 0.6 * math.exp(-0.3 * i)
            o, oc = diff_attention(uc, u, da_w_in[slot], da_w_out[slot], da_lam_q[slot], da_lam_k[slot],
                                   da_subln[slot], lam_init, cos, sin, need_ctx)
        elif kind == 1:
            o, oc = s5_mixer(uc, u, s5_lam_re[slot], s5_lam_im[slot], s5_log_dt[slot], s5_b_re[slot],
                             s5_b_im[slot], s5_c_re[slot], s5_c_im[slot], s5_d[slot], s5_w_glu[slot], need_ctx)
        else:
            o, oc = hgrn2_mixer(uc, u, hg_w_in[slot], hg_w_out[slot], hg_norm[slot], lb_all[i], need_ctx)
        h = residual_layer_norm(h, o, g1, ln_g[i, 0], ln_b[i, 0])
        uv_tab = peer_table(peer_u[i], peer_v[i])
        f = peer_ffn(h, sc2, sh2, peer_wq[i], peer_keys[i], uv_tab)
        h = residual_layer_norm(h, f, g2, ln_g[i, 1], ln_b[i, 1])
        if need_ctx:
            hc = residual_layer_norm(hc, oc, cg1.reshape(1, 1, -1), ln_g[i, 0], ln_b[i, 0])
            fc = peer_ffn(hc, csc2.reshape(1, 1, -1), csh2.reshape(1, 1, -1), peer_wq[i], peer_keys[i], uv_tab)
            hc = residual_layer_norm(hc, fc, cg2.reshape(1, 1, -1), ln_g[i, 1], ln_b[i, 1])
    return h
```

```python
import functools
import math
import jax, jax.numpy as jnp
from jax import lax
import numpy as np
from jax.experimental import pallas as pl
from jax.experimental.pallas import tpu as pltpu

D_MODEL = 1024
BATCH = 32
SEQ = 2048
DEPTH = 4

CTX_LEN = 256
GRID_W = 64
N_MIXERS = 3
LAYER_TYPES = tuple(i % N_MIXERS for i in range(DEPTH))
N_ATTN = LAYER_TYPES.count(0)
N_S5 = LAYER_TYPES.count(1)
N_HG = LAYER_TYPES.count(2)

DA_HEADS = 8
DA_HEAD_DIM = 64
DA_V_DIM = 2 * DA_HEAD_DIM
Q_BLOCK = 128
ROPE_THETA = 10000.0
S5_GROUP = 16
S5_GROUPS = D_MODEL // S5_GROUP
S5_STATE = 64
HG_HEADS = 8
HG_KEY = D_MODEL // HG_HEADS
HG_VAL = D_MODEL // HG_HEADS
HG_CHUNK = 32
PEER_HEADS = 8
PEER_NKEYS = 128
PEER_EXPERTS = PEER_NKEYS * PEER_NKEYS
PEER_QDIM = 256
PEER_TOPK = 16
PEER_BLOCK = 128
LN_EPS = 1e-5
RMS_EPS = 1e-6
DN_ALPHA = (2 * DEPTH) ** 0.25
DN_BETA = (8 * DEPTH) ** -0.25

F32 = jnp.float32
BF16 = jnp.bfloat16
LANES = 128
MOSAIC_VMEM_LIMIT = 48 * 1024 * 1024


def _gelu_tanh(x):
    return 0.5 * x * (1.0 + jnp.tanh(math.sqrt(2.0 / math.pi) * (x + 0.044715 * (x * x * x))))


def _res_ln_kernel(h_ref, o_ref, gate_ref, g_ref, b_ref, out_ref):
    y = DN_ALPHA * h_ref[0] + gate_ref[0] * o_ref[0]
    mu = jnp.mean(y, -1, keepdims=True)
    yc = y - mu
    var = jnp.mean(yc * yc, -1, keepdims=True)
    out_ref[0] = yc * lax.rsqrt(var + LN_EPS) * g_ref[...] + b_ref[...]


def residual_layer_norm(h, o, gate, g, b, block_n=512):
    b_, n, d = h.shape
    bn = min(block_n, n)
    per_batch_gate = gate.shape[0] == b_
    gate_map = (lambda i, j: (i, 0, 0)) if per_batch_gate else (lambda i, j: (0, 0, 0))
    return pl.pallas_call(
        _res_ln_kernel,
        grid=(b_, n // bn),
        in_specs=[pl.BlockSpec((1, bn, d), lambda i, j: (i, j, 0)),
                  pl.BlockSpec((1, bn, d), lambda i, j: (i, j, 0)),
                  pl.BlockSpec((1, 1, d), gate_map),
                  pl.BlockSpec((1, d), lambda i, j: (0, 0)),
                  pl.BlockSpec((1, d), lambda i, j: (0, 0))],
        out_specs=pl.BlockSpec((1, bn, d), lambda i, j: (i, j, 0)),
        out_shape=jax.ShapeDtypeStruct(h.shape, h.dtype),
        name="residual_layer_norm",
    )(h, o, gate, g.reshape(1, d), b.reshape(1, d))


def axial_rope(length, dim):
    rows = length // GRID_W
    row = jnp.repeat(jnp.arange(rows, dtype=F32), GRID_W)
    col = jnp.tile(jnp.arange(GRID_W, dtype=F32), rows)
    n_freq = dim // 4
    inv = ROPE_THETA ** (-jnp.arange(n_freq, dtype=F32) / n_freq)
    ang = jnp.concatenate([row[:, None] * inv, col[:, None] * inv], axis=-1)
    return jnp.cos(ang), jnp.sin(ang)


def _linear_kernel(x_ref, w_ref, o_ref):
    o_ref[0] = jnp.dot(x_ref[0].astype(BF16), w_ref[...], preferred_element_type=F32).astype(o_ref.dtype)


def linear(x, w, out_dtype=F32, block_rows=256, block_cols=1024):
    b_, n, kdim = x.shape
    ncols = w.shape[1]
    br, bc = min(block_rows, n), min(block_cols, ncols)
    assert n % br == 0 and ncols % bc == 0
    return pl.pallas_call(
        _linear_kernel,
        grid=(b_, n // br, ncols // bc),
        in_specs=[pl.BlockSpec((1, br, kdim), lambda i, j, c: (i, j, 0)),
                  pl.BlockSpec((kdim, bc), lambda i, j, c: (0, c))],
        out_specs=pl.BlockSpec((1, br, bc), lambda i, j, c: (i, j, c)),
        out_shape=jax.ShapeDtypeStruct((b_, n, ncols), out_dtype),
        compiler_params=pltpu.CompilerParams(vmem_limit_bytes=MOSAIC_VMEM_LIMIT),
        name="linear",
    )(x, w.astype(BF16))


DA_Q_TOKENS = 256
DA_PROJ_TOKENS = 256


def _da_qkv_kernel(x_ref, w_ref, wsw_ref, cos_ref, sin_ref, q_ref, k_ref, v_ref):
    d = q_ref.shape[-1]
    x = x_ref[0].astype(BF16)
    reps = 2 * d // cos_ref.shape[-1]
    c = jnp.concatenate([cos_ref[...]] * reps, axis=1)
    s = jnp.concatenate([sin_ref[...]] * reps, axis=1)
    qk = (jnp.dot(x, w_ref[:, :2 * d], preferred_element_type=F32) * c
          + jnp.dot(x, wsw_ref[...], preferred_element_type=F32) * s)
    q_ref[0] = (qk[:, :d] * DA_HEAD_DIM ** -0.5).astype(BF16)
    k_ref[0] = qk[:, d:].astype(BF16)
    v_ref[0] = jnp.dot(x, w_ref[:, 2 * d:], preferred_element_type=F32).astype(BF16)


def da_qkv(u_all, w_in, cos_t, sin_t):
    b_, n, d = u_all.shape
    tb = DA_PROJ_TOKENS
    swap = jnp.arange(2 * d) ^ 1
    w = w_in.astype(BF16)
    w_sw = w[:, :2 * d][:, swap]
    row = pl.BlockSpec((1, tb, d), lambda i, j: (i, j, 0))
    tab = pl.BlockSpec((tb, cos_t.shape[1]), lambda i, j: (j, 0))
    return pl.pallas_call(
        _da_qkv_kernel,
        grid=(b_, n // tb),
        in_specs=[row, pl.BlockSpec(w.shape, lambda i, j: (0, 0)), pl.BlockSpec(w_sw.shape, lambda i, j: (0, 0)),
                  tab, tab],
        out_specs=[row, row, row],
        out_shape=[jax.ShapeDtypeStruct((b_, n, d), BF16)] * 3,
        compiler_params=pltpu.CompilerParams(vmem_limit_bytes=MOSAIC_VMEM_LIMIT),
        name="da_qkv",
    )(u_all, w, w_sw, cos_t, sin_t)


def _da_attn_kernel(lam_ref, q_ref, k_ref, v_ref, g_ref, o_ref, *, post_scale):
    q, k, v = q_ref[0], k_ref[0], v_ref[0]
    lane = lax.broadcasted_iota(jnp.int32, q.shape, 1)
    zero = jnp.zeros_like(q)
    contract_last = (((1,), (1,)), ((), ()))

    def softmax_parts(qm):
        s = lax.dot_general(qm, k, contract_last, preferred_element_type=F32)
        e = jnp.exp(s - jnp.max(s, axis=-1, keepdims=True))
        return e, jnp.sum(e, axis=-1, keepdims=True)

    e0, z0 = softmax_parts(jnp.where(lane < DA_HEAD_DIM, q, zero))
    e1, z1 = softmax_parts(jnp.where(lane >= DA_HEAD_DIM, q, zero))
    a = e0 / z0 - lam_ref[0] * (e1 / z1)
    o = jnp.dot(a.astype(BF16), v, preferred_element_type=F32)
    o = o * lax.rsqrt(jnp.mean(o * o, axis=-1, keepdims=True) + RMS_EPS)
    o_ref[0] = (o * g_ref[...] * post_scale).astype(o_ref.dtype)


def da_attend(q, k, v, lam, subln_g, post_scale, q_start, n_q, n_k):
    b_, _, d = q.shape
    hd = d // DA_HEADS
    tq = DA_Q_TOKENS
    q0 = q_start // tq
    return pl.pallas_call(
        functools.partial(_da_attn_kernel, post_scale=post_scale),
        grid=(b_, DA_HEADS, n_q // tq),
        in_specs=[pl.BlockSpec(memory_space=pltpu.SMEM),
                  pl.BlockSpec((1, tq, hd), lambda i, h, j: (i, q0 + j, h)),
                  pl.BlockSpec((1, n_k, hd), lambda i, h, j: (i, 0, h)),
                  pl.BlockSpec((1, n_k, hd), lambda i, h, j: (i, 0, h)),
                  pl.BlockSpec((1, hd), lambda i, h, j: (0, 0))],
        out_specs=pl.BlockSpec((1, tq, hd), lambda i, h, j: (i, j, h)),
        out_shape=jax.ShapeDtypeStruct((b_, n_q, d), BF16),
        compiler_params=pltpu.CompilerParams(vmem_limit_bytes=MOSAIC_VMEM_LIMIT),
        name="da_attend",
    )(lam.reshape(1), q, k, v, subln_g.reshape(1, hd))


def diff_attention(u_ctx, u_lat, w_in, w_out, lam_q, lam_k, subln_g, lam_init, cos, sin, need_ctx):
    b_, n_ctx, d = u_ctx.shape
    n_lat = u_lat.shape[1]
    lanes_cos = jnp.tile(jnp.repeat(cos, 2, axis=1), (1, 2))
    lanes_sin = jnp.tile(jnp.stack([-sin, sin], axis=-1).reshape(n_lat, -1), (1, 2))
    cos_t = jnp.concatenate([jnp.ones((n_ctx, lanes_cos.shape[1]), F32), lanes_cos], axis=0)
    sin_t = jnp.concatenate([jnp.zeros((n_ctx, lanes_sin.shape[1]), F32), lanes_sin], axis=0)
    u_all = jnp.concatenate([u_ctx, u_lat], axis=1)
    q, k, v = da_qkv(u_all, w_in, cos_t, sin_t)
    lq, lk = lam_q.astype(F32), lam_k.astype(F32)
    lam = jnp.exp(jnp.sum(lq[0] * lk[0])) - jnp.exp(jnp.sum(lq[1] * lk[1])) + lam_init
    post = 1.0 - lam_init
    o_lat = linear(da_attend(q, k, v, lam, subln_g, post, n_ctx, n_lat, n_ctx + n_lat), w_out)
    o_ctx = linear(da_attend(q, k, v, lam, subln_g, post, 0, n_ctx, n_ctx), w_out) if need_ctx else None
    return o_lat, o_ctx


def s5_discretize(lam_re, lam_im, log_dt, b_re, b_im):
    lam_re, lam_im = lam_re.astype(F32), lam_im.astype(F32)
    b_re, b_im = b_re.astype(F32), b_im.astype(F32)
    dt = jnp.exp(log_dt.astype(F32))[:, None]
    mag = jnp.exp(lam_re * dt)
    abar_re, abar_im = mag * jnp.cos(lam_im * dt), mag * jnp.sin(lam_im * dt)
    nr, ni = abar_re - 1.0, abar_im
    den = lam_re * lam_re + lam_im * lam_im
    k_re = (nr * lam_re + ni * lam_im) / den
    k_im = (ni * lam_re - nr * lam_im) / den
    bb_re = k_re[..., None] * b_re - k_im[..., None] * b_im
    bb_im = k_re[..., None] * b_im + k_im[..., None] * b_re
    return abar_re, abar_im, bb_re, bb_im


S5_BATCH_TILE = 8
S5_SCAN_TOKENS = 128
S5_PANEL_GROUPS = 16
S5_PANELS = S5_GROUPS // S5_PANEL_GROUPS


def _s5_panels(bb, c):
    g, p, m = bb.shape
    pg = S5_PANEL_GROUPS
    eye = jnp.eye(pg, dtype=bb.dtype)
    w_in = jnp.einsum('qgpm,gh->qgmhp', bb.reshape(g // pg, pg, p, m), eye).reshape(g // pg, pg * m, pg * p)
    w_out = jnp.einsum('qgmp,gh->qgphm', c.reshape(g // pg, pg, m, p), eye).reshape(g // pg, pg * p, pg * m)
    return w_in.astype(BF16), w_out.astype(BF16)


def _s5_scan_kernel(u_ref, wb_re_ref, wb_im_ref, a_re_ref, a_im_ref, c_re_ref, c_im_ref, y_ref,
                    x_re, x_im, state, *, reverse):
    rows = u_ref.shape[1]
    tokens = rows // S5_BATCH_TILE
    cin = S5_PANEL_GROUPS * S5_GROUP

    @pl.when(pl.program_id(1) == 0)
    def _():
        state[...] = jnp.zeros_like(state)

    for p in range(S5_PANELS):
        ub = u_ref[0, :, p * cin:(p + 1) * cin].astype(BF16)
        x_re[...] = jnp.dot(ub, wb_re_ref[p], preferred_element_type=F32)
        x_im[...] = jnp.dot(ub, wb_im_ref[p], preferred_element_type=F32)
        a_re, a_im = a_re_ref[p], a_im_ref[p]

        def step(i, carry):
            xr, xi = carry
            t = (tokens - 1 - i) if reverse else i
            r0 = pl.multiple_of(t * S5_BATCH_TILE, S5_BATCH_TILE)
            nr = a_re * xr - a_im * xi + x_re[pl.ds(r0, S5_BATCH_TILE), :]
            ni = a_re * xi + a_im * xr + x_im[pl.ds(r0, S5_BATCH_TILE), :]
            x_re[pl.ds(r0, S5_BATCH_TILE), :] = nr
            x_im[pl.ds(r0, S5_BATCH_TILE), :] = ni
            return nr, ni

        xr, xi = lax.fori_loop(0, tokens, step, (state[p, 0], state[p, 1]), unroll=2)
        state[p, 0] = xr
        state[p, 1] = xi
        y_ref[0, :, p * cin:(p + 1) * cin] = (
            jnp.dot(x_re[...].astype(BF16), c_re_ref[p], preferred_element_type=F32)
            - jnp.dot(x_im[...].astype(BF16), c_im_ref[p], preferred_element_type=F32))


def s5_scan(u_g, lam_re, lam_im, log_dt, b_re, b_im, c_re, c_im, n_ctx, reverse):
    nbg, rows_total, d = u_g.shape
    abar_re, abar_im, bb_re, bb_im = s5_discretize(lam_re, lam_im, log_dt, b_re, b_im)
    wb_re, cp_re = _s5_panels(bb_re, c_re.astype(F32))
    wb_im, cp_im = _s5_panels(bb_im, c_im.astype(F32))
    states = S5_PANEL_GROUPS * S5_STATE
    tile = lambda a: jnp.broadcast_to(a.reshape(S5_PANELS, 1, states), (S5_PANELS, S5_BATCH_TILE, states))
    rows = S5_SCAN_TOKENS * S5_BATCH_TILE
    n_chunks = rows_total // rows
    ctx_chunks = n_ctx // S5_SCAN_TOKENS
    if reverse:
        chunk = lambda s: jnp.where(s < ctx_chunks, ctx_chunks - 1 - s, n_chunks - 1 - (s - ctx_chunks))
    else:
        chunk = lambda s: s
    full = lambda a: pl.BlockSpec(a.shape, lambda i, s: (0,) * a.ndim)
    a_re_t, a_im_t = tile(abar_re), tile(abar_im)
    return pl.pallas_call(
        functools.partial(_s5_scan_kernel, reverse=reverse),
        grid=(nbg, n_chunks),
        in_specs=[pl.BlockSpec((1, rows, d), lambda i, s: (i, chunk(s), 0)),
                  full(wb_re), full(wb_im), full(a_re_t), full(a_im_t), full(cp_re), full(cp_im)],
        out_specs=pl.BlockSpec((1, rows, d), lambda i, s: (i, chunk(s), 0)),
        out_shape=jax.ShapeDtypeStruct(u_g.shape, F32),
        scratch_shapes=[pltpu.VMEM((rows, states), F32), pltpu.VMEM((rows, states), F32),
                        pltpu.VMEM((S5_PANELS, 2, S5_BATCH_TILE, states), F32)],
        compiler_params=pltpu.CompilerParams(dimension_semantics=("arbitrary", "arbitrary"),
                                             vmem_limit_bytes=MOSAIC_VMEM_LIMIT),
        name="s5_scan_bwd" if reverse else "s5_scan_fwd",
    )(u_g, wb_re, wb_im, a_re_t, a_im_t, cp_re, cp_im)


def _s5_glu_kernel(yf_ref, yb_ref, u_ref, d_ref, w_ref, o_ref):
    d = o_ref.shape[-1]
    y = yf_ref[0] + yb_ref[0] + d_ref[...] * u_ref[0]
    r = jnp.dot(_gelu_tanh(y).astype(BF16), w_ref[...], preferred_element_type=F32)
    o_ref[0] = r[:, :d] * jax.nn.sigmoid(r[:, d:])


def s5_glu(y_fw, y_bw, u_g, d_skip, w_glu, block_rows=512):
    nbg, rows_total, d = u_g.shape
    blk = pl.BlockSpec((1, block_rows, d), lambda i, j: (i, j, 0))
    return pl.pallas_call(
        _s5_glu_kernel,
        grid=(nbg, rows_total // block_rows),
        in_specs=[blk, blk, blk,
                  pl.BlockSpec((1, d), lambda i, j: (0, 0)),
                  pl.BlockSpec(w_glu.shape, lambda i, j: (0, 0))],
        out_specs=blk,
        out_shape=jax.ShapeDtypeStruct(u_g.shape, F32),
        compiler_params=pltpu.CompilerParams(vmem_limit_bytes=MOSAIC_VMEM_LIMIT),
        name="s5_glu",
    )(y_fw, y_bw, u_g, d_skip.reshape(1, d), w_glu.astype(BF16))


def s5_mixer(u_ctx, u_lat, lam_re, lam_im, log_dt, b_re, b_im, c_re, c_im, d_skip, w_glu, need_ctx):
    b_, n_ctx, d = u_ctx.shape
    n_all = n_ctx + u_lat.shape[1]
    bt = S5_BATCH_TILE
    u_all = jnp.concatenate([u_ctx, u_lat], axis=1)
    u_g = u_all.reshape(b_ // bt, bt, n_all, d).transpose(0, 2, 1, 3).reshape(b_ // bt, n_all * bt, d)
    ys = [s5_scan(u_g, lam_re[dr], lam_im[dr], log_dt[dr], b_re[dr], b_im[dr], c_re[dr], c_im[dr],
                  n_ctx, reverse=bool(dr)) for dr in range(2)]
    o_g = s5_glu(ys[0], ys[1], u_g, d_skip, w_glu)
    o = o_g.reshape(b_ // bt, n_all, bt, d).transpose(0, 2, 1, 3).reshape(b_, n_all, d)
    return o[:, n_ctx:], (o[:, :n_ctx] if need_ctx else None)


HG_SCAN_TOKENS = 256
HG_SUB = 32
HG_STEP_HEADS = 4


def _hgrn_scan_kernel(q_ref, f_ref, v_ref, lb_ref, tri_ref, o_ref, q_s, k_s, cum_s, v_s, st, *, reverse):
    tokens = q_ref.shape[1]

    @pl.when(pl.program_id(2) == 0)
    def _():
        st[...] = jnp.zeros_like(st)

    f = f_ref[0]
    log_f = jnp.logaddexp(lb_ref[0:1, :], lb_ref[1:2, :] + jax.nn.log_sigmoid(f))
    k_all = lb_ref[2:3, :] * jax.nn.sigmoid(-f)
    q_all = jax.nn.silu(q_ref[0])
    cum_all = jnp.dot(tri_ref[...], log_f, precision=lax.Precision.HIGHEST, preferred_element_type=F32)
    for hd in range(HG_STEP_HEADS):
        lanes = slice(hd * HG_KEY, (hd + 1) * HG_KEY)
        k_s[hd] = k_all[:, lanes]
        q_s[hd] = q_all[:, lanes]
        cum_s[hd] = cum_all[:, lanes]
        v_s[hd] = v_ref[0, :, lanes]
    tpos = lax.broadcasted_iota(jnp.int32, (HG_SUB, HG_KEY), 0)
    nsub = tokens // HG_SUB
    ones = jnp.ones((HG_KEY, HG_VAL), BF16)

    def head_block(base, hd):
        lanes = slice(hd * HG_KEY, (hd + 1) * HG_KEY)
        qc = q_s[hd, pl.ds(base, HG_SUB), :]
        cumc = cum_s[hd, pl.ds(base, HG_SUB), :]

        def column(s, o):
            ks = k_s[hd, pl.ds(base + s, 1), :]
            cs = cum_s[hd, pl.ds(base + s, 1), :]
            vs = v_s[hd, pl.ds(base + s, 1), :]
            seen = (tpos <= s) if reverse else (tpos >= s)
            decay = jnp.exp(jnp.where(seen, cumc - cs, -jnp.inf))
            att = jnp.dot((qc * ks * decay).astype(BF16), ones, preferred_element_type=F32)
            return o + att * vs

        o = lax.fori_loop(0, HG_SUB, column, jnp.zeros(qc.shape, F32), unroll=True)
        s_t = st[hd]
        o = o + lax.dot_general((qc * jnp.exp(cumc)).astype(BF16), s_t.astype(BF16),
                                (((1,), (1,)), ((), ())), preferred_element_type=F32)
        last = cum_s[hd, pl.ds(base + (0 if reverse else HG_SUB - 1), 1), :]
        kh = (k_s[hd, pl.ds(base, HG_SUB), :] * jnp.exp(last - cumc)).astype(BF16)
        vc = v_s[hd, pl.ds(base, HG_SUB), :].astype(BF16)
        st[hd] = s_t * jnp.exp(last) + lax.dot_general(vc, kh, (((0,), (0,)), ((), ())),
                                                       preferred_element_type=F32)
        o_ref[0, pl.ds(base, HG_SUB), lanes] = o

    def block(i, carry):
        c = (nsub - 1 - i) if reverse else i
        base = pl.multiple_of(c * HG_SUB, HG_SUB)
        for hd in range(HG_STEP_HEADS):
            head_block(base, hd)
        return carry

    lax.fori_loop(0, nsub, block, 0)


def hgrn_scan(proj, lb_rows, n_ctx, f_col, reverse):
    b_, n, _ = proj.shape
    t = HG_SCAN_TOKENS
    hk = HG_STEP_HEADS * HG_KEY
    steps = HG_HEADS // HG_STEP_HEADS
    n_chunks, ctx_chunks = n // t, n_ctx // t
    if reverse:
        chunk = lambda s: jnp.where(s < ctx_chunks, ctx_chunks - 1 - s, n_chunks - 1 - (s - ctx_chunks))
    else:
        chunk = lambda s: s
    pos = jnp.arange(t)
    same = (pos[:, None] // HG_SUB) == (pos[None, :] // HG_SUB)
    order = (pos[None, :] >= pos[:, None]) if reverse else (pos[None, :] <= pos[:, None])
    tri = (same & order).astype(F32)
    col = lambda off: pl.BlockSpec((1, t, hk), lambda i, h, s: (i, chunk(s), off * steps + h))
    return pl.pallas_call(
        functools.partial(_hgrn_scan_kernel, reverse=reverse),
        grid=(b_, steps, n_chunks),
        in_specs=[col(0), col(f_col), col(3),
                  pl.BlockSpec((3, hk), lambda i, h, s: (0, h)),
                  pl.BlockSpec((t, t), lambda i, h, s: (0, 0))],
        out_specs=pl.BlockSpec((1, t, hk), lambda i, h, s: (i, chunk(s), h)),
        out_shape=jax.ShapeDtypeStruct((b_, n, HG_HEADS * HG_VAL), F32),
        scratch_shapes=[pltpu.VMEM((HG_STEP_HEADS, t, HG_KEY), F32)] * 4
                       + [pltpu.VMEM((HG_STEP_HEADS, HG_VAL, HG_KEY), F32)],
        compiler_params=pltpu.CompilerParams(dimension_semantics=("arbitrary", "arbitrary", "arbitrary"),
                                             vmem_limit_bytes=MOSAIC_VMEM_LIMIT),
        name="hgrn_scan_bwd" if reverse else "hgrn_scan_fwd",
    )(proj, proj, proj, lb_rows, tri)


def _hgrn_out_kernel(of_ref, ob_ref, gate_ref, g_ref, w_ref, o_ref):
    o = of_ref[0] + ob_ref[0]
    parts = []
    for h in range(HG_HEADS):
        oh = o[:, h * HG_VAL:(h + 1) * HG_VAL]
        parts.append(oh * lax.rsqrt(jnp.mean(oh * oh, axis=-1, keepdims=True) + RMS_EPS))
    y = jnp.concatenate(parts, axis=1) * g_ref[...] * jax.nn.silu(gate_ref[0])
    o_ref[0] = jnp.dot(y.astype(BF16), w_ref[...], preferred_element_type=F32)


def hgrn_out(o_fw, o_bw, proj, norm_g, w_out, block_rows=256):
    b_, n, d = o_fw.shape
    assert n % block_rows == 0
    row = pl.BlockSpec((1, block_rows, d), lambda i, j: (i, j, 0))
    return pl.pallas_call(
        _hgrn_out_kernel,
        grid=(b_, n // block_rows),
        in_specs=[row, row, pl.BlockSpec((1, block_rows, d), lambda i, j: (i, j, 4)),
                  pl.BlockSpec((1, d), lambda i, j: (0, 0)), pl.BlockSpec(w_out.shape, lambda i, j: (0, 0))],
        out_specs=row,
        out_shape=jax.ShapeDtypeStruct((b_, n, d), F32),
        compiler_params=pltpu.CompilerParams(vmem_limit_bytes=MOSAIC_VMEM_LIMIT),
        name="hgrn_out",
    )(o_fw, o_bw, proj, norm_g.reshape(1, d), w_out.astype(BF16))


def hgrn2_mixer(u_ctx, u_lat, w_in, w_out, norm_g, lb, need_ctx):
    n_ctx = u_ctx.shape[1]
    proj = linear(jnp.concatenate([u_ctx, u_lat], axis=1), w_in)
    lb_rows = jnp.stack([jnp.log(lb), jnp.log1p(-lb), 1.0 - lb])
    o_fw = hgrn_scan(proj, lb_rows, n_ctx, 1, reverse=False)
    o_bw = hgrn_scan(proj, lb_rows, n_ctx, 2, reverse=True)
    o = hgrn_out(o_fw, o_bw, proj, norm_g, w_out)
    return o[:, n_ctx:], (o[:, :n_ctx] if need_ctx else None)


PEER_SLOTS = PEER_HEADS * PEER_TOPK
PEER_SEL_TOKENS = 256
PEER_MIX_TOKENS = 256
PEER_ROW_BUFFERS = 8
PEER_QUEUE_PATTERN = (0, 1)


def _topk_axis0(cur, k, payload=None):
    rows = cur.shape[0]
    iota = lax.broadcasted_iota(jnp.int32, cur.shape, 0)
    vals, picks = [], []
    for _ in range(k):
        m = jnp.max(cur, axis=0, keepdims=True)
        pos = jnp.min(jnp.where(cur == m, iota, rows), axis=0, keepdims=True)
        hit = iota == pos
        vals.append(m)
        if payload is None:
            picks.append(pos)
        else:
            picks.append(jnp.sum(jnp.where(hit, payload, 0), axis=0, keepdims=True))
        cur = jnp.where(hit, -jnp.inf, cur)
    return jnp.concatenate(vals, axis=0), jnp.concatenate(picks, axis=0)


def _peer_select_kernel(h_ref, sc_ref, sh_ref, wq_ref, keys_ref, idx_ref, g_ref):
    half = PEER_QDIM // 2
    x = h_ref[0] * (1.0 + sc_ref[0]) + sh_ref[0]
    q = jnp.dot(x.astype(BF16), wq_ref[...], preferred_element_type=F32)
    tokens = x.shape[0]
    for c0 in range(0, tokens, LANES):
        idx_rows, g_rows = [], []
        for hd in range(PEER_HEADS):
            tops = []
            for c in range(2):
                lo = (hd * 2 + c) * half
                qhc = q[c0:c0 + LANES, lo:lo + half].astype(BF16)
                s_t = lax.dot_general(keys_ref[c], qhc, (((1,), (1,)), ((), ())),
                                      preferred_element_type=F32)
                tops.append(_topk_axis0(s_t, PEER_TOPK))
            (s1, i1), (s2, i2) = tops
            width = [PEER_TOPK // (a + 1) for a in range(PEER_TOPK)]
            pad = -sum(width) % 8
            cand_s = jnp.concatenate([s1[a:a + 1] + s2[:width[a]] for a in range(PEER_TOPK)]
                                     + [jnp.full((pad, LANES), -jnp.inf, F32)], axis=0)
            cand_i = jnp.concatenate([i1[a:a + 1] * PEER_NKEYS + i2[:width[a]] for a in range(PEER_TOPK)]
                                     + [jnp.zeros((pad, LANES), jnp.int32)], axis=0)
            top_s, top_i = _topk_axis0(cand_s, PEER_TOPK, payload=cand_i)
            e = jnp.exp(top_s - top_s[0:1])
            g_rows.append(e / jnp.sum(e, axis=0, keepdims=True))
            idx_rows.append(top_i)
        idx_ref[0, c0:c0 + LANES, :] = jnp.concatenate(idx_rows, axis=0).T
        g_ref[0, c0:c0 + LANES, :] = jnp.concatenate(g_rows, axis=0).T


def peer_select(h, sc, sh, w_q, sub_keys):
    b_, n, d = h.shape
    tb = min(PEER_SEL_TOKENS, n)
    mod_map = (lambda i, j: (i, 0, 0)) if sc.shape[0] == b_ else (lambda i, j: (0, 0, 0))
    return pl.pallas_call(
        _peer_select_kernel,
        grid=(b_, n // tb),
        in_specs=[pl.BlockSpec((1, tb, d), lambda i, j: (i, j, 0)),
                  pl.BlockSpec((1, 1, d), mod_map),
                  pl.BlockSpec((1, 1, d), mod_map),
                  pl.BlockSpec(w_q.shape, lambda i, j: (0, 0)),
                  pl.BlockSpec(sub_keys.shape, lambda i, j: (0, 0, 0))],
        out_specs=[pl.BlockSpec((1, tb, PEER_SLOTS), lambda i, j: (i, j, 0)),
                   pl.BlockSpec((1, tb, PEER_SLOTS), lambda i, j: (i, j, 0))],
        out_shape=[jax.ShapeDtypeStruct((b_, n, PEER_SLOTS), jnp.int32),
                   jax.ShapeDtypeStruct((b_, n, PEER_SLOTS), F32)],
        compiler_params=pltpu.CompilerParams(vmem_limit_bytes=48 * 1024 * 1024),
        name="peer_select",
    )(h, sc, sh, w_q.astype(BF16), sub_keys.astype(BF16))


def peer_table(u_tab, v_tab):
    bits = lambda t: lax.bitcast_convert_type(t.astype(BF16), jnp.uint16).astype(jnp.uint32)
    return ((bits(u_tab) << 16) | bits(v_tab))[:, None, :]


def _peer_mix_kernel(idx_hbm, h_ref, sc_ref, sh_ref, g_ref, tab_hbm, out_ref,
                     idx_smem, x_scr, rows, row_sem, idx_sem):
    tokens, d = x_scr.shape
    nbuf = PEER_ROW_BUFFERS
    ahead = nbuf - 1
    blk = pl.program_id(0) * pl.num_programs(1) + pl.program_id(1)
    per_blk = tokens * PEER_SLOTS
    idx_copy = pltpu.make_async_copy(idx_hbm.at[pl.ds(pl.multiple_of(blk * per_blk, per_blk), per_blk)],
                                     idx_smem, idx_sem)
    idx_copy.start()
    x_scr[...] = h_ref[0] * (1.0 + sc_ref[0]) + sh_ref[0]
    idx_copy.wait()

    def fetch(t, slot):
        for j in range(PEER_SLOTS):
            pltpu.make_async_copy(tab_hbm.at[idx_smem[t * PEER_SLOTS + j]], rows.at[slot, pl.ds(j, 1), :],
                                  row_sem.at[slot]).start(priority=PEER_QUEUE_PATTERN[j % len(PEER_QUEUE_PATTERN)])

    def wait_rows(slot):
        pltpu.make_async_copy(rows.at[slot], rows.at[slot], row_sem.at[slot]).wait()

    eye = (lax.broadcasted_iota(jnp.int32, (PEER_SLOTS, PEER_SLOTS), 0)
           == lax.broadcasted_iota(jnp.int32, (PEER_SLOTS, PEER_SLOTS), 1))

    def combine(t, slot):
        x_row = x_scr[pl.ds(t, 1), :]
        words = rows[slot]
        u_rows = lax.bitcast_convert_type(words & jnp.uint32(0xFFFF0000), F32)
        v_rows = lax.bitcast_convert_type(words << 16, F32)
        act = jnp.sum(u_rows * x_row, axis=1, keepdims=True)
        g_col = jnp.sum(jnp.where(eye, g_ref[0, pl.ds(t, 1), :], 0.0), axis=1, keepdims=True)
        w = g_col * _gelu_tanh(act)
        out_ref[0, pl.ds(t, 1), :] = jnp.sum(w * v_rows, axis=0, keepdims=True)

    def token(t, slot, prefetch):
        if prefetch:
            fetch(t + ahead, (slot + ahead) % nbuf)
        wait_rows(slot)
        combine(t, slot)

    for t0 in range(ahead):
        fetch(t0, t0)

    def group(i, carry):
        for slot in range(nbuf):
            token(i * nbuf + slot, slot, True)
        return carry

    lax.fori_loop(0, tokens // nbuf - 1, group, 0)
    for slot in range(nbuf):
        token(tokens - nbuf + slot, slot, slot + ahead < nbuf)


def peer_mix(h, sc, sh, idx, g, uv_tab):
    b_, n, d = h.shape
    tb = min(PEER_MIX_TOKENS, n)
    assert n % tb == 0 and tb % PEER_ROW_BUFFERS == 0
    mod_map = (lambda i, j: (i, 0, 0)) if sc.shape[0] == b_ else (lambda i, j: (0, 0, 0))
    return pl.pallas_call(
        _peer_mix_kernel,
        grid=(b_, n // tb),
        in_specs=[pl.BlockSpec(memory_space=pl.ANY),
                  pl.BlockSpec((1, tb, d), lambda i, j: (i, j, 0)),
                  pl.BlockSpec((1, 1, d), mod_map),
                  pl.BlockSpec((1, 1, d), mod_map),
                  pl.BlockSpec((1, tb, PEER_SLOTS), lambda i, j: (i, j, 0)),
                  pl.BlockSpec(memory_space=pl.ANY)],
        out_specs=pl.BlockSpec((1, tb, d), lambda i, j: (i, j, 0)),
        out_shape=jax.ShapeDtypeStruct(h.shape, F32),
        scratch_shapes=[pltpu.SMEM((tb * PEER_SLOTS,), jnp.int32),
                        pltpu.VMEM((tb, d), F32),
                        pltpu.VMEM((PEER_ROW_BUFFERS, PEER_SLOTS, d), jnp.uint32),
                        pltpu.SemaphoreType.DMA((PEER_ROW_BUFFERS,)),
                        pltpu.SemaphoreType.DMA(())],
        compiler_params=pltpu.CompilerParams(vmem_limit_bytes=32 * 1024 * 1024),
        name="peer_mix",
    )(idx.reshape(-1), h, sc, sh, g, uv_tab)


def peer_ffn(h, sc, sh, w_q, sub_keys, uv_tab):
    idx, g = peer_select(h, sc, sh, w_q, sub_keys)
    return peer_mix(h, sc, sh, idx, g, uv_tab)


def kernel(x, c, ctx, c_ctx, ada_w, ada_b, ln_g, ln_b, da_w_in, da_w_out, da_lam_q, da_lam_k, da_subln,
           s5_lam_re, s5_lam_im, s5_log_dt, s5_b_re, s5_b_im, s5_c_re, s5_c_im, s5_d, s5_w_glu,
           hg_w_in, hg_w_out, hg_norm, hg_lb, peer_wq, peer_keys, peer_u, peer_v):
    L = x.shape[1]
    cos, sin = axial_rope(L, DA_HEAD_DIM)
    s_c = jax.nn.silu(c)
    s_ctx = jax.nn.silu(c_ctx)
    lb_soft = jax.nn.softmax(hg_lb.astype(F32), axis=0)
    lb_all = jnp.cumsum(lb_soft, axis=0) - lb_soft[0]
    h, hc = x, ctx
    for i in range(DEPTH):
        kind, slot = LAYER_TYPES[i], i // N_MIXERS
        need_ctx = i < DEPTH - 1
        mod = (s_c @ ada_w[i] + ada_b[i])[:, None, :]
        mod_c = s_ctx @ ada_w[i] + ada_b[i]
        sh1, sc1, g1, sh2, sc2, g2 = jnp.split(mod, 6, axis=-1)
        csh1, csc1, cg1, csh2, csc2, cg2 = jnp.split(mod_c, 6, axis=-1)
        u = h * (1.0 + sc1) + sh1
        uc = hc * (1.0 + csc1) + csh1
        if kind == 0:
            lam_init = 0.8 - 0.6 * math.exp(-0.3 * i)
            o, oc = diff_attention(uc, u, da_w_in[slot], da_w_out[slot], da_lam_q[slot], da_lam_k[slot],
                                   da_subln[slot], lam_init, cos, sin, need_ctx)
        elif kind == 1:
            o, oc = s5_mixer(uc, u, s5_lam_re[slot], s5_lam_im[slot], s5_log_dt[slot], s5_b_re[slot],
                             s5_b_im[slot], s5_c_re[slot], s5_c_im[slot], s5_d[slot], s5_w_glu[slot], need_ctx)
        else:
            o, oc = hgrn2_mixer(uc, u, hg_w_in[slot], hg_w_out[slot], hg_norm[slot], lb_all[i], need_ctx)
        h = residual_layer_norm(h, o, g1, ln_g[i, 0], ln_b[i, 0])
        uv_tab = peer_table(peer_u[i], peer_v[i])
        f = peer_ffn(h, sc2, sh2, peer_wq[i], peer_keys[i], uv_tab)
        h = residual_layer_norm(h, f, g2, ln_g[i, 1], ln_b[i, 1])
        if need_ctx:
            hc = residual_layer_norm(hc, oc, cg1.reshape(1, 1, -1), ln_g[i, 0], ln_b[i, 0])
            fc = peer_ffn(hc, csc2.reshape(1, 1, -1), csh2.reshape(1, 1, -1), peer_wq[i], peer_keys[i], uv_tab)
            hc = residual_layer_norm(hc, fc, cg2.reshape(1, 1, -1), ln_g[i, 1], ln_b[i, 1])
    return h
```

```python
import functools
import math
import jax, jax.numpy as jnp
from jax import lax
import numpy as np
from jax.experimental import pallas as pl
from jax.experimental.pallas import tpu as pltpu

D_MODEL = 1024
BATCH = 32
SEQ = 2048
DEPTH = 4

CTX_LEN = 256
GRID_W = 64
N_MIXERS = 3
LAYER_TYPES = tuple(i % N_MIXERS for i in range(DEPTH))
N_ATTN = LAYER_TYPES.count(0)
N_S5 = LAYER_TYPES.count(1)
N_HG = LAYER_TYPES.count(2)

DA_HEADS = 8
DA_HEAD_DIM = 64
DA_V_DIM = 2 * DA_HEAD_DIM
Q_BLOCK = 128
ROPE_THETA = 10000.0
S5_GROUP = 16
S5_GROUPS = D_MODEL // S5_GROUP
S5_STATE = 64
HG_HEADS = 8
HG_KEY = D_MODEL // HG_HEADS
HG_VAL = D_MODEL // HG_HEADS
HG_CHUNK = 32
PEER_HEADS = 8
PEER_NKEYS = 128
PEER_EXPERTS = PEER_NKEYS * PEER_NKEYS
PEER_QDIM = 256
PEER_TOPK = 16
PEER_BLOCK = 128
LN_EPS = 1e-5
RMS_EPS = 1e-6
DN_ALPHA = (2 * DEPTH) ** 0.25
DN_BETA = (8 * DEPTH) ** -0.25

F32 = jnp.float32
BF16 = jnp.bfloat16
LANES = 128
MOSAIC_VMEM_LIMIT = 48 * 1024 * 1024


def _gelu_tanh(x):
    return 0.5 * x * (1.0 + jnp.tanh(math.sqrt(2.0 / math.pi) * (x + 0.044715 * (x * x * x))))


def _res_ln_kernel(h_ref, o_ref, gate_ref, g_ref, b_ref, out_ref):
    y = DN_ALPHA * h_ref[0] + gate_ref[0] * o_ref[0]
    mu = jnp.mean(y, -1, keepdims=True)
    yc = y - mu
    var = jnp.mean(yc * yc, -1, keepdims=True)
    out_ref[0] = yc * lax.rsqrt(var + LN_EPS) * g_ref[...] + b_ref[...]


def residual_layer_norm(h, o, gate, g, b, block_n=512):
    b_, n, d = h.shape
    bn = min(block_n, n)
    per_batch_gate = gate.shape[0] == b_
    gate_map = (lambda i, j: (i, 0, 0)) if per_batch_gate else (lambda i, j: (0, 0, 0))
    return pl.pallas_call(
        _res_ln_kernel,
        grid=(b_, n // bn),
        in_specs=[pl.BlockSpec((1, bn, d), lambda i, j: (i, j, 0)),
                  pl.BlockSpec((1, bn, d), lambda i, j: (i, j, 0)),
                  pl.BlockSpec((1, 1, d), gate_map),
                  pl.BlockSpec((1, d), lambda i, j: (0, 0)),
                  pl.BlockSpec((1, d), lambda i, j: (0, 0))],
        out_specs=pl.BlockSpec((1, bn, d), lambda i, j: (i, j, 0)),
        out_shape=jax.ShapeDtypeStruct(h.shape, h.dtype),
        name="residual_layer_norm",
    )(h, o, gate, g.reshape(1, d), b.reshape(1, d))


def axial_rope(length, dim):
    rows = length // GRID_W
    row = jnp.repeat(jnp.arange(rows, dtype=F32), GRID_W)
    col = jnp.tile(jnp.arange(GRID_W, dtype=F32), rows)
    n_freq = dim // 4
    inv = ROPE_THETA ** (-jnp.arange(n_freq, dtype=F32) / n_freq)
    ang = jnp.concatenate([row[:, None] * inv, col[:, None] * inv], axis=-1)
    return jnp.cos(ang), jnp.sin(ang)


def _linear_kernel(x_ref, w_ref, o_ref):
    o_ref[0] = jnp.dot(x_ref[0].astype(BF16), w_ref[...], preferred_element_type=F32).astype(o_ref.dtype)


def linear(x, w, out_dtype=F32, block_rows=256, block_cols=1024):
    b_, n, kdim = x.shape
    ncols = w.shape[1]
    br, bc = min(block_rows, n), min(block_cols, ncols)
    assert n % br == 0 and ncols % bc == 0
    return pl.pallas_call(
        _linear_kernel,
        grid=(b_, n // br, ncols // bc),
        in_specs=[pl.BlockSpec((1, br, kdim), lambda i, j, c: (i, j, 0)),
                  pl.BlockSpec((kdim, bc), lambda i, j, c: (0, c))],
        out_specs=pl.BlockSpec((1, br, bc), lambda i, j, c: (i, j, c)),
        out_shape=jax.ShapeDtypeStruct((b_, n, ncols), out_dtype),
        compiler_params=pltpu.CompilerParams(vmem_limit_bytes=MOSAIC_VMEM_LIMIT),
        name="linear",
    )(x, w.astype(BF16))


DA_Q_TOKENS = 256
DA_PROJ_TOKENS = 256


def _da_qkv_kernel(x_ref, w_ref, wsw_ref, cos_ref, sin_ref, q_ref, k_ref, v_ref):
    d = q_ref.shape[-1]
    x = x_ref[0].astype(BF16)
    reps = 2 * d // cos_ref.shape[-1]
    c = jnp.concatenate([cos_ref[...]] * reps, axis=1)
    s = jnp.concatenate([sin_ref[...]] * reps, axis=1)
    qk = (jnp.dot(x, w_ref[:, :2 * d], preferred_element_type=F32) * c
          + jnp.dot(x, wsw_ref[...], preferred_element_type=F32) * s)
    q_ref[0] = (qk[:, :d] * DA_HEAD_DIM ** -0.5).astype(BF16)
    k_ref[0] = qk[:, d:].astype(BF16)
    v_ref[0] = jnp.dot(x, w_ref[:, 2 * d:], preferred_element_type=F32).astype(BF16)


def da_qkv(u_all, w_in, cos_t, sin_t):
    b_, n, d = u_all.shape
    tb = DA_PROJ_TOKENS
    swap = jnp.arange(2 * d) ^ 1
    w = w_in.astype(BF16)
    w_sw = w[:, :2 * d][:, swap]
    row = pl.BlockSpec((1, tb, d), lambda i, j: (i, j, 0))
    tab = pl.BlockSpec((tb, cos_t.shape[1]), lambda i, j: (j, 0))
    return pl.pallas_call(
        _da_qkv_kernel,
        grid=(b_, n // tb),
        in_specs=[row, pl.BlockSpec(w.shape, lambda i, j: (0, 0)), pl.BlockSpec(w_sw.shape, lambda i, j: (0, 0)),
                  tab, tab],
        out_specs=[row, row, row],
        out_shape=[jax.ShapeDtypeStruct((b_, n, d), BF16)] * 3,
        compiler_params=pltpu.CompilerParams(vmem_limit_bytes=MOSAIC_VMEM_LIMIT),
        name="da_qkv",
    )(u_all, w, w_sw, cos_t, sin_t)


def _da_attn_kernel(lam_ref, q_ref, k_ref, v_ref, g_ref, o_ref, *, post_scale):
    q, k, v = q_ref[0], k_ref[0], v_ref[0]
    lane = lax.broadcasted_iota(jnp.int32, q.shape, 1)
    zero = jnp.zeros_like(q)
    contract_last = (((1,), (1,)), ((), ()))

    def softmax_parts(qm):
        s = lax.dot_general(qm, k, contract_last, preferred_element_type=F32)
        e = jnp.exp(s - jnp.max(s, axis=-1, keepdims=True))
        return e, jnp.sum(e, axis=-1, keepdims=True)

    e0, z0 = softmax_parts(jnp.where(lane < DA_HEAD_DIM, q, zero))
    e1, z1 = softmax_parts(jnp.where(lane >= DA_HEAD_DIM, q, zero))
    a = e0 * (1.0 / z0) - e1 * (lam_ref[0] / z1)
    o = jnp.dot(a.astype(BF16), v, preferred_element_type=F32)
    o = o * lax.rsqrt(jnp.mean(o * o, axis=-1, keepdims=True) + RMS_EPS)
    o_ref[0] = (o * g_ref[...] * post_scale).astype(o_ref.dtype)


def da_attend(q, k, v, lam, subln_g, post_scale, q_start, n_q, n_k):
    b_, _, d = q.shape
    hd = d // DA_HEADS
    tq = DA_Q_TOKENS
    q0 = q_start // tq
    return pl.pallas_call(
        functools.partial(_da_attn_kernel, post_scale=post_scale),
        grid=(b_, DA_HEADS, n_q // tq),
        in_specs=[pl.BlockSpec(memory_space=pltpu.SMEM),
                  pl.BlockSpec((1, tq, hd), lambda i, h, j: (i, q0 + j, h)),
                  pl.BlockSpec((1, n_k, hd), lambda i, h, j: (i, 0, h)),
                  pl.BlockSpec((1, n_k, hd), lambda i, h, j: (i, 0, h)),
                  pl.BlockSpec((1, hd), lambda i, h, j: (0, 0))],
        out_specs=pl.BlockSpec((1, tq, hd), lambda i, h, j: (i, j, h)),
        out_shape=jax.ShapeDtypeStruct((b_, n_q, d), BF16),
        compiler_params=pltpu.CompilerParams(vmem_limit_bytes=MOSAIC_VMEM_LIMIT),
        name="da_attend",
    )(lam.reshape(1), q, k, v, subln_g.reshape(1, hd))


def diff_attention(u_ctx, u_lat, w_in, w_out, lam_q, lam_k, subln_g, lam_init, cos, sin, need_ctx):
    b_, n_ctx, d = u_ctx.shape
    n_lat = u_lat.shape[1]
    lanes_cos = jnp.tile(jnp.repeat(cos, 2, axis=1), (1, 2))
    lanes_sin = jnp.tile(jnp.stack([-sin, sin], axis=-1).reshape(n_lat, -1), (1, 2))
    cos_t = jnp.concatenate([jnp.ones((n_ctx, lanes_cos.shape[1]), F32), lanes_cos], axis=0)
    sin_t = jnp.concatenate([jnp.zeros((n_ctx, lanes_sin.shape[1]), F32), lanes_sin], axis=0)
    u_all = jnp.concatenate([u_ctx, u_lat], axis=1)
    q, k, v = da_qkv(u_all, w_in, cos_t, sin_t)
    lq, lk = lam_q.astype(F32), lam_k.astype(F32)
    lam = jnp.exp(jnp.sum(lq[0] * lk[0])) - jnp.exp(jnp.sum(lq[1] * lk[1])) + lam_init
    post = 1.0 - lam_init
    o_lat = linear(da_attend(q, k, v, lam, subln_g, post, n_ctx, n_lat, n_ctx + n_lat), w_out)
    o_ctx = linear(da_attend(q, k, v, lam, subln_g, post, 0, n_ctx, n_ctx), w_out) if need_ctx else None
    return o_lat, o_ctx


def s5_discretize(lam_re, lam_im, log_dt, b_re, b_im):
    lam_re, lam_im = lam_re.astype(F32), lam_im.astype(F32)
    b_re, b_im = b_re.astype(F32), b_im.astype(F32)
    dt = jnp.exp(log_dt.astype(F32))[:, None]
    mag = jnp.exp(lam_re * dt)
    abar_re, abar_im = mag * jnp.cos(lam_im * dt), mag * jnp.sin(lam_im * dt)
    nr, ni = abar_re - 1.0, abar_im
    den = lam_re * lam_re + lam_im * lam_im
    k_re = (nr * lam_re + ni * lam_im) / den
    k_im = (ni * lam_re - nr * lam_im) / den
    bb_re = k_re[..., None] * b_re - k_im[..., None] * b_im
    bb_im = k_re[..., None] * b_im + k_im[..., None] * b_re
    return abar_re, abar_im, bb_re, bb_im


S5_BATCH_TILE = 8
S5_SCAN_TOKENS = 128
S5_PANEL_GROUPS = 16
S5_PANELS = S5_GROUPS // S5_PANEL_GROUPS


def _s5_panels(bb, c):
    g, p, m = bb.shape
    pg = S5_PANEL_GROUPS
    eye = jnp.eye(pg, dtype=bb.dtype)
    w_in = jnp.einsum('qgpm,gh->qgmhp', bb.reshape(g // pg, pg, p, m), eye).reshape(g // pg, pg * m, pg * p)
    w_out = jnp.einsum('qgmp,gh->qgphm', c.reshape(g // pg, pg, m, p), eye).reshape(g // pg, pg * p, pg * m)
    return w_in.astype(BF16), w_out.astype(BF16)


def _s5_scan_kernel(u_ref, wb_re_ref, wb_im_ref, a_re_ref, a_im_ref, c_re_ref, c_im_ref, y_ref,
                    x_re, x_im, state, *, reverse):
    rows = u_ref.shape[1]
    tokens = rows // S5_BATCH_TILE
    cin = S5_PANEL_GROUPS * S5_GROUP

    @pl.when(pl.program_id(1) == 0)
    def _():
        state[...] = jnp.zeros_like(state)

    for p in range(S5_PANELS):
        ub = u_ref[0, :, p * cin:(p + 1) * cin].astype(BF16)
        x_re[...] = jnp.dot(ub, wb_re_ref[p], preferred_element_type=F32)
        x_im[...] = jnp.dot(ub, wb_im_ref[p], preferred_element_type=F32)
        a_re, a_im = a_re_ref[p], a_im_ref[p]

        def step(i, carry):
            xr, xi = carry
            t = (tokens - 1 - i) if reverse else i
            r0 = pl.multiple_of(t * S5_BATCH_TILE, S5_BATCH_TILE)
            nr = a_re * xr - a_im * xi + x_re[pl.ds(r0, S5_BATCH_TILE), :]
            ni = a_re * xi + a_im * xr + x_im[pl.ds(r0, S5_BATCH_TILE), :]
            x_re[pl.ds(r0, S5_BATCH_TILE), :] = nr
            x_im[pl.ds(r0, S5_BATCH_TILE), :] = ni
            return nr, ni

        xr, xi = lax.fori_loop(0, tokens, step, (state[p, 0], state[p, 1]), unroll=2)
        state[p, 0] = xr
        state[p, 1] = xi
        y_ref[0, :, p * cin:(p + 1) * cin] = (
            jnp.dot(x_re[...].astype(BF16), c_re_ref[p], preferred_element_type=F32)
            - jnp.dot(x_im[...].astype(BF16), c_im_ref[p], preferred_element_type=F32))


def s5_scan(u_g, lam_re, lam_im, log_dt, b_re, b_im, c_re, c_im, n_ctx, reverse):
    nbg, rows_total, d = u_g.shape
    abar_re, abar_im, bb_re, bb_im = s5_discretize(lam_re, lam_im, log_dt, b_re, b_im)
    wb_re, cp_re = _s5_panels(bb_re, c_re.astype(F32))
    wb_im, cp_im = _s5_panels(bb_im, c_im.astype(F32))
    states = S5_PANEL_GROUPS * S5_STATE
    tile = lambda a: jnp.broadcast_to(a.reshape(S5_PANELS, 1, states), (S5_PANELS, S5_BATCH_TILE, states))
    rows = S5_SCAN_TOKENS * S5_BATCH_TILE
    n_chunks = rows_total // rows
    ctx_chunks = n_ctx // S5_SCAN_TOKENS
    if reverse:
        chunk = lambda s: jnp.where(s < ctx_chunks, ctx_chunks - 1 - s, n_chunks - 1 - (s - ctx_chunks))
    else:
        chunk = lambda s: s
    full = lambda a: pl.BlockSpec(a.shape, lambda i, s: (0,) * a.ndim)
    a_re_t, a_im_t = tile(abar_re), tile(abar_im)
    return pl.pallas_call(
        functools.partial(_s5_scan_kernel, reverse=reverse),
        grid=(nbg, n_chunks),
        in_specs=[pl.BlockSpec((1, rows, d), lambda i, s: (i, chunk(s), 0)),
                  full(wb_re), full(wb_im), full(a_re_t), full(a_im_t), full(cp_re), full(cp_im)],
        out_specs=pl.BlockSpec((1, rows, d), lambda i, s: (i, chunk(s), 0)),
        out_shape=jax.ShapeDtypeStruct(u_g.shape, F32),
        scratch_shapes=[pltpu.VMEM((rows, states), F32), pltpu.VMEM((rows, states), F32),
                        pltpu.VMEM((S5_PANELS, 2, S5_BATCH_TILE, states), F32)],
        compiler_params=pltpu.CompilerParams(dimension_semantics=("arbitrary", "arbitrary"),
                                             vmem_limit_bytes=MOSAIC_VMEM_LIMIT),
        name="s5_scan_bwd" if reverse else "s5_scan_fwd",
    )(u_g, wb_re, wb_im, a_re_t, a_im_t, cp_re, cp_im)


def _s5_glu_kernel(yf_ref, yb_ref, u_ref, d_ref, w_ref, o_ref):
    d = o_ref.shape[-1]
    y = yf_ref[0] + yb_ref[0] + d_ref[...] * u_ref[0]
    r = jnp.dot(_gelu_tanh(y).astype(BF16), w_ref[...], preferred_element_type=F32)
    o_ref[0] = r[:, :d] * jax.nn.sigmoid(r[:, d:])


def s5_glu(y_fw, y_bw, u_g, d_skip, w_glu, block_rows=512):
    nbg, rows_total, d = u_g.shape
    blk = pl.BlockSpec((1, block_rows, d), lambda i, j: (i, j, 0))
    return pl.pallas_call(
        _s5_glu_kernel,
        grid=(nbg, rows_total // block_rows),
        in_specs=[blk, blk, blk,
                  pl.BlockSpec((1, d), lambda i, j: (0, 0)),
                  pl.BlockSpec(w_glu.shape, lambda i, j: (0, 0))],
        out_specs=blk,
        out_shape=jax.ShapeDtypeStruct(u_g.shape, F32),
        compiler_params=pltpu.CompilerParams(vmem_limit_bytes=MOSAIC_VMEM_LIMIT),
        name="s5_glu",
    )(y_fw, y_bw, u_g, d_skip.reshape(1, d), w_glu.astype(BF16))


def s5_mixer(u_ctx, u_lat, lam_re, lam_im, log_dt, b_re, b_im, c_re, c_im, d_skip, w_glu, need_ctx):
    b_, n_ctx, d = u_ctx.shape
    n_all = n_ctx + u_lat.shape[1]
    bt = S5_BATCH_TILE
    u_all = jnp.concatenate([u_ctx, u_lat], axis=1)
    u_g = u_all.reshape(b_ // bt, bt, n_all, d).transpose(0, 2, 1, 3).reshape(b_ // bt, n_all * bt, d)
    ys = [s5_scan(u_g, lam_re[dr], lam_im[dr], log_dt[dr], b_re[dr], b_im[dr], c_re[dr], c_im[dr],
                  n_ctx, reverse=bool(dr)) for dr in range(2)]
    o_g = s5_glu(ys[0], ys[1], u_g, d_skip, w_glu)
    o = o_g.reshape(b_ // bt, n_all, bt, d).transpose(0, 2, 1, 3).reshape(b_, n_all, d)
    return o[:, n_ctx:], (o[:, :n_ctx] if need_ctx else None)


HG_SCAN_TOKENS = 256
HG_SUB = 32
HG_STEP_HEADS = 8


def _hgrn_scan_kernel(q_ref, f_ref, v_ref, lb_ref, tri_ref, o_ref, q_s, k_s, cum_s, v_s, st, *, reverse):
    tokens = q_ref.shape[1]

    @pl.when(pl.program_id(2) == 0)
    def _():
        st[...] = jnp.zeros_like(st)

    f = f_ref[0]
    log_f = jnp.logaddexp(lb_ref[0:1, :], lb_ref[1:2, :] + jax.nn.log_sigmoid(f))
    k_all = lb_ref[2:3, :] * jax.nn.sigmoid(-f)
    q_all = jax.nn.silu(q_ref[0])
    cum_all = jnp.dot(tri_ref[...], log_f, precision=lax.Precision.HIGHEST, preferred_element_type=F32)
    for hd in range(HG_STEP_HEADS):
        lanes = slice(hd * HG_KEY, (hd + 1) * HG_KEY)
        k_s[hd] = k_all[:, lanes]
        q_s[hd] = q_all[:, lanes]
        cum_s[hd] = cum_all[:, lanes]
        v_s[hd] = v_ref[0, :, lanes]
    tpos = lax.broadcasted_iota(jnp.int32, (HG_SUB, HG_KEY), 0)
    nsub = tokens // HG_SUB
    ones = jnp.ones((HG_KEY, HG_VAL), BF16)

    def head_block(base, hd):
        lanes = slice(hd * HG_KEY, (hd + 1) * HG_KEY)
        qc = q_s[hd, pl.ds(base, HG_SUB), :]
        cumc = cum_s[hd, pl.ds(base, HG_SUB), :]

        def column(s, o):
            ks = k_s[hd, pl.ds(base + s, 1), :]
            cs = cum_s[hd, pl.ds(base + s, 1), :]
            vs = v_s[hd, pl.ds(base + s, 1), :]
            seen = (tpos <= s) if reverse else (tpos >= s)
            decay = jnp.exp(jnp.where(seen, cumc - cs, -jnp.inf))
            att = jnp.dot((qc * ks * decay).astype(BF16), ones, preferred_element_type=F32)
            return o + att * vs

        o = lax.fori_loop(0, HG_SUB, column, jnp.zeros(qc.shape, F32), unroll=True)
        s_t = st[hd]
        o = o + lax.dot_general((qc * jnp.exp(cumc)).astype(BF16), s_t.astype(BF16),
                                (((1,), (1,)), ((), ())), preferred_element_type=F32)
        last = cum_s[hd, pl.ds(base + (0 if reverse else HG_SUB - 1), 1), :]
        kh = (k_s[hd, pl.ds(base, HG_SUB), :] * jnp.exp(last - cumc)).astype(BF16)
        vc = v_s[hd, pl.ds(base, HG_SUB), :].astype(BF16)
        st[hd] = s_t * jnp.exp(last) + lax.dot_general(vc, kh, (((0,), (0,)), ((), ())),
                                                       preferred_element_type=F32)
        o_ref[0, pl.ds(base, HG_SUB), lanes] = o

    def block(i, carry):
        c = (nsub - 1 - i) if reverse else i
        base = pl.multiple_of(c * HG_SUB, HG_SUB)
        for hd in range(HG_STEP_HEADS):
            head_block(base, hd)
        return carry

    lax.fori_loop(0, nsub, block, 0)


def hgrn_scan(proj, lb_rows, n_ctx, f_col, reverse):
    b_, n, _ = proj.shape
    t = HG_SCAN_TOKENS
    hk = HG_STEP_HEADS * HG_KEY
    steps = HG_HEADS // HG_STEP_HEADS
    n_chunks, ctx_chunks = n // t, n_ctx // t
    if reverse:
        chunk = lambda s: jnp.where(s < ctx_chunks, ctx_chunks - 1 - s, n_chunks - 1 - (s - ctx_chunks))
    else:
        chunk = lambda s: s
    pos = jnp.arange(t)
    same = (pos[:, None] // HG_SUB) == (pos[None, :] // HG_SUB)
    order = (pos[None, :] >= pos[:, None]) if reverse else (pos[None, :] <= pos[:, None])
    tri = (same & order).astype(F32)
    col = lambda off: pl.BlockSpec((1, t, hk), lambda i, h, s: (i, chunk(s), off * steps + h))
    return pl.pallas_call(
        functools.partial(_hgrn_scan_kernel, reverse=reverse),
        grid=(b_, steps, n_chunks),
        in_specs=[col(0), col(f_col), col(3),
                  pl.BlockSpec((3, hk), lambda i, h, s: (0, h)),
                  pl.BlockSpec((t, t), lambda i, h, s: (0, 0))],
        out_specs=pl.BlockSpec((1, t, hk), lambda i, h, s: (i, chunk(s), h)),
        out_shape=jax.ShapeDtypeStruct((b_, n, HG_HEADS * HG_VAL), F32),
        scratch_shapes=[pltpu.VMEM((HG_STEP_HEADS, t, HG_KEY), F32)] * 4
                       + [pltpu.VMEM((HG_STEP_HEADS, HG_VAL, HG_KEY), F32)],
        compiler_params=pltpu.CompilerParams(dimension_semantics=("arbitrary", "arbitrary", "arbitrary"),
                                             vmem_limit_bytes=MOSAIC_VMEM_LIMIT),
        name="hgrn_scan_bwd" if reverse else "hgrn_scan_fwd",
    )(proj, proj, proj, lb_rows, tri)


def _hgrn_out_kernel(of_ref, ob_ref, gate_ref, g_ref, w_ref, o_ref):
    o = of_ref[0] + ob_ref[0]
    parts = []
    for h in range(HG_HEADS):
        oh = o[:, h * HG_VAL:(h + 1) * HG_VAL]
        parts.append(oh * lax.rsqrt(jnp.mean(oh * oh, axis=-1, keepdims=True) + RMS_EPS))
    y = jnp.concatenate(parts, axis=1) * g_ref[...] * jax.nn.silu(gate_ref[0])
    o_ref[0] = jnp.dot(y.astype(BF16), w_ref[...], preferred_element_type=F32)


def hgrn_out(o_fw, o_bw, proj, norm_g, w_out, block_rows=256):
    b_, n, d = o_fw.shape
    assert n % block_rows == 0
    row = pl.BlockSpec((1, block_rows, d), lambda i, j: (i, j, 0))
    return pl.pallas_call(
        _hgrn_out_kernel,
        grid=(b_, n // block_rows),
        in_specs=[row, row, pl.BlockSpec((1, block_rows, d), lambda i, j: (i, j, 4)),
                  pl.BlockSpec((1, d), lambda i, j: (0, 0)), pl.BlockSpec(w_out.shape, lambda i, j: (0, 0))],
        out_specs=row,
        out_shape=jax.ShapeDtypeStruct((b_, n, d), F32),
        compiler_params=pltpu.CompilerParams(vmem_limit_bytes=MOSAIC_VMEM_LIMIT),
        name="hgrn_out",
    )(o_fw, o_bw, proj, norm_g.reshape(1, d), w_out.astype(BF16))


def hgrn2_mixer(u_ctx, u_lat, w_in, w_out, norm_g, lb, need_ctx):
    n_ctx = u_ctx.shape[1]
    proj = linear(jnp.concatenate([u_ctx, u_lat], axis=1), w_in)
    lb_rows = jnp.stack([jnp.log(lb), jnp.log1p(-lb), 1.0 - lb])
    o_fw = hgrn_scan(proj, lb_rows, n_ctx, 1, reverse=False)
    o_bw = hgrn_scan(proj, lb_rows, n_ctx, 2, reverse=True)
    o = hgrn_out(o_fw, o_bw, proj, norm_g, w_out)
    return o[:, n_ctx:], (o[:, :n_ctx] if need_ctx else None)


PEER_SLOTS = PEER_HEADS * PEER_TOPK
PEER_SEL_TOKENS = 256
PEER_MIX_TOKENS = 256
PEER_ROW_BUFFERS = 4


def _topk_axis0(cur, k, payload=None):
    rows = cur.shape[0]
    iota = lax.broadcasted_iota(jnp.int32, cur.shape, 0)
    vals, picks = [], []
    for _ in range(k):
        m = jnp.max(cur, axis=0, keepdims=True)
        pos = jnp.min(jnp.where(cur == m, iota, rows), axis=0, keepdims=True)
        hit = iota == pos
        vals.append(m)
        if payload is None:
            picks.append(pos)
        else:
            picks.append(jnp.sum(jnp.where(hit, payload, 0), axis=0, keepdims=True))
        cur = jnp.where(hit, -jnp.inf, cur)
    return jnp.concatenate(vals, axis=0), jnp.concatenate(picks, axis=0)


def _peer_select_kernel(h_ref, sc_ref, sh_ref, wq_ref, keys_ref, idx_ref, g_ref):
    half = PEER_QDIM // 2
    x = h_ref[0] * (1.0 + sc_ref[0]) + sh_ref[0]
    q = jnp.dot(x.astype(BF16), wq_ref[...], preferred_element_type=F32)
    tokens = x.shape[0]
    for c0 in range(0, tokens, LANES):
        idx_rows, g_rows = [], []
        for hd in range(PEER_HEADS):
            tops = []
            for c in range(2):
                lo = (hd * 2 + c) * half
                qhc = q[c0:c0 + LANES, lo:lo + half].astype(BF16)
                s_t = lax.dot_general(keys_ref[c], qhc, (((1,), (1,)), ((), ())),
                                      preferred_element_type=F32)
                tops.append(_topk_axis0(s_t, PEER_TOPK))
            (s1, i1), (s2, i2) = tops
            width = [PEER_TOPK // (a + 1) for a in range(PEER_TOPK)]
            pad = -sum(width) % 8
            cand_s = jnp.concatenate([s1[a:a + 1] + s2[:width[a]] for a in range(PEER_TOPK)]
                                     + [jnp.full((pad, LANES), -jnp.inf, F32)], axis=0)
            cand_i = jnp.concatenate([i1[a:a + 1] * PEER_NKEYS + i2[:width[a]] for a in range(PEER_TOPK)]
                                     + [jnp.zeros((pad, LANES), jnp.int32)], axis=0)
            top_s, top_i = _topk_axis0(cand_s, PEER_TOPK, payload=cand_i)
            e = jnp.exp(top_s - top_s[0:1])
            g_rows.append(e / jnp.sum(e, axis=0, keepdims=True))
            idx_rows.append(top_i)
        idx_ref[0, c0:c0 + LANES, :] = jnp.concatenate(idx_rows, axis=0).T
        g_ref[0, c0:c0 + LANES, :] = jnp.concatenate(g_rows, axis=0).T


def peer_select(h, sc, sh, w_q, sub_keys):
    b_, n, d = h.shape
    tb = min(PEER_SEL_TOKENS, n)
    mod_map = (lambda i, j: (i, 0, 0)) if sc.shape[0] == b_ else (lambda i, j: (0, 0, 0))
    return pl.pallas_call(
        _peer_select_kernel,
        grid=(b_, n // tb),
        in_specs=[pl.BlockSpec((1, tb, d), lambda i, j: (i, j, 0)),
                  pl.BlockSpec((1, 1, d), mod_map),
                  pl.BlockSpec((1, 1, d), mod_map),
                  pl.BlockSpec(w_q.shape, lambda i, j: (0, 0)),
                  pl.BlockSpec(sub_keys.shape, lambda i, j: (0, 0, 0))],
        out_specs=[pl.BlockSpec((1, tb, PEER_SLOTS), lambda i, j: (i, j, 0)),
                   pl.BlockSpec((1, tb, PEER_SLOTS), lambda i, j: (i, j, 0))],
        out_shape=[jax.ShapeDtypeStruct((b_, n, PEER_SLOTS), jnp.int32),
                   jax.ShapeDtypeStruct((b_, n, PEER_SLOTS), F32)],
        compiler_params=pltpu.CompilerParams(vmem_limit_bytes=48 * 1024 * 1024),
        name="peer_select",
    )(h, sc, sh, w_q.astype(BF16), sub_keys.astype(BF16))


def peer_table(u_tab, v_tab):
    bits = lambda t: lax.bitcast_convert_type(t.astype(BF16), jnp.uint16).astype(jnp.uint32)
    return ((bits(u_tab) << 16) | bits(v_tab))[:, None, :]


def _peer_mix_kernel(idx_hbm, h_ref, sc_ref, sh_ref, g_ref, tab_hbm, out_ref,
                     idx_smem, x_scr, rows, row_sem, idx_sem):
    tokens, d = x_scr.shape
    nbuf = PEER_ROW_BUFFERS
    ahead = nbuf - 1
    blk = pl.program_id(0) * pl.num_programs(1) + pl.program_id(1)
    per_blk = tokens * PEER_SLOTS
    idx_copy = pltpu.make_async_copy(idx_hbm.at[pl.ds(pl.multiple_of(blk * per_blk, per_blk), per_blk)],
                                     idx_smem, idx_sem)
    idx_copy.start()
    x_scr[...] = h_ref[0] * (1.0 + sc_ref[0]) + sh_ref[0]
    idx_copy.wait()

    def fetch(t, slot):
        for j in range(PEER_SLOTS):
            pltpu.make_async_copy(tab_hbm.at[idx_smem[t * PEER_SLOTS + j]],
                                  rows.at[slot, pl.ds(j, 1), :], row_sem.at[slot]).start(priority=j % 2)

    def wait_rows(slot):
        pltpu.make_async_copy(rows.at[slot], rows.at[slot], row_sem.at[slot]).wait()

    eye = (lax.broadcasted_iota(jnp.int32, (PEER_SLOTS, PEER_SLOTS), 0)
           == lax.broadcasted_iota(jnp.int32, (PEER_SLOTS, PEER_SLOTS), 1))

    def combine(t, slot):
        x_row = x_scr[pl.ds(t, 1), :]
        words = rows[slot]
        u_rows = lax.bitcast_convert_type(words & jnp.uint32(0xFFFF0000), F32)
        v_rows = lax.bitcast_convert_type(words << 16, F32)
        act = jnp.sum(u_rows * x_row, axis=1, keepdims=True)
        g_col = jnp.sum(jnp.where(eye, g_ref[0, pl.ds(t, 1), :], 0.0), axis=1, keepdims=True)
        w = g_col * _gelu_tanh(act)
        out_ref[0, pl.ds(t, 1), :] = jnp.sum(w * v_rows, axis=0, keepdims=True)

    def token(t, slot, prefetch):
        if prefetch:
            fetch(t + ahead, (slot + ahead) % nbuf)
        wait_rows(slot)
        combine(t, slot)

    for t0 in range(ahead):
        fetch(t0, t0)

    def group(i, carry):
        for slot in range(nbuf):
            token(i * nbuf + slot, slot, True)
        return carry

    lax.fori_loop(0, tokens // nbuf - 1, group, 0)
    for slot in range(nbuf):
        token(tokens - nbuf + slot, slot, slot + ahead < nbuf)


def peer_mix(h, sc, sh, idx, g, uv_tab):
    b_, n, d = h.shape
    tb = min(PEER_MIX_TOKENS, n)
    assert n % tb == 0 and tb % PEER_ROW_BUFFERS == 0
    mod_map = (lambda i, j: (i, 0, 0)) if sc.shape[0] == b_ else (lambda i, j: (0, 0, 0))
    return pl.pallas_call(
        _peer_mix_kernel,
        grid=(b_, n // tb),
        in_specs=[pl.BlockSpec(memory_space=pl.ANY),
                  pl.BlockSpec((1, tb, d), lambda i, j: (i, j, 0)),
                  pl.BlockSpec((1, 1, d), mod_map),
                  pl.BlockSpec((1, 1, d), mod_map),
                  pl.BlockSpec((1, tb, PEER_SLOTS), lambda i, j: (i, j, 0)),
                  pl.BlockSpec(memory_space=pl.ANY)],
        out_specs=pl.BlockSpec((1, tb, d), lambda i, j: (i, j, 0)),
        out_shape=jax.ShapeDtypeStruct(h.shape, F32),
        scratch_shapes=[pltpu.SMEM((tb * PEER_SLOTS,), jnp.int32),
                        pltpu.VMEM((tb, d), F32),
                        pltpu.VMEM((PEER_ROW_BUFFERS, PEER_SLOTS, d), jnp.uint32),
                        pltpu.SemaphoreType.DMA((PEER_ROW_BUFFERS,)),
                        pltpu.SemaphoreType.DMA(())],
        compiler_params=pltpu.CompilerParams(vmem_limit_bytes=32 * 1024 * 1024),
        name="peer_mix",
    )(idx.reshape(-1), h, sc, sh, g, uv_tab)


def peer_ffn(h, sc, sh, w_q, sub_keys, uv_tab):
    idx, g = peer_select(h, sc, sh, w_q, sub_keys)
    return peer_mix(h, sc, sh, idx, g, uv_tab)


def kernel(x, c, ctx, c_ctx, ada_w, ada_b, ln_g, ln_b, da_w_in, da_w_out, da_lam_q, da_lam_k, da_subln,
           s5_lam_re, s5_lam_im, s5_log_dt, s5_b_re, s5_b_im, s5_c_re, s5_c_im, s5_d, s5_w_glu,
           hg_w_in, hg_w_out, hg_norm, hg_lb, peer_wq, peer_keys, peer_u, peer_v):
    L = x.shape[1]
    cos, sin = axial_rope(L, DA_HEAD_DIM)
    s_c = jax.nn.silu(c)
    s_ctx = jax.nn.silu(c_ctx)
    lb_soft = jax.nn.softmax(hg_lb.astype(F32), axis=0)
    lb_all = jnp.cumsum(lb_soft, axis=0) - lb_soft[0]
    h, hc = x, ctx
    for i in range(DEPTH):
        kind, slot = LAYER_TYPES[i], i // N_MIXERS
        need_ctx = i < DEPTH - 1
        mod = (s_c @ ada_w[i] + ada_b[i])[:, None, :]
        mod_c = s_ctx @ ada_w[i] + ada_b[i]
        sh1, sc1, g1, sh2, sc2, g2 = jnp.split(mod, 6, axis=-1)
        csh1, csc1, cg1, csh2, csc2, cg2 = jnp.split(mod_c, 6, axis=-1)
        u = h * (1.0 + sc1) + sh1
        uc = hc * (1.0 + csc1) + csh1
        if kind == 0:
            lam_init = 0.8 - 0.6 * math.exp(-0.3 * i)
            o, oc = diff_attention(uc, u, da_w_in[slot], da_w_out[slot], da_lam_q[slot], da_lam_k[slot],
                                   da_subln[slot], lam_init, cos, sin, need_ctx)
        elif kind == 1:
            o, oc = s5_mixer(uc, u, s5_lam_re[slot], s5_lam_im[slot], s5_log_dt[slot], s5_b_re[slot],
                             s5_b_im[slot], s5_c_re[slot], s5_c_im[slot], s5_d[slot], s5_w_glu[slot], need_ctx)
        else:
            o, oc = hgrn2_mixer(uc, u, hg_w_in[slot], hg_w_out[slot], hg_norm[slot], lb_all[i], need_ctx)
        h = residual_layer_norm(h, o, g1, ln_g[i, 0], ln_b[i, 0])
        uv_tab = peer_table(peer_u[i], peer_v[i])
        f = peer_ffn(h, sc2, sh2, peer_wq[i], peer_keys[i], uv_tab)
        h = residual_layer_norm(h, f, g2, ln_g[i, 1], ln_b[i, 1])
        if need_ctx:
            hc = residual_layer_norm(hc, oc, cg1.reshape(1, 1, -1), ln_g[i, 0], ln_b[i, 0])
            fc = peer_ffn(hc, csc2.reshape(1, 1, -1), csh2.reshape(1, 1, -1), peer_wq[i], peer_keys[i], uv_tab)
            hc = residual_layer_norm(hc, fc, cg2.reshape(1, 1, -1), ln_g[i, 1], ln_b[i, 1])
    return h
```

```python
import functools
import math
import jax, jax.numpy as jnp
from jax import lax
import numpy as np
from jax.experimental import pallas as pl
from jax.experimental.pallas import tpu as pltpu

D_MODEL = 1024
BATCH = 32
SEQ = 2048
DEPTH = 4

CTX_LEN = 256
GRID_W = 64
N_MIXERS = 3
LAYER_TYPES = tuple(i % N_MIXERS for i in range(DEPTH))
N_ATTN = LAYER_TYPES.count(0)
N_S5 = LAYER_TYPES.count(1)
N_HG = LAYER_TYPES.count(2)

DA_HEADS = 8
DA_HEAD_DIM = 64
DA_V_DIM = 2 * DA_HEAD_DIM
Q_BLOCK = 128
ROPE_THETA = 10000.0
S5_GROUP = 16
S5_GROUPS = D_MODEL // S5_GROUP
S5_STATE = 64
HG_HEADS = 8
HG_KEY = D_MODEL // HG_HEADS
HG_VAL = D_MODEL // HG_HEADS
HG_CHUNK = 32
PEER_HEADS = 8
PEER_NKEYS = 128
PEER_EXPERTS = PEER_NKEYS * PEER_NKEYS
PEER_QDIM = 256
PEER_TOPK = 16
PEER_BLOCK = 128
LN_EPS = 1e-5
RMS_EPS = 1e-6
DN_ALPHA = (2 * DEPTH) ** 0.25
DN_BETA = (8 * DEPTH) ** -0.25

F32 = jnp.float32
BF16 = jnp.bfloat16
LANES = 128
MOSAIC_VMEM_LIMIT = 48 * 1024 * 1024


def _gelu_tanh(x):
    return 0.5 * x * (1.0 + jnp.tanh(math.sqrt(2.0 / math.pi) * (x + 0.044715 * (x * x * x))))


def _res_ln_kernel(h_ref, o_ref, gate_ref, g_ref, b_ref, out_ref):
    y = DN_ALPHA * h_ref[0] + gate_ref[0] * o_ref[0]
    mu = jnp.mean(y, -1, keepdims=True)
    yc = y - mu
    var = jnp.mean(yc * yc, -1, keepdims=True)
    out_ref[0] = yc * lax.rsqrt(var + LN_EPS) * g_ref[...] + b_ref[...]


def residual_layer_norm(h, o, gate, g, b, block_n=512):
    b_, n, d = h.shape
    bn = min(block_n, n)
    per_batch_gate = gate.shape[0] == b_
    gate_map = (lambda i, j: (i, 0, 0)) if per_batch_gate else (lambda i, j: (0, 0, 0))
    return pl.pallas_call(
        _res_ln_kernel,
        grid=(b_, n // bn),
        in_specs=[pl.BlockSpec((1, bn, d), lambda i, j: (i, j, 0)),
                  pl.BlockSpec((1, bn, d), lambda i, j: (i, j, 0)),
                  pl.BlockSpec((1, 1, d), gate_map),
                  pl.BlockSpec((1, d), lambda i, j: (0, 0)),
                  pl.BlockSpec((1, d), lambda i, j: (0, 0))],
        out_specs=pl.BlockSpec((1, bn, d), lambda i, j: (i, j, 0)),
        out_shape=jax.ShapeDtypeStruct(h.shape, h.dtype),
        name="residual_layer_norm",
    )(h, o, gate, g.reshape(1, d), b.reshape(1, d))


def axial_rope(length, dim):
    rows = length // GRID_W
    row = jnp.repeat(jnp.arange(rows, dtype=F32), GRID_W)
    col = jnp.tile(jnp.arange(GRID_W, dtype=F32), rows)
    n_freq = dim // 4
    inv = ROPE_THETA ** (-jnp.arange(n_freq, dtype=F32) / n_freq)
    ang = jnp.concatenate([row[:, None] * inv, col[:, None] * inv], axis=-1)
    return jnp.cos(ang), jnp.sin(ang)


def _linear_kernel(x_ref, w_ref, o_ref):
    o_ref[0] = jnp.dot(x_ref[0].astype(BF16), w_ref[...], preferred_element_type=F32).astype(o_ref.dtype)


def linear(x, w, out_dtype=F32, block_rows=256, block_cols=1024):
    b_, n, kdim = x.shape
    ncols = w.shape[1]
    br, bc = min(block_rows, n), min(block_cols, ncols)
    assert n % br == 0 and ncols % bc == 0
    return pl.pallas_call(
        _linear_kernel,
        grid=(b_, n // br, ncols // bc),
        in_specs=[pl.BlockSpec((1, br, kdim), lambda i, j, c: (i, j, 0)),
                  pl.BlockSpec((kdim, bc), lambda i, j, c: (0, c))],
        out_specs=pl.BlockSpec((1, br, bc), lambda i, j, c: (i, j, c)),
        out_shape=jax.ShapeDtypeStruct((b_, n, ncols), out_dtype),
        compiler_params=pltpu.CompilerParams(vmem_limit_bytes=MOSAIC_VMEM_LIMIT),
        name="linear",
    )(x, w.astype(BF16))


DA_Q_TOKENS = 256
DA_PROJ_TOKENS = 256


def _da_qkv_kernel(x_ref, w_ref, wsw_ref, cos_ref, sin_ref, q_ref, k_ref, v_ref):
    d = q_ref.shape[-1]
    x = x_ref[0].astype(BF16)
    reps = 2 * d // cos_ref.shape[-1]
    c = jnp.concatenate([cos_ref[...]] * reps, axis=1)
    s = jnp.concatenate([sin_ref[...]] * reps, axis=1)
    qk = (jnp.dot(x, w_ref[:, :2 * d], preferred_element_type=F32) * c
          + jnp.dot(x, wsw_ref[...], preferred_element_type=F32) * s)
    q_ref[0] = (qk[:, :d] * DA_HEAD_DIM ** -0.5).astype(BF16)
    k_ref[0] = qk[:, d:].astype(BF16)
    v_ref[0] = jnp.dot(x, w_ref[:, 2 * d:], preferred_element_type=F32).astype(BF16)


def da_qkv(u_all, w_in, cos_t, sin_t):
    b_, n, d = u_all.shape
    tb = DA_PROJ_TOKENS
    swap = jnp.arange(2 * d) ^ 1
    w = w_in.astype(BF16)
    w_sw = w[:, :2 * d][:, swap]
    row = pl.BlockSpec((1, tb, d), lambda i, j: (i, j, 0))
    tab = pl.BlockSpec((tb, cos_t.shape[1]), lambda i, j: (j, 0))
    return pl.pallas_call(
        _da_qkv_kernel,
        grid=(b_, n // tb),
        in_specs=[row, pl.BlockSpec(w.shape, lambda i, j: (0, 0)), pl.BlockSpec(w_sw.shape, lambda i, j: (0, 0)),
                  tab, tab],
        out_specs=[row, row, row],
        out_shape=[jax.ShapeDtypeStruct((b_, n, d), BF16)] * 3,
        compiler_params=pltpu.CompilerParams(vmem_limit_bytes=MOSAIC_VMEM_LIMIT),
        name="da_qkv",
    )(u_all, w, w_sw, cos_t, sin_t)


def _da_attn_kernel(lam_ref, q_ref, k_ref, v_ref, g_ref, o_ref, *, post_scale):
    q, k, v = q_ref[0], k_ref[0], v_ref[0]
    lane = lax.broadcasted_iota(jnp.int32, q.shape, 1)
    zero = jnp.zeros_like(q)
    contract_last = (((1,), (1,)), ((), ()))

    def softmax_parts(qm):
        s = lax.dot_general(qm, k, contract_last, preferred_element_type=F32)
        e = jnp.exp(s - jnp.max(s, axis=-1, keepdims=True))
        return e, jnp.sum(e, axis=-1, keepdims=True)

    e0, z0 = softmax_parts(jnp.where(lane < DA_HEAD_DIM, q, zero))
    e1, z1 = softmax_parts(jnp.where(lane >= DA_HEAD_DIM, q, zero))
    a = e0 * (1.0 / z0) - e1 * (lam_ref[0] / z1)
    o = jnp.dot(a.astype(BF16), v, preferred_element_type=F32)
    o = o * lax.rsqrt(jnp.mean(o * o, axis=-1, keepdims=True) + RMS_EPS)
    o_ref[0] = (o * g_ref[...] * post_scale).astype(o_ref.dtype)


def da_attend(q, k, v, lam, subln_g, post_scale, q_start, n_q, n_k):
    b_, _, d = q.shape
    hd = d // DA_HEADS
    tq = DA_Q_TOKENS
    q0 = q_start // tq
    return pl.pallas_call(
        functools.partial(_da_attn_kernel, post_scale=post_scale),
        grid=(b_, DA_HEADS, n_q // tq),
        in_specs=[pl.BlockSpec(memory_space=pltpu.SMEM),
                  pl.BlockSpec((1, tq, hd), lambda i, h, j: (i, q0 + j, h)),
                  pl.BlockSpec((1, n_k, hd), lambda i, h, j: (i, 0, h)),
                  pl.BlockSpec((1, n_k, hd), lambda i, h, j: (i, 0, h)),
                  pl.BlockSpec((1, hd), lambda i, h, j: (0, 0))],
        out_specs=pl.BlockSpec((1, tq, hd), lambda i, h, j: (i, j, h)),
        out_shape=jax.ShapeDtypeStruct((b_, n_q, d), BF16),
        compiler_params=pltpu.CompilerParams(vmem_limit_bytes=MOSAIC_VMEM_LIMIT),
        name="da_attend",
    )(lam.reshape(1), q, k, v, subln_g.reshape(1, hd))


def diff_attention(u_ctx, u_lat, w_in, w_out, lam_q, lam_k, subln_g, lam_init, cos, sin, need_ctx):
    b_, n_ctx, d = u_ctx.shape
    n_lat = u_lat.shape[1]
    lanes_cos = jnp.tile(jnp.repeat(cos, 2, axis=1), (1, 2))
    lanes_sin = jnp.tile(jnp.stack([-sin, sin], axis=-1).reshape(n_lat, -1), (1, 2))
    cos_t = jnp.concatenate([jnp.ones((n_ctx, lanes_cos.shape[1]), F32), lanes_cos], axis=0)
    sin_t = jnp.concatenate([jnp.zeros((n_ctx, lanes_sin.shape[1]), F32), lanes_sin], axis=0)
    u_all = jnp.concatenate([u_ctx, u_lat], axis=1)
    q, k, v = da_qkv(u_all, w_in, cos_t, sin_t)
    lq, lk = lam_q.astype(F32), lam_k.astype(F32)
    lam = jnp.exp(jnp.sum(lq[0] * lk[0])) - jnp.exp(jnp.sum(lq[1] * lk[1])) + lam_init
    post = 1.0 - lam_init
    o_lat = linear(da_attend(q, k, v, lam, subln_g, post, n_ctx, n_lat, n_ctx + n_lat), w_out)
    o_ctx = linear(da_attend(q, k, v, lam, subln_g, post, 0, n_ctx, n_ctx), w_out) if need_ctx else None
    return o_lat, o_ctx


def s5_discretize(lam_re, lam_im, log_dt, b_re, b_im):
    lam_re, lam_im = lam_re.astype(F32), lam_im.astype(F32)
    b_re, b_im = b_re.astype(F32), b_im.astype(F32)
    dt = jnp.exp(log_dt.astype(F32))[:, None]
    mag = jnp.exp(lam_re * dt)
    abar_re, abar_im = mag * jnp.cos(lam_im * dt), mag * jnp.sin(lam_im * dt)
    nr, ni = abar_re - 1.0, abar_im
    den = lam_re * lam_re + lam_im * lam_im
    k_re = (nr * lam_re + ni * lam_im) / den
    k_im = (ni * lam_re - nr * lam_im) / den
    bb_re = k_re[..., None] * b_re - k_im[..., None] * b_im
    bb_im = k_re[..., None] * b_im + k_im[..., None] * b_re
    return abar_re, abar_im, bb_re, bb_im


S5_BATCH_TILE = 8
S5_SCAN_TOKENS = 128
S5_PANEL_GROUPS = 16
S5_PANELS = S5_GROUPS // S5_PANEL_GROUPS


def _s5_panels(bb, c):
    g, p, m = bb.shape
    pg = S5_PANEL_GROUPS
    eye = jnp.eye(pg, dtype=bb.dtype)
    w_in = jnp.einsum('qgpm,gh->qgmhp', bb.reshape(g // pg, pg, p, m), eye).reshape(g // pg, pg * m, pg * p)
    w_out = jnp.einsum('qgmp,gh->qgphm', c.reshape(g // pg, pg, m, p), eye).reshape(g // pg, pg * p, pg * m)
    return w_in.astype(BF16), w_out.astype(BF16)


def _s5_scan_kernel(u_ref, wb_re_ref, wb_im_ref, a_re_ref, a_im_ref, c_re_ref, c_im_ref, y_ref,
                    x_re, x_im, state, *, reverse):
    rows = u_ref.shape[1]
    tokens = rows // S5_BATCH_TILE
    cin = S5_PANEL_GROUPS * S5_GROUP

    @pl.when(pl.program_id(1) == 0)
    def _():
        state[...] = jnp.zeros_like(state)

    for p in range(S5_PANELS):
        ub = u_ref[0, :, p * cin:(p + 1) * cin].astype(BF16)
        x_re[...] = jnp.dot(ub, wb_re_ref[p], preferred_element_type=F32)
        x_im[...] = jnp.dot(ub, wb_im_ref[p], preferred_element_type=F32)
        a_re, a_im = a_re_ref[p], a_im_ref[p]

        def step(i, carry):
            xr, xi = carry
            t = (tokens - 1 - i) if reverse else i
            r0 = pl.multiple_of(t * S5_BATCH_TILE, S5_BATCH_TILE)
            nr = a_re * xr - a_im * xi + x_re[pl.ds(r0, S5_BATCH_TILE), :]
            ni = a_re * xi + a_im * xr + x_im[pl.ds(r0, S5_BATCH_TILE), :]
            x_re[pl.ds(r0, S5_BATCH_TILE), :] = nr
            x_im[pl.ds(r0, S5_BATCH_TILE), :] = ni
            return nr, ni

        xr, xi = lax.fori_loop(0, tokens, step, (state[p, 0], state[p, 1]), unroll=2)
        state[p, 0] = xr
        state[p, 1] = xi
        y_ref[0, :, p * cin:(p + 1) * cin] = (
            jnp.dot(x_re[...].astype(BF16), c_re_ref[p], preferred_element_type=F32)
            - jnp.dot(x_im[...].astype(BF16), c_im_ref[p], preferred_element_type=F32))


def s5_scan(u_g, lam_re, lam_im, log_dt, b_re, b_im, c_re, c_im, n_ctx, reverse):
    nbg, rows_total, d = u_g.shape
    abar_re, abar_im, bb_re, bb_im = s5_discretize(lam_re, lam_im, log_dt, b_re, b_im)
    wb_re, cp_re = _s5_panels(bb_re, c_re.astype(F32))
    wb_im, cp_im = _s5_panels(bb_im, c_im.astype(F32))
    states = S5_PANEL_GROUPS * S5_STATE
    tile = lambda a: jnp.broadcast_to(a.reshape(S5_PANELS, 1, states), (S5_PANELS, S5_BATCH_TILE, states))
    rows = S5_SCAN_TOKENS * S5_BATCH_TILE
    n_chunks = rows_total // rows
    ctx_chunks = n_ctx // S5_SCAN_TOKENS
    if reverse:
        chunk = lambda s: jnp.where(s < ctx_chunks, ctx_chunks - 1 - s, n_chunks - 1 - (s - ctx_chunks))
    else:
        chunk = lambda s: s
    full = lambda a: pl.BlockSpec(a.shape, lambda i, s: (0,) * a.ndim)
    a_re_t, a_im_t = tile(abar_re), tile(abar_im)
    return pl.pallas_call(
        functools.partial(_s5_scan_kernel, reverse=reverse),
        grid=(nbg, n_chunks),
        in_specs=[pl.BlockSpec((1, rows, d), lambda i, s: (i, chunk(s), 0)),
                  full(wb_re), full(wb_im), full(a_re_t), full(a_im_t), full(cp_re), full(cp_im)],
        out_specs=pl.BlockSpec((1, rows, d), lambda i, s: (i, chunk(s), 0)),
        out_shape=jax.ShapeDtypeStruct(u_g.shape, F32),
        scratch_shapes=[pltpu.VMEM((rows, states), F32), pltpu.VMEM((rows, states), F32),
                        pltpu.VMEM((S5_PANELS, 2, S5_BATCH_TILE, states), F32)],
        compiler_params=pltpu.CompilerParams(dimension_semantics=("arbitrary", "arbitrary"),
                                             vmem_limit_bytes=MOSAIC_VMEM_LIMIT),
        name="s5_scan_bwd" if reverse else "s5_scan_fwd",
    )(u_g, wb_re, wb_im, a_re_t, a_im_t, cp_re, cp_im)


def _s5_glu_kernel(yf_ref, yb_ref, u_ref, d_ref, w_ref, o_ref):
    d = o_ref.shape[-1]
    y = yf_ref[0] + yb_ref[0] + d_ref[...] * u_ref[0]
    r = jnp.dot(_gelu_tanh(y).astype(BF16), w_ref[...], preferred_element_type=F32)
    o_ref[0] = r[:, :d] * jax.nn.sigmoid(r[:, d:])


def s5_glu(y_fw, y_bw, u_g, d_skip, w_glu, block_rows=512):
    nbg, rows_total, d = u_g.shape
    blk = pl.BlockSpec((1, block_rows, d), lambda i, j: (i, j, 0))
    return pl.pallas_call(
        _s5_glu_kernel,
        grid=(nbg, rows_total // block_rows),
        in_specs=[blk, blk, blk,
                  pl.BlockSpec((1, d), lambda i, j: (0, 0)),
                  pl.BlockSpec(w_glu.shape, lambda i, j: (0, 0))],
        out_specs=blk,
        out_shape=jax.ShapeDtypeStruct(u_g.shape, F32),
        compiler_params=pltpu.CompilerParams(vmem_limit_bytes=MOSAIC_VMEM_LIMIT),
        name="s5_glu",
    )(y_fw, y_bw, u_g, d_skip.reshape(1, d), w_glu.astype(BF16))


def s5_mixer(u_ctx, u_lat, lam_re, lam_im, log_dt, b_re, b_im, c_re, c_im, d_skip, w_glu, need_ctx):
    b_, n_ctx, d = u_ctx.shape
    n_all = n_ctx + u_lat.shape[1]
    bt = S5_BATCH_TILE
    u_all = jnp.concatenate([u_ctx, u_lat], axis=1)
    u_g = u_all.reshape(b_ // bt, bt, n_all, d).transpose(0, 2, 1, 3).reshape(b_ // bt, n_all * bt, d)
    ys = [s5_scan(u_g, lam_re[dr], lam_im[dr], log_dt[dr], b_re[dr], b_im[dr], c_re[dr], c_im[dr],
                  n_ctx, reverse=bool(dr)) for dr in range(2)]
    o_g = s5_glu(ys[0], ys[1], u_g, d_skip, w_glu)
    o = o_g.reshape(b_ // bt, n_all, bt, d).transpose(0, 2, 1, 3).reshape(b_, n_all, d)
    return o[:, n_ctx:], (o[:, :n_ctx] if need_ctx else None)


HG_SCAN_TOKENS = 256
HG_SUB = 32
HG_STEP_HEADS = 8


def _hgrn_scan_kernel(q_ref, f_ref, v_ref, lb_ref, tri_ref, o_ref, q_s, k_s, cum_s, v_s, st, *, reverse):
    tokens = q_ref.shape[1]

    @pl.when(pl.program_id(2) == 0)
    def _():
        st[...] = jnp.zeros_like(st)

    f = f_ref[0]
    log_f = jnp.logaddexp(lb_ref[0:1, :], lb_ref[1:2, :] + jax.nn.log_sigmoid(f))
    k_all = lb_ref[2:3, :] * jax.nn.sigmoid(-f)
    q_all = jax.nn.silu(q_ref[0])
    cum_all = jnp.dot(tri_ref[...], log_f, precision=lax.Precision.HIGHEST, preferred_element_type=F32)
    for hd in range(HG_STEP_HEADS):
        lanes = slice(hd * HG_KEY, (hd + 1) * HG_KEY)
        k_s[hd] = k_all[:, lanes]
        q_s[hd] = q_all[:, lanes]
        cum_s[hd] = cum_all[:, lanes]
        v_s[hd] = v_ref[0, :, lanes]
    tpos = lax.broadcasted_iota(jnp.int32, (HG_SUB, HG_KEY), 0)
    nsub = tokens // HG_SUB
    ones = jnp.ones((HG_KEY, HG_VAL), BF16)

    def head_block(base, hd):
        lanes = slice(hd * HG_KEY, (hd + 1) * HG_KEY)
        qc = q_s[hd, pl.ds(base, HG_SUB), :]
        cumc = cum_s[hd, pl.ds(base, HG_SUB), :]

        def column(s, o):
            ks = k_s[hd, pl.ds(base + s, 1), :]
            cs = cum_s[hd, pl.ds(base + s, 1), :]
            vs = v_s[hd, pl.ds(base + s, 1), :]
            seen = (tpos <= s) if reverse else (tpos >= s)
            decay = jnp.exp(jnp.where(seen, cumc - cs, -jnp.inf))
            att = jnp.dot((qc * ks * decay).astype(BF16), ones, preferred_element_type=F32)
            return o + att * vs

        o = lax.fori_loop(0, HG_SUB, column, jnp.zeros(qc.shape, F32), unroll=True)
        s_t = st[hd]
        o = o + lax.dot_general((qc * jnp.exp(cumc)).astype(BF16), s_t.astype(BF16),
                                (((1,), (1,)), ((), ())), preferred_element_type=F32)
        last = cum_s[hd, pl.ds(base + (0 if reverse else HG_SUB - 1), 1), :]
        kh = (k_s[hd, pl.ds(base, HG_SUB), :] * jnp.exp(last - cumc)).astype(BF16)
        vc = v_s[hd, pl.ds(base, HG_SUB), :].astype(BF16)
        st[hd] = s_t * jnp.exp(last) + lax.dot_general(vc, kh, (((0,), (0,)), ((), ())),
                                                       preferred_element_type=F32)
        o_ref[0, pl.ds(base, HG_SUB), lanes] = o

    def block(i, carry):
        c = (nsub - 1 - i) if reverse else i
        base = pl.multiple_of(c * HG_SUB, HG_SUB)
        for hd in range(HG_STEP_HEADS):
            head_block(base, hd)
        return carry

    lax.fori_loop(0, nsub, block, 0)


def hgrn_scan(proj, lb_rows, n_ctx, f_col, reverse):
    b_, n, _ = proj.shape
    t = HG_SCAN_TOKENS
    hk = HG_STEP_HEADS * HG_KEY
    steps = HG_HEADS // HG_STEP_HEADS
    n_chunks, ctx_chunks = n // t, n_ctx // t
    if reverse:
        chunk = lambda s: jnp.where(s < ctx_chunks, ctx_chunks - 1 - s, n_chunks - 1 - (s - ctx_chunks))
    else:
        chunk = lambda s: s
    pos = jnp.arange(t)
    same = (pos[:, None] // HG_SUB) == (pos[None, :] // HG_SUB)
    order = (pos[None, :] >= pos[:, None]) if reverse else (pos[None, :] <= pos[:, None])
    tri = (same & order).astype(F32)
    col = lambda off: pl.BlockSpec((1, t, hk), lambda i, h, s: (i, chunk(s), off * steps + h))
    return pl.pallas_call(
        functools.partial(_hgrn_scan_kernel, reverse=reverse),
        grid=(b_, steps, n_chunks),
        in_specs=[col(0), col(f_col), col(3),
                  pl.BlockSpec((3, hk), lambda i, h, s: (0, h)),
                  pl.BlockSpec((t, t), lambda i, h, s: (0, 0))],
        out_specs=pl.BlockSpec((1, t, hk), lambda i, h, s: (i, chunk(s), h)),
        out_shape=jax.ShapeDtypeStruct((b_, n, HG_HEADS * HG_VAL), F32),
        scratch_shapes=[pltpu.VMEM((HG_STEP_HEADS, t, HG_KEY), F32)] * 4
                       + [pltpu.VMEM((HG_STEP_HEADS, HG_VAL, HG_KEY), F32)],
        compiler_params=pltpu.CompilerParams(dimension_semantics=("arbitrary", "arbitrary", "arbitrary"),
                                             vmem_limit_bytes=MOSAIC_VMEM_LIMIT),
        name="hgrn_scan_bwd" if reverse else "hgrn_scan_fwd",
    )(proj, proj, proj, lb_rows, tri)


def _hgrn_out_kernel(of_ref, ob_ref, gate_ref, g_ref, w_ref, o_ref):
    o = of_ref[0] + ob_ref[0]
    parts = []
    for h in range(HG_HEADS):
        oh = o[:, h * HG_VAL:(h + 1) * HG_VAL]
        parts.append(oh * lax.rsqrt(jnp.mean(oh * oh, axis=-1, keepdims=True) + RMS_EPS))
    y = jnp.concatenate(parts, axis=1) * g_ref[...] * jax.nn.silu(gate_ref[0])
    o_ref[0] = jnp.dot(y.astype(BF16), w_ref[...], preferred_element_type=F32)


def hgrn_out(o_fw, o_bw, proj, norm_g, w_out, block_rows=256):
    b_, n, d = o_fw.shape
    assert n % block_rows == 0
    row = pl.BlockSpec((1, block_rows, d), lambda i, j: (i, j, 0))
    return pl.pallas_call(
        _hgrn_out_kernel,
        grid=(b_, n // block_rows),
        in_specs=[row, row, pl.BlockSpec((1, block_rows, d), lambda i, j: (i, j, 4)),
                  pl.BlockSpec((1, d), lambda i, j: (0, 0)), pl.BlockSpec(w_out.shape, lambda i, j: (0, 0))],
        out_specs=row,
        out_shape=jax.ShapeDtypeStruct((b_, n, d), F32),
        compiler_params=pltpu.CompilerParams(vmem_limit_bytes=MOSAIC_VMEM_LIMIT),
        name="hgrn_out",
    )(o_fw, o_bw, proj, norm_g.reshape(1, d), w_out.astype(BF16))


def hgrn2_mixer(u_ctx, u_lat, w_in, w_out, norm_g, lb, need_ctx):
    n_ctx = u_ctx.shape[1]
    proj = linear(jnp.concatenate([u_ctx, u_lat], axis=1), w_in)
    lb_rows = jnp.stack([jnp.log(lb), jnp.log1p(-lb), 1.0 - lb])
    o_fw = hgrn_scan(proj, lb_rows, n_ctx, 1, reverse=False)
    o_bw = hgrn_scan(proj, lb_rows, n_ctx, 2, reverse=True)
    o = hgrn_out(o_fw, o_bw, proj, norm_g, w_out)
    return o[:, n_ctx:], (o[:, :n_ctx] if need_ctx else None)


PEER_SLOTS = PEER_HEADS * PEER_TOPK
PEER_SEL_TOKENS = 256
PEER_MIX_TOKENS = 256
PEER_ROW_BUFFERS = 4


def _topk_axis0(cur, k, payload=None):
    rows = cur.shape[0]
    iota = lax.broadcasted_iota(jnp.int32, cur.shape, 0)
    vals, picks = [], []
    for _ in range(k):
        m = jnp.max(cur, axis=0, keepdims=True)
        pos = jnp.min(jnp.where(cur == m, iota, rows), axis=0, keepdims=True)
        hit = iota == pos
        vals.append(m)
        if payload is None:
            picks.append(pos)
        else:
            picks.append(jnp.sum(jnp.where(hit, payload, 0), axis=0, keepdims=True))
        cur = jnp.where(hit, -jnp.inf, cur)
    return jnp.concatenate(vals, axis=0), jnp.concatenate(picks, axis=0)


def _peer_select_kernel(h_ref, sc_ref, sh_ref, wq_ref, keys_ref, idx_ref, g_ref):
    half = PEER_QDIM // 2
    x = h_ref[0] * (1.0 + sc_ref[0]) + sh_ref[0]
    q = jnp.dot(x.astype(BF16), wq_ref[...], preferred_element_type=F32)
    tokens = x.shape[0]
    for c0 in range(0, tokens, LANES):
        idx_rows, g_rows = [], []
        for hd in range(PEER_HEADS):
            tops = []
            for c in range(2):
                lo = (hd * 2 + c) * half
                qhc = q[c0:c0 + LANES, lo:lo + half].astype(BF16)
                s_t = lax.dot_general(keys_ref[c], qhc, (((1,), (1,)), ((), ())),
                                      preferred_element_type=F32)
                tops.append(_topk_axis0(s_t, PEER_TOPK))
            (s1, i1), (s2, i2) = tops
            width = [PEER_TOPK // (a + 1) for a in range(PEER_TOPK)]
            pad = -sum(width) % 8
            cand_s = jnp.concatenate([s1[a:a + 1] + s2[:width[a]] for a in range(PEER_TOPK)]
                                     + [jnp.full((pad, LANES), -jnp.inf, F32)], axis=0)
            cand_i = jnp.concatenate([i1[a:a + 1] * PEER_NKEYS + i2[:width[a]] for a in range(PEER_TOPK)]
                                     + [jnp.zeros((pad, LANES), jnp.int32)], axis=0)
            top_s, top_i = _topk_axis0(cand_s, PEER_TOPK, payload=cand_i)
            e = jnp.exp(top_s - top_s[0:1])
            g_rows.append(e / jnp.sum(e, axis=0, keepdims=True))
            idx_rows.append(top_i)
        idx_ref[0, c0:c0 + LANES, :] = jnp.concatenate(idx_rows, axis=0).T
        g_ref[0, c0:c0 + LANES, :] = jnp.concatenate(g_rows, axis=0).T


def peer_select(h, sc, sh, w_q, sub_keys):
    b_, n, d = h.shape
    tb = min(PEER_SEL_TOKENS, n)
    mod_map = (lambda i, j: (i, 0, 0)) if sc.shape[0] == b_ else (lambda i, j: (0, 0, 0))
    return pl.pallas_call(
        _peer_select_kernel,
        grid=(b_, n // tb),
        in_specs=[pl.BlockSpec((1, tb, d), lambda i, j: (i, j, 0)),
                  pl.BlockSpec((1, 1, d), mod_map),
                  pl.BlockSpec((1, 1, d), mod_map),
                  pl.BlockSpec(w_q.shape, lambda i, j: (0, 0)),
                  pl.BlockSpec(sub_keys.shape, lambda i, j: (0, 0, 0))],
        out_specs=[pl.BlockSpec((1, tb, PEER_SLOTS), lambda i, j: (i, j, 0)),
                   pl.BlockSpec((1, tb, PEER_SLOTS), lambda i, j: (i, j, 0))],
        out_shape=[jax.ShapeDtypeStruct((b_, n, PEER_SLOTS), jnp.int32),
                   jax.ShapeDtypeStruct((b_, n, PEER_SLOTS), F32)],
        compiler_params=pltpu.CompilerParams(vmem_limit_bytes=48 * 1024 * 1024),
        name="peer_select",
    )(h, sc, sh, w_q.astype(BF16), sub_keys.astype(BF16))


def peer_table(u_tab, v_tab):
    bits = lambda t: lax.bitcast_convert_type(t.astype(BF16), jnp.uint16).astype(jnp.uint32)
    return ((bits(u_tab) << 16) | bits(v_tab))[:, None, :]


def _peer_mix_kernel(idx_hbm, h_ref, sc_ref, sh_ref, g_ref, tab_hbm, out_ref,
                     idx_smem, x_scr, rows, row_sem, idx_sem):
    tokens, d = x_scr.shape
    nbuf = PEER_ROW_BUFFERS
    ahead = nbuf - 1
    blk = pl.program_id(0) * pl.num_programs(1) + pl.program_id(1)
    per_blk = tokens * PEER_SLOTS
    idx_copy = pltpu.make_async_copy(idx_hbm.at[pl.ds(pl.multiple_of(blk * per_blk, per_blk), per_blk)],
                                     idx_smem, idx_sem)
    idx_copy.start()
    x_scr[...] = h_ref[0] * (1.0 + sc_ref[0]) + sh_ref[0]
    idx_copy.wait()

    def fetch(t, slot):
        for j in range(PEER_SLOTS):
            pltpu.make_async_copy(tab_hbm.at[idx_smem[t * PEER_SLOTS + j]],
                                  rows.at[slot, pl.ds(j, 1), :], row_sem.at[slot]).start(priority=j % 2)

    def wait_rows(slot):
        pltpu.make_async_copy(rows.at[slot], rows.at[slot], row_sem.at[slot]).wait()

    eye = (lax.broadcasted_iota(jnp.int32, (PEER_SLOTS, PEER_SLOTS), 0)
           == lax.broadcasted_iota(jnp.int32, (PEER_SLOTS, PEER_SLOTS), 1))

    def combine(t, slot):
        x_row = x_scr[pl.ds(t, 1), :]
        words = rows[slot]
        u_rows = lax.bitcast_convert_type(words & jnp.uint32(0xFFFF0000), F32)
        v_rows = lax.bitcast_convert_type(words << 16, F32)
        act = jnp.sum(u_rows * x_row, axis=1, keepdims=True)
        g_col = jnp.sum(jnp.where(eye, g_ref[0, pl.ds(t, 1), :], 0.0), axis=1, keepdims=True)
        w = g_col * _gelu_tanh(act)
        out_ref[0, pl.ds(t, 1), :] = jnp.sum(w * v_rows, axis=0, keepdims=True)

    def token(t, slot, prefetch):
        wait_rows(slot)
        combine(t, slot)
        if prefetch:
            fetch(t + nbuf, slot)

    for t0 in range(nbuf):
        fetch(t0, t0)

    def group(i, carry):
        for slot in range(nbuf):
            token(i * nbuf + slot, slot, True)
        return carry

    lax.fori_loop(0, tokens // nbuf - 1, group, 0)
    for slot in range(nbuf):
        token(tokens - nbuf + slot, slot, False)


def peer_mix(h, sc, sh, idx, g, uv_tab):
    b_, n, d = h.shape
    tb = min(PEER_MIX_TOKENS, n)
    assert n % tb == 0 and tb % PEER_ROW_BUFFERS == 0
    mod_map = (lambda i, j: (i, 0, 0)) if sc.shape[0] == b_ else (lambda i, j: (0, 0, 0))
    return pl.pallas_call(
        _peer_mix_kernel,
        grid=(b_, n // tb),
        in_specs=[pl.BlockSpec(memory_space=pl.ANY),
                  pl.BlockSpec((1, tb, d), lambda i, j: (i, j, 0)),
                  pl.BlockSpec((1, 1, d), mod_map),
                  pl.BlockSpec((1, 1, d), mod_map),
                  pl.BlockSpec((1, tb, PEER_SLOTS), lambda i, j: (i, j, 0)),
                  pl.BlockSpec(memory_space=pl.ANY)],
        out_specs=pl.BlockSpec((1, tb, d), lambda i, j: (i, j, 0)),
        out_shape=jax.ShapeDtypeStruct(h.shape, F32),
        scratch_shapes=[pltpu.SMEM((tb * PEER_SLOTS,), jnp.int32),
                        pltpu.VMEM((tb, d), F32),
                        pltpu.VMEM((PEER_ROW_BUFFERS, PEER_SLOTS, d), jnp.uint32),
                        pltpu.SemaphoreType.DMA((PEER_ROW_BUFFERS,)),
                        pltpu.SemaphoreType.DMA(())],
        compiler_params=pltpu.CompilerParams(vmem_limit_bytes=32 * 1024 * 1024),
        name="peer_mix",
    )(idx.reshape(-1), h, sc, sh, g, uv_tab)


def peer_ffn(h, sc, sh, w_q, sub_keys, uv_tab):
    idx, g = peer_select(h, sc, sh, w_q, sub_keys)
    return peer_mix(h, sc, sh, idx, g, uv_tab)


def kernel(x, c, ctx, c_ctx, ada_w, ada_b, ln_g, ln_b, da_w_in, da_w_out, da_lam_q, da_lam_k, da_subln,
           s5_lam_re, s5_lam_im, s5_log_dt, s5_b_re, s5_b_im, s5_c_re, s5_c_im, s5_d, s5_w_glu,
           hg_w_in, hg_w_out, hg_norm, hg_lb, peer_wq, peer_keys, peer_u, peer_v):
    L = x.shape[1]
    cos, sin = axial_rope(L, DA_HEAD_DIM)
    s_c = jax.nn.silu(c)
    s_ctx = jax.nn.silu(c_ctx)
    lb_soft = jax.nn.softmax(hg_lb.astype(F32), axis=0)
    lb_all = jnp.cumsum(lb_soft, axis=0) - lb_soft[0]
    h, hc = x, ctx
    for i in range(DEPTH):
        kind, slot = LAYER_TYPES[i], i // N_MIXERS
        need_ctx = i < DEPTH - 1
        mod = (s_c @ ada_w[i] + ada_b[i])[:, None, :]
        mod_c = s_ctx @ ada_w[i] + ada_b[i]
        sh1, sc1, g1, sh2, sc2, g2 = jnp.split(mod, 6, axis=-1)
        csh1, csc1, cg1, csh2, csc2, cg2 = jnp.split(mod_c, 6, axis=-1)
        u = h * (1.0 + sc1) + sh1
        uc = hc * (1.0 + csc1) + csh1
        if kind == 0:
            lam_init = 0.8 - 0.6 * math.exp(-0.3 * i)
            o, oc = diff_attention(uc, u, da_w_in[slot], da_w_out[slot], da_lam_q[slot], da_lam_k[slot],
                                   da_subln[slot], lam_init, cos, sin, need_ctx)
        elif kind == 1:
            o, oc = s5_mixer(uc, u, s5_lam_re[slot], s5_lam_im[slot], s5_log_dt[slot], s5_b_re[slot],
                             s5_b_im[slot], s5_c_re[slot], s5_c_im[slot], s5_d[slot], s5_w_glu[slot], need_ctx)
        else:
            o, oc = hgrn2_mixer(uc, u, hg_w_in[slot], hg_w_out[slot], hg_norm[slot], lb_all[i], need_ctx)
        h = residual_layer_norm(h, o, g1, ln_g[i, 0], ln_b[i, 0])
        uv_tab = peer_table(peer_u[i], peer_v[i])
        f = peer_ffn(h, sc2, sh2, peer_wq[i], peer_keys[i], uv_tab)
        h = residual_layer_norm(h, f, g2, ln_g[i, 1], ln_b[i, 1])
        if need_ctx:
            hc = residual_layer_norm(hc, oc, cg1.reshape(1, 1, -1), ln_g[i, 0], ln_b[i, 0])
            fc = peer_ffn(hc, csc2.reshape(1, 1, -1), csh2.reshape(1, 1, -1), peer_wq[i], peer_keys[i], uv_tab)
            hc = residual_layer_norm(hc, fc, cg2.reshape(1, 1, -1), ln_g[i, 1], ln_b[i, 1])
    return h
```
